```python
import jax, jax.numpy as jnp
from jax import lax
import numpy as np

D_MODEL = 1024
BATCH = 8
SEQ = 2048
DEPTH = 4

GRID_W = 64
CTX_LEN = 256
N_MIXERS = 3
HEAD_DIM = 64
N_HEADS = D_MODEL // HEAD_DIM
MLA_Q_LORA = 3 * D_MODEL // 8
MLA_KV_LORA = D_MODEL // 4
MLA_QK_NOPE = 64
MLA_QK_ROPE = 32
MLA_V_DIM = 64
SWA_KV_HEADS = N_HEADS // 4
SWA_WINDOW = 128
GA_KV_HEADS = N_HEADS // 2
Q_BLOCK = 128
D_FF = ((8 * D_MODEL // 3 + 127) // 128) * 128
N_EXPERTS = 8
TOP_K = 2
EXPERT_FF = D_FF // 2
ROPE_THETA = 10000.0
NORM_EPS = 1e-6
NEG_INF = -1e30

kernel_name = "hybrid_mla_swa_axialgqa_moe_dit"


def _rmsnorm(x, g):
    xf = x.astype(jnp.float32)
    y = xf * lax.rsqrt(jnp.mean(xf * xf, axis=-1, keepdims=True) + NORM_EPS)
    return (y * g.astype(jnp.float32)).astype(x.dtype)


def _modulate(h, shift, scale):
    return h * (1 + scale) + shift


def _axial_rope_tables(rows_n, rot_dim):
    n_freq = rot_dim // 4
    inv = ROPE_THETA ** (-jnp.arange(n_freq, dtype=jnp.float32) / n_freq)
    rows = jnp.broadcast_to(jnp.arange(rows_n, dtype=jnp.float32)[:, None], (rows_n, GRID_W)).reshape(-1)
    cols = jnp.broadcast_to(jnp.arange(GRID_W, dtype=jnp.float32)[None, :], (rows_n, GRID_W)).reshape(-1)
    ang = jnp.concatenate([rows[:, None] * inv, cols[:, None] * inv], axis=-1)
    return jnp.cos(ang), jnp.sin(ang)


def _rope(x, cos, sin):
    half = x.shape[-1] // 2
    xf = x.astype(jnp.float32)
    x1, x2 = xf[..., :half], xf[..., half:]
    cs = cos[None, :, None, :]
    sn = sin[None, :, None, :]
    return jnp.concatenate([x1 * cs - x2 * sn, x2 * cs + x1 * sn], axis=-1).astype(x.dtype)


def _softmax_attend(q, ks, vs, masks, sink):
    scores = []
    for k, m in zip(ks, masks):
        s = jnp.einsum('bqhgd,bkhd->bhgqk', q, k).astype(jnp.float32)
        if m is not None:
            s = jnp.where(m, s, NEG_INF)
        scores.append(s)
    lens = [s.shape[-1] for s in scores]
    if sink is not None:
        sk = sink.astype(jnp.float32)[None, :, :, None, None]
        scores.append(jnp.broadcast_to(sk, scores[0].shape[:-1] + (1,)))
    p = jax.nn.softmax(jnp.concatenate(scores, axis=-1), axis=-1)
    out = 0
    off = 0
    for v, n in zip(vs, lens):
        out = out + jnp.einsum('bhgqk,bkhd->bqhgd', p[..., off:off + n].astype(v.dtype), v)
        off += n
    return out


def _dense_blocks(q, k_l, v_l, k_c, v_c):
    b, s, hkv, g, dk = q.shape
    nb = s // Q_BLOCK
    qb = jnp.moveaxis(q.reshape(b, nb, Q_BLOCK, hkv, g, dk), 1, 0)
    ob = lax.map(lambda qi: _softmax_attend(qi, (k_l, k_c), (v_l, v_c), (None, None), None), qb)
    return jnp.moveaxis(ob, 0, 1).reshape(b, s, hkv, g, -1)


def _window_blocks(q, k_l, v_l, k_c, v_c, sink):
    b, s, hkv, g, dk = q.shape
    nb = s // Q_BLOCK

    def bands(t):
        tp = jnp.pad(t, ((0, 0), (Q_BLOCK, Q_BLOCK), (0, 0), (0, 0))).reshape(b, nb + 2, Q_BLOCK, hkv, -1)
        band = jnp.concatenate([tp[:, :-2], tp[:, 1:-1], tp[:, 2:]], axis=2)
        return jnp.moveaxis(band, 1, 0)

    qpos = jnp.arange(s).reshape(nb, Q_BLOCK)
    kpos = jnp.arange(nb)[:, None] * Q_BLOCK - Q_BLOCK + jnp.arange(3 * Q_BLOCK)[None, :]
    inside = (kpos >= 0) & (kpos < s)
    valid = (jnp.abs(qpos[:, :, None] - kpos[:, None, :]) <= SWA_WINDOW) & inside[:, None, :]
    qb = jnp.moveaxis(q.reshape(b, nb, Q_BLOCK, hkv, g, dk), 1, 0)
    ob = lax.map(lambda a: _softmax_attend(a[0], (a[1], k_c), (a[2], v_c), (a[3], None), sink),
                 (qb, bands(k_l), bands(v_l), valid))
    return jnp.moveaxis(ob, 0, 1).reshape(b, s, hkv, g, -1)


def _mla(h_l, h_c, w_dq, q_norm_g, w_uq, w_dkv, kv_norm_g, w_ukv, w_o, cos, sin, need_ctx):
    def project(h, rotate):
        b, l, _ = h.shape
        q = (_rmsnorm(h @ w_dq, q_norm_g) @ w_uq).reshape(b, l, N_HEADS, MLA_QK_NOPE + MLA_QK_ROPE)
        q_nope, q_rope = q[..., :MLA_QK_NOPE], q[..., MLA_QK_NOPE:]
        kv_a = h @ w_dkv
        c_kv = _rmsnorm(kv_a[..., :MLA_KV_LORA], kv_norm_g)
        k_rope = kv_a[..., MLA_KV_LORA:][:, :, None, :]
        if rotate:
            q_rope = _rope(q_rope, cos, sin)
            k_rope = _rope(k_rope, cos, sin)
        kv = (c_kv @ w_ukv).reshape(b, l, N_HEADS, MLA_QK_NOPE + MLA_V_DIM)
        k = jnp.concatenate([kv[..., :MLA_QK_NOPE],
                             jnp.broadcast_to(k_rope, (b, l, N_HEADS, MLA_QK_ROPE))], axis=-1)
        v = kv[..., MLA_QK_NOPE:]
        q = jnp.concatenate([q_nope, q_rope], axis=-1) * (MLA_QK_NOPE + MLA_QK_ROPE) ** -0.5
        return q[:, :, :, None, :], k, v

    b, s, _ = h_l.shape
    q_l, k_l, v_l = project(h_l, True)
    q_c, k_c, v_c = project(h_c, False)
    y_l = _dense_blocks(q_l, k_l, v_l, k_c, v_c).reshape(b, s, -1) @ w_o
    y_c = None
    if need_ctx:
        o_c = _softmax_attend(q_c, (k_c,), (v_c,), (None,), None)
        y_c = o_c.reshape(b, h_c.shape[1], -1) @ w_o
    return y_l, y_c


def _gqa_project(h, w_qkv, n_kv, q_norm_g, k_norm_g, cos, sin, rotate):
    b, l, _ = h.shape
    nq = N_HEADS * HEAD_DIM
    nk = n_kv * HEAD_DIM
    qkv = h @ w_qkv
    q = qkv[..., :nq].reshape(b, l, N_HEADS, HEAD_DIM)
    k = qkv[..., nq:nq + nk].reshape(b, l, n_kv, HEAD_DIM)
    v = qkv[..., nq + nk:].reshape(b, l, n_kv, HEAD_DIM)
    if q_norm_g is not None:
        q = _rmsnorm(q, q_norm_g)
        k = _rmsnorm(k, k_norm_g)
    if rotate:
        q = _rope(q, cos, sin)
        k = _rope(k, cos, sin)
    q = (q * HEAD_DIM ** -0.5).reshape(b, l, n_kv, N_HEADS // n_kv, HEAD_DIM)
    return q, k, v


def _swa(h_l, h_c, w_qkv, sinks, w_o, cos, sin, need_ctx):
    b, s, _ = h_l.shape
    sink = sinks.reshape(SWA_KV_HEADS, N_HEADS // SWA_KV_HEADS)
    q_l, k_l, v_l = _gqa_project(h_l, w_qkv, SWA_KV_HEADS, None, None, cos, sin, True)
    q_c, k_c, v_c = _gqa_project(h_c, w_qkv, SWA_KV_HEADS, None, None, cos, sin, False)
    y_l = _window_blocks(q_l, k_l, v_l, k_c, v_c, sink).reshape(b, s, -1) @ w_o
    y_c = None
    if need_ctx:
        o_c = _softmax_attend(q_c, (k_c,), (v_c,), (None,), sink)
        y_c = o_c.reshape(b, h_c.shape[1], -1) @ w_o
    return y_l, y_c


def _axial_gqa(h_l, h_c, w_qkv, q_norm_g, k_norm_g, w_o, cos, sin, need_ctx):
    b, s, _ = h_l.shape
    q_l, k_l, v_l = _gqa_project(h_l, w_qkv, GA_KV_HEADS, q_norm_g, k_norm_g, cos, sin, True)
    q_c, k_c, v_c = _gqa_project(h_c, w_qkv, GA_KV_HEADS, q_norm_g, k_norm_g, cos, sin, False)
    y_l = _dense_blocks(q_l, k_l, v_l, k_c, v_c).reshape(b, s, -1) @ w_o
    y_c = None
    if need_ctx:
        o_c = _softmax_attend(q_c, (k_c,), (v_c,), (None,), None)
        y_c = o_c.reshape(b, h_c.shape[1], -1) @ w_o
    return y_l, y_c


def _swiglu(t, w_gate_up, w_down):
    gu = t @ w_gate_up
    f = w_down.shape[0]
    return (jax.nn.silu(gu[..., :f]) * gu[..., f:]) @ w_down


def _moe(t, router_w, router_b, w_gate_up, w_down):
    logits = (t @ router_w).astype(jnp.float32) + router_b.astype(jnp.float32)
    top_v, top_i = lax.top_k(logits, TOP_K)
    gates = jax.nn.softmax(top_v, axis=-1)
    combine = jnp.sum(jax.nn.one_hot(top_i, N_EXPERTS, dtype=jnp.float32) * gates[..., None], axis=-2)
    y = jnp.zeros_like(t)
    for e in range(N_EXPERTS):
        y = y + combine[:, e:e + 1].astype(t.dtype) * _swiglu(t, w_gate_up[e], w_down[e])
    return y


def setup_inputs(seed: int = 0) -> dict:
    key = jax.random.key(seed)
    ks = iter(jax.random.split(key, 40))
    n_a = len(range(0, DEPTH, N_MIXERS))
    n_b = len(range(1, DEPTH, N_MIXERS))
    n_c = len(range(2, DEPTH, N_MIXERS))
    n_dense = len(range(0, DEPTH, 2))
    n_moe = len(range(1, DEPTH, 2))
    d = D_MODEL
    attn = N_HEADS * HEAD_DIM

    def rn(shape, std=1.0):
        return jax.random.normal(next(ks), shape, jnp.float32) * std

    def w(shape, fan_in, gain=1.0):
        return rn(shape, gain * fan_in ** -0.5)

    def g(shape):
        return 1.0 + rn(shape, 0.05)

    return {
        "x": rn((BATCH, SEQ, d)),
        "c": rn((BATCH, d)),
        "ctx": rn((BATCH, CTX_LEN, d)),
        "c_ctx": rn((d,)),
        "ada_w": w((DEPTH, d, 6 * d), d, 0.5),
        "ada_b": rn((DEPTH, 6 * d), 0.02),
        "norm_g": g((DEPTH, 2, d)),
        "final_norm_g": g((d,)),
        "mla_w_dq": w((n_a, d, MLA_Q_LORA), d),
        "mla_q_norm_g": g((n_a, MLA_Q_LORA)),
        "mla_w_uq": w((n_a, MLA_Q_LORA, N_HEADS * (MLA_QK_NOPE + MLA_QK_ROPE)), MLA_Q_LORA),
        "mla_w_dkv": w((n_a, d, MLA_KV_LORA + MLA_QK_ROPE), d),
        "mla_kv_norm_g": g((n_a, MLA_KV_LORA)),
        "mla_w_ukv": w((n_a, MLA_KV_LORA, N_HEADS * (MLA_QK_NOPE + MLA_V_DIM)), MLA_KV_LORA),
        "mla_w_o": w((n_a, N_HEADS * MLA_V_DIM, d), N_HEADS * MLA_V_DIM),
        "swa_w_qkv": w((n_b, d, (N_HEADS + 2 * SWA_KV_HEADS) * HEAD_DIM), d),
        "swa_sinks": rn((n_b, N_HEADS), 0.5),
        "swa_w_o": w((n_b, attn, d), attn),
        "ga_w_qkv": w((n_c, d, (N_HEADS + 2 * GA_KV_HEADS) * HEAD_DIM), d),
        "ga_q_norm_g": g((n_c, HEAD_DIM)),
        "ga_k_norm_g": g((n_c, HEAD_DIM)),
        "ga_w_o": w((n_c, attn, d), attn),
        "ffn_w_gate_up": w((n_dense, d, 2 * D_FF), d),
        "ffn_w_down": w((n_dense, D_FF, d), D_FF),
        "moe_router_w": w((n_moe, d, N_EXPERTS), d),
        "moe_router_b": rn((n_moe, N_EXPERTS), 0.01),
        "moe_w_gate_up": w((n_moe, N_EXPERTS, d, 2 * EXPERT_FF), d),
        "moe_w_down": w((n_moe, N_EXPERTS, EXPERT_FF, d), EXPERT_FF),
    }


def reference(x, c, ctx, c_ctx, ada_w, ada_b, norm_g, final_norm_g,
              mla_w_dq, mla_q_norm_g, mla_w_uq, mla_w_dkv, mla_kv_norm_g, mla_w_ukv, mla_w_o,
              swa_w_qkv, swa_sinks, swa_w_o,
              ga_w_qkv, ga_q_norm_g, ga_k_norm_g, ga_w_o,
              ffn_w_gate_up, ffn_w_down,
              moe_router_w, moe_router_b, moe_w_gate_up, moe_w_down):
    b, s, d = x.shape
    n_lat = b * s
    rows_n = s // GRID_W
    cos_mla, sin_mla = _axial_rope_tables(rows_n, MLA_QK_ROPE)
    cos_h, sin_h = _axial_rope_tables(rows_n, HEAD_DIM)
    sc = jax.nn.silu(c)
    scc = jax.nn.silu(c_ctx)

    for i in range(DEPTH):
        need_ctx = i < DEPTH - 1
        mod_l = (sc @ ada_w[i] + ada_b[i]).reshape(b, 6, 1, d)
        mod_c = (scc @ ada_w[i] + ada_b[i]).reshape(6, d)

        h_l = _modulate(_rmsnorm(x, norm_g[i, 0]), mod_l[:, 0], mod_l[:, 1])
        h_c = _modulate(_rmsnorm(ctx, norm_g[i, 0]), mod_c[0], mod_c[1])
        kind, j = i % N_MIXERS, i // N_MIXERS
        if kind == 0:
            y_l, y_c = _mla(h_l, h_c, mla_w_dq[j], mla_q_norm_g[j], mla_w_uq[j], mla_w_dkv[j],
                            mla_kv_norm_g[j], mla_w_ukv[j], mla_w_o[j], cos_mla, sin_mla, need_ctx)
        elif kind == 1:
            y_l, y_c = _swa(h_l, h_c, swa_w_qkv[j], swa_sinks[j], swa_w_o[j], cos_h, sin_h, need_ctx)
        else:
            y_l, y_c = _axial_gqa(h_l, h_c, ga_w_qkv[j], ga_q_norm_g[j], ga_k_norm_g[j], ga_w_o[j],
                                  cos_h, sin_h, need_ctx)
        x = x + mod_l[:, 2] * y_l
        if need_ctx:
            ctx = ctx + mod_c[2] * y_c

        h_l = _modulate(_rmsnorm(x, norm_g[i, 1]), mod_l[:, 3], mod_l[:, 4])
        tokens = h_l.reshape(n_lat, d)
        if need_ctx:
            h_c = _modulate(_rmsnorm(ctx, norm_g[i, 1]), mod_c[3], mod_c[4])
            tokens = jnp.concatenate([tokens, h_c.reshape(-1, d)], axis=0)
        f = i // 2
        if i % 2 == 0:
            out = _swiglu(tokens, ffn_w_gate_up[f], ffn_w_down[f])
        else:
            out = _moe(tokens, moe_router_w[f], moe_router_b[f], moe_w_gate_up[f], moe_w_down[f])
        x = x + mod_l[:, 5] * out[:n_lat].reshape(b, s, d)
        if need_ctx:
            ctx = ctx + mod_c[5] * out[n_lat:].reshape(ctx.shape)

    return _rmsnorm(x, final_norm_g)
```

```python
import functools

import numpy as np
import jax
import jax.numpy as jnp
from jax import lax
from jax.experimental import pallas as pl
from jax.experimental.pallas import tpu as pltpu

F32 = jnp.float32
BF16 = jnp.bfloat16

D_MODEL = 1024
GRID_W = 64
HEAD_DIM = 64
N_HEADS = 16
MLA_Q_LORA = 384
MLA_KV_LORA = 256
MLA_QK_NOPE = 64
MLA_QK_ROPE = 32
MLA_V_DIM = 64
SWA_KV_HEADS = 4
SWA_WINDOW = 128
GA_KV_HEADS = 8
D_FF = 2816
N_EXPERTS = 8
EXPERT_FF = 1408
ROPE_THETA = 10000.0
NORM_EPS = 1e-6
NEG_INF = -1e30
DEPTH = 4

LANES = 128
VMEM_LIMIT = 56 * 2**20
ROW_TILE = 512
Q_TILE = 256
MOE_TILE = 512
FF_CHUNK = 1408


def _cparams(*sem):
    return pltpu.CompilerParams(dimension_semantics=sem, vmem_limit_bytes=VMEM_LIMIT)


def _silu(x):
    return x * (1.0 / (1.0 + jnp.exp(-x)))


def _rms(x, g):
    ms = jnp.mean(x * x, axis=-1, keepdims=True)
    return x * lax.rsqrt(ms + NORM_EPS) * g


def _norm_mod(x, g, shift, scale):
    return _rms(x, g) * (1.0 + scale) + shift


def _dot(a, b):
    return jnp.dot(a, b, preferred_element_type=F32)


def _dot_t(a, b):
    return lax.dot_general(a, b, (((1,), (1,)), ((), ())), preferred_element_type=F32)


def _mods_body(c_ref, w_ref, b_ref, o_ref):
    sc = _silu(c_ref[...]).astype(BF16)
    o_ref[0] = _dot(sc, w_ref[0].astype(BF16)) + b_ref[0]


def _mods(c_all, ada_w, ada_b):
    depth, d, n = ada_w.shape
    rows = c_all.shape[0]
    tn = 1536
    return pl.pallas_call(
        _mods_body,
        grid=(depth, n // tn),
        in_specs=[
            pl.BlockSpec((rows, d), lambda i, j: (0, 0)),
            pl.BlockSpec((1, d, tn), lambda i, j: (i, 0, j)),
            pl.BlockSpec((1, 1, tn), lambda i, j: (i, 0, j)),
        ],
        out_specs=pl.BlockSpec((1, rows, tn), lambda i, j: (i, 0, j)),
        out_shape=jax.ShapeDtypeStruct((depth, rows, n), F32),
        compiler_params=_cparams("arbitrary", "arbitrary"),
        name="adaln_mods",
    )(c_all, ada_w, ada_b.reshape(depth, 1, n))


def _pair_lanes():
    lane = np.arange(LANES)
    is_b = (lane % 64) >= 32
    dim = (lane % 32) + 32 * (lane // 64)
    return is_b, dim


def _gqa_perm(n_kv, group):
    is_b, dim = _pair_lanes()
    nq = N_HEADS * HEAD_DIM
    nk = n_kv * HEAD_DIM
    q_cols, k_cols, o_rows = [], [], []
    nat = np.arange(HEAD_DIM)
    for j in range(n_kv // 2):
        for g in range(group):
            head_a, head_b = (2 * j) * group + g, (2 * j + 1) * group + g
            q_cols.append(np.where(is_b, head_b, head_a) * HEAD_DIM + dim)
            o_rows.append(np.concatenate([head_a * HEAD_DIM + nat, head_b * HEAD_DIM + nat]))
        k_cols.append(nq + np.where(is_b, 2 * j + 1, 2 * j) * HEAD_DIM + dim)
    cols = np.concatenate(q_cols + k_cols + [nq + nk + np.arange(nk)])
    return cols.astype(np.int32), np.concatenate(o_rows).astype(np.int32)


def _angles(s, rot_dim):
    n_freq = rot_dim // 4
    inv = ROPE_THETA ** (-jnp.arange(n_freq, dtype=F32) / n_freq)
    pos = jnp.arange(s)
    rows = (pos // GRID_W).astype(F32)
    cols = (pos % GRID_W).astype(F32)
    return jnp.concatenate([rows[:, None] * inv, cols[:, None] * inv], axis=-1)


def _rope_tables(cos_l, sin_l, q_scale):
    lat = jnp.stack([cos_l * q_scale, sin_l * q_scale, cos_l, sin_l])
    one = jnp.ones_like(cos_l)
    zero = jnp.zeros_like(cos_l)
    ctx = jnp.stack([one * q_scale, zero, one, zero])
    return jnp.stack([lat, ctx])


def _gqa_tables(s):
    ang = _angles(s, HEAD_DIM)
    lane = np.arange(LANES)
    idx = lane % 32
    sign = jnp.asarray(np.where(lane < 64, -1.0, 1.0), F32)
    return _rope_tables(jnp.cos(ang)[:, idx], jnp.sin(ang)[:, idx] * sign, HEAD_DIM ** -0.5)


def _mla_lane_src():
    lane = np.arange(LANES)
    nope = np.where((lane >= 16) & (lane < 64), lane - 16,
                    np.where((lane >= 80) & (lane < 96), 48 + lane - 80, -1))
    rope = np.where(lane < 16, lane, np.where((lane >= 64) & (lane < 80), 16 + lane - 64, -1))
    return nope, rope


def _mla_tables(s):
    ang = _angles(s, MLA_QK_ROPE)
    lane = np.arange(LANES)
    is_x1 = lane < 16
    is_x2 = (lane >= 64) & (lane < 80)
    idx = np.where(is_x1, lane, np.where(is_x2, lane - 64, 0))
    rot = jnp.asarray(is_x1 | is_x2)
    sign = jnp.asarray(np.where(is_x1, -1.0, np.where(is_x2, 1.0, 0.0)), F32)
    cos_l = jnp.where(rot, jnp.cos(ang)[:, idx], 1.0)
    sin_l = jnp.sin(ang)[:, idx] * sign
    return _rope_tables(cos_l, sin_l, (MLA_QK_NOPE + MLA_QK_ROPE) ** -0.5)


def _rope(blk, cos, sin):
    return blk * cos + pltpu.roll(blk, 64, 1) * sin


def _gqa_proj_body(*refs, nqb, nkb, qk_norm):
    if qk_norm:
        x_ref, mod_ref, g_ref, w_ref, tab_ref, gq_ref, gk_ref, ind_ref, q_ref, k_ref, v_ref = refs
    else:
        x_ref, mod_ref, g_ref, w_ref, tab_ref, q_ref, k_ref, v_ref = refs
    h = _norm_mod(x_ref[0], g_ref[...], mod_ref[0, 0:1, :], mod_ref[0, 1:2, :]).astype(BF16)
    qkv = _dot(h, w_ref[...])
    cq, sq, ck, sk = tab_ref[0, 0], tab_ref[0, 1], tab_ref[0, 2], tab_ref[0, 3]

    def head_norm(blk, gain):
        ssq = _dot((blk * blk).astype(BF16), ind_ref[...])
        return blk * lax.rsqrt(ssq * (1.0 / HEAD_DIM) + NORM_EPS) * gain

    for c in range(nqb):
        blk = qkv[:, c * LANES:(c + 1) * LANES]
        if qk_norm:
            blk = head_norm(blk, gq_ref[...])
        q_ref[0, :, c * LANES:(c + 1) * LANES] = _rope(blk, cq, sq).astype(BF16)
    for c in range(nkb):
        blk = qkv[:, (nqb + c) * LANES:(nqb + c + 1) * LANES]
        if qk_norm:
            blk = head_norm(blk, gk_ref[...])
        k_ref[0, :, c * LANES:(c + 1) * LANES] = _rope(blk, ck, sk).astype(BF16)
    v_ref[0] = qkv[:, (nqb + nkb) * LANES:].astype(BF16)


def _gqa_proj(xs, mods_i, norm_g, w_perm, tables, n_kv, qk_gains=None):
    nb, s, d = xs.shape
    tm = min(ROW_TILE, s)
    nqb = N_HEADS * HEAD_DIM // LANES
    nkb = n_kv * HEAD_DIM // LANES
    wn = w_perm.shape[1]
    in_specs = [
        pl.BlockSpec((1, tm, d), lambda b, i: (b, i, 0)),
        pl.BlockSpec((1, 6, d), lambda b, i: (b, 0, 0)),
        pl.BlockSpec((1, d), lambda b, i: (0, 0)),
        pl.BlockSpec((d, wn), lambda b, i: (0, 0)),
        pl.BlockSpec((1, 4, tm, LANES), lambda b, i: (b // (nb - 1), 0, i, 0)),
    ]
    args = [xs, mods_i, norm_g.reshape(1, d), w_perm, tables]
    if qk_gains is not None:
        is_b, dim = _pair_lanes()
        ind = jnp.asarray(is_b[:, None] == is_b[None, :], BF16)
        in_specs += [pl.BlockSpec((1, LANES), lambda b, i: (0, 0)),
                     pl.BlockSpec((1, LANES), lambda b, i: (0, 0)),
                     pl.BlockSpec((LANES, LANES), lambda b, i: (0, 0))]
        args += [qk_gains[0][dim].reshape(1, LANES), qk_gains[1][dim].reshape(1, LANES), ind]
    out_w = (nqb * LANES, nkb * LANES, nkb * LANES)
    return pl.pallas_call(
        functools.partial(_gqa_proj_body, nqb=nqb, nkb=nkb, qk_norm=qk_gains is not None),
        grid=(nb, s // tm),
        in_specs=in_specs,
        out_specs=[pl.BlockSpec((1, tm, w), lambda b, i: (b, i, 0)) for w in out_w],
        out_shape=[jax.ShapeDtypeStruct((nb, s, w), BF16) for w in out_w],
        compiler_params=_cparams("arbitrary", "arbitrary"),
        name="gqa_proj",
    )(*args)


def _mla_proj_body(x_ref, mod_ref, g_ref, wa_ref, gq_ref, gkv_ref, wuq_ref, wukv_ref, tab_ref,
                   q_ref, k_ref, v_ref):
    h = _norm_mod(x_ref[0], g_ref[...], mod_ref[0, 0:1, :], mod_ref[0, 1:2, :]).astype(BF16)
    a = _dot(h, wa_ref[...])
    qn = _rms(a[:, :MLA_Q_LORA], gq_ref[...]).astype(BF16)
    cn = _rms(a[:, MLA_Q_LORA:MLA_Q_LORA + MLA_KV_LORA], gkv_ref[...]).astype(BF16)
    kr = a[:, MLA_Q_LORA + MLA_KV_LORA:]
    cq, sq, ck, sk = tab_ref[0, 0], tab_ref[0, 1], tab_ref[0, 2], tab_ref[0, 3]
    q = _dot(qn, wuq_ref[...])
    kv = _dot(cn, wukv_ref[...])
    kr = _rope(kr, ck, sk)
    for hh in range(N_HEADS):
        sl = slice(hh * LANES, (hh + 1) * LANES)
        q_ref[0, :, sl] = _rope(q[:, sl], cq, sq).astype(BF16)
        k_ref[0, :, sl] = (kv[:, sl] + kr).astype(BF16)
    v_ref[0] = kv[:, N_HEADS * LANES:].astype(BF16)


def _mla_proj(xs, mods_i, norm_g, w_a, gq, gkv, w_uq, w_ukv, tables):
    nb, s, d = xs.shape
    tm = min(ROW_TILE, s)
    full = lambda arr: pl.BlockSpec(arr.shape, lambda b, i: (0,) * arr.ndim)
    gq = gq.reshape(1, -1)
    gkv = gkv.reshape(1, -1)
    g = norm_g.reshape(1, d)
    out_w = (N_HEADS * LANES, N_HEADS * LANES, N_HEADS * MLA_V_DIM)
    return pl.pallas_call(
        _mla_proj_body,
        grid=(nb, s // tm),
        in_specs=[
            pl.BlockSpec((1, tm, d), lambda b, i: (b, i, 0)),
            pl.BlockSpec((1, 6, d), lambda b, i: (b, 0, 0)),
            full(g), full(w_a), full(gq), full(gkv), full(w_uq), full(w_ukv),
            pl.BlockSpec((1, 4, tm, LANES), lambda b, i: (b // (nb - 1), 0, i, 0)),
        ],
        out_specs=[pl.BlockSpec((1, tm, w), lambda b, i: (b, i, 0)) for w in out_w],
        out_shape=[jax.ShapeDtypeStruct((nb, s, w), BF16) for w in out_w],
        compiler_params=_cparams("arbitrary", "arbitrary"),
        name="mla_proj",
    )(xs, mods_i, g, w_a, gq, gkv, w_uq, w_ukv, tables)


def _attn_body(*refs, group, ql, split, mode, tq, band, seq, use_sink):
    refs = list(refs)
    sink_ref = refs.pop(0) if use_sink else None
    q_ref = refs.pop(0)
    if mode != "ctx":
        kl_ref, vl_ref = refs.pop(0), refs.pop(0)
    kc_ref, vc_ref, o_ref = refs
    j = pl.program_id(1)
    qi = pl.program_id(2)
    lane = lax.broadcasted_iota(jnp.int32, (1, LANES), 1)
    in_a = (lane & 63) < 32
    left = lane < 64
    kc = kc_ref[0]
    vc = vc_ref[0]
    valid = None
    if mode == "dense":
        kl = kl_ref[0]
        vl = vl_ref[0]
    elif mode == "window":
        start = jnp.clip(qi * tq - SWA_WINDOW, 0, seq - band)
        start = pl.multiple_of(start, LANES)
        kl = kl_ref[0, pl.ds(start, band), :]
        vl = vl_ref[0, pl.ds(start, band), :]
        qpos = qi * tq + lax.broadcasted_iota(jnp.int32, (tq, 1), 0)
        kpos = start + lax.broadcasted_iota(jnp.int32, (1, band), 1)
        valid = jnp.abs(qpos - kpos) <= SWA_WINDOW

    for g in range(group):
        q = q_ref[0, :, g * ql:(g + 1) * ql]
        outs = []
        for half in range(2):
            if split:
                hs = slice(half * LANES, (half + 1) * LANES)
                qh, kch = q[:, hs], kc[:, hs]
            else:
                qh = jnp.where(in_a if half == 0 else jnp.logical_not(in_a), q, jnp.zeros_like(q))
                kch = kc
            s_c = _dot_t(qh, kch)
            m = jnp.max(s_c, axis=-1, keepdims=True)
            if mode != "ctx":
                klh = kl[:, hs] if split else kl
                s_l = _dot_t(qh, klh)
                if valid is not None:
                    s_l = jnp.where(valid, s_l, NEG_INF)
                m = jnp.maximum(m, jnp.max(s_l, axis=-1, keepdims=True))
            if use_sink:
                sink = sink_ref[(2 * j + half) * group + g]
                m = jnp.maximum(m, sink)
            p_c = jnp.exp(s_c - m)
            den = jnp.sum(p_c, axis=-1, keepdims=True)
            o = _dot(p_c.astype(BF16), vc)
            if mode != "ctx":
                p_l = jnp.exp(s_l - m)
                den = den + jnp.sum(p_l, axis=-1, keepdims=True)
                o = o + _dot(p_l.astype(BF16), vl)
            if use_sink:
                den = den + jnp.exp(sink - m)
            outs.append(o / den)
        o_ref[0, :, g * LANES:(g + 1) * LANES] = jnp.where(left, outs[0], outs[1]).astype(BF16)


def _attention(q, k, v, n_pairs, group, split, window, sinks, need_ctx):
    nb, s, _ = q.shape
    b_lat = nb - 1
    ctx_len = s // b_lat
    ql = q.shape[2] // (n_pairs * group)
    klw = k.shape[2] // n_pairs
    ow = n_pairs * group * LANES
    tq = min(Q_TILE, s)
    band = min(tq + 2 * SWA_WINDOW, s)
    use_sink = sinks is not None
    common = dict(group=group, ql=ql, split=split, tq=tq, band=band, seq=s, use_sink=use_sink)
    smem = [pl.BlockSpec(memory_space=pltpu.SMEM)] if use_sink else []
    sink_args = [sinks] if use_sink else []

    lat_specs = smem + [
        pl.BlockSpec((1, tq, group * ql), lambda b, j, i: (b, i, j)),
        pl.BlockSpec((1, s, klw), lambda b, j, i: (b, 0, j)),
        pl.BlockSpec((1, s, LANES), lambda b, j, i: (b, 0, j)),
        pl.BlockSpec((1, ctx_len, klw), lambda b, j, i: (b_lat, b, j)),
        pl.BlockSpec((1, ctx_len, LANES), lambda b, j, i: (b_lat, b, j)),
    ]
    n_out = nb if need_ctx else b_lat
    o = pl.pallas_call(
        functools.partial(_attn_body, mode="window" if window else "dense", **common),
        grid=(b_lat, n_pairs, s // tq),
        in_specs=lat_specs,
        out_specs=pl.BlockSpec((1, tq, group * LANES), lambda b, j, i: (b, i, j)),
        out_shape=jax.ShapeDtypeStruct((n_out, s, ow), BF16),
        compiler_params=_cparams("arbitrary", "arbitrary", "arbitrary"),
        name="attn_latent",
    )(*sink_args, q, k, v, k, v)
    if not need_ctx:
        return o

    common["tq"] = ctx_len
    ctx_specs = smem + [
        pl.BlockSpec((1, ctx_len, group * ql), lambda b, j, i: (b_lat, b, j)),
        pl.BlockSpec((1, ctx_len, klw), lambda b, j, i: (b_lat, b, j)),
        pl.BlockSpec((1, ctx_len, LANES), lambda b, j, i: (b_lat, b, j)),
        pl.BlockSpec(memory_space=pl.ANY),
    ]
    n_in = len(ctx_specs)

    def ctx_body(*refs):
        refs = list(refs)
        del refs[n_in - 1]
        _attn_body(*refs, mode="ctx", **common)

    return pl.pallas_call(
        ctx_body,
        grid=(b_lat, n_pairs, 1),
        in_specs=ctx_specs,
        out_specs=pl.BlockSpec((1, ctx_len, group * LANES), lambda b, j, i: (b_lat, b, j)),
        out_shape=jax.ShapeDtypeStruct((nb, s, ow), BF16),
        input_output_aliases={n_in - 1: 0},
        compiler_params=_cparams("arbitrary", "arbitrary", "arbitrary"),
        name="attn_context",
    )(*sink_args, q, k, v, o)


def _post_body(*refs, moe):
    if moe:
        o_ref, x_ref, mod_ref, g_ref, wo_ref, rw_ref, rb_ref, xo_ref, h_ref, comb_ref = refs
    else:
        o_ref, x_ref, mod_ref, g_ref, wo_ref, xo_ref, h_ref = refs
    y = _dot(o_ref[0], wo_ref[...])
    x = x_ref[0] + mod_ref[0, 2:3, :] * y
    xo_ref[0] = x
    h = _norm_mod(x, g_ref[...], mod_ref[0, 3:4, :], mod_ref[0, 4:5, :])
    h_hi = h.astype(BF16)
    h_ref[0] = h_hi
    if not moe:
        return
    h_lo = (h - h_hi.astype(F32)).astype(BF16)
    both = _dot(h_hi, rw_ref[...])
    logits = both[:, :LANES] + both[:, LANES:] + _dot(h_lo, rw_ref[:, :LANES]) + rb_ref[...]
    lane = lax.broadcasted_iota(jnp.int32, logits.shape, 1).astype(F32)
    lg = jnp.where(lane < N_EXPERTS, logits, -jnp.inf)
    v1 = jnp.max(lg, axis=-1, keepdims=True)
    i1 = jnp.min(jnp.where(lg == v1, lane, float(LANES)), axis=-1, keepdims=True)
    lg2 = jnp.where(lane == i1, -jnp.inf, lg)
    v2 = jnp.max(lg2, axis=-1, keepdims=True)
    i2 = jnp.min(jnp.where(lg2 == v2, lane, float(LANES)), axis=-1, keepdims=True)
    e = jnp.exp(v2 - v1)
    g1 = 1.0 / (1.0 + e)
    g2 = e / (1.0 + e)
    comb_ref[0] = jnp.where(lane == i1, g1, 0.0) + jnp.where(lane == i2, g2, 0.0)


def _post(o, xs, mods_i, norm_g, w_o, nb, router=None):
    _, s, d = xs.shape
    tm = min(ROW_TILE, s)
    moe = router is not None
    in_specs = [
        pl.BlockSpec((1, tm, d), lambda b, i: (b, i, 0)),
        pl.BlockSpec((1, tm, d), lambda b, i: (b, i, 0)),
        pl.BlockSpec((1, 6, d), lambda b, i: (b, 0, 0)),
        pl.BlockSpec((1, d), lambda b, i: (0, 0)),
        pl.BlockSpec((d, d), lambda b, i: (0, 0)),
    ]
    args = [o, xs, mods_i, norm_g.reshape(1, d), w_o]
    out_specs = [pl.BlockSpec((1, tm, d), lambda b, i: (b, i, 0)),
                 pl.BlockSpec((1, tm, d), lambda b, i: (b, i, 0))]
    out_shape = [jax.ShapeDtypeStruct(xs.shape, F32), jax.ShapeDtypeStruct((nb, s, d), BF16)]
    if moe:
        in_specs += [pl.BlockSpec((d, 2 * LANES), lambda b, i: (0, 0)),
                     pl.BlockSpec((1, LANES), lambda b, i: (0, 0))]
        args += list(router)
        out_specs.append(pl.BlockSpec((1, tm, LANES), lambda b, i: (b, i, 0)))
        out_shape.append(jax.ShapeDtypeStruct((nb, s, LANES), F32))
    return pl.pallas_call(
        functools.partial(_post_body, moe=moe),
        grid=(nb, s // tm),
        in_specs=in_specs,
        out_specs=out_specs,
        out_shape=out_shape,
        input_output_aliases={1: 0},
        compiler_params=_cparams("arbitrary", "arbitrary"),
        name="attn_out_norm",
    )(*args)


def _ffn_body(h_ref, x_ref, mod_ref, wgu_ref, wd_ref, xo_ref):
    h = h_ref[0]
    acc = None
    for c in range(D_FF // FF_CHUNK):
        gate = _dot(h, wgu_ref[:, c * FF_CHUNK:(c + 1) * FF_CHUNK])
        up = _dot(h, wgu_ref[:, D_FF + c * FF_CHUNK:D_FF + (c + 1) * FF_CHUNK])
        act = (_silu(gate) * up).astype(BF16)
        part = _dot(act, wd_ref[c * FF_CHUNK:(c + 1) * FF_CHUNK, :])
        acc = part if acc is None else acc + part
    xo_ref[0] = x_ref[0] + mod_ref[0, 5:6, :] * acc


def _ffn(h2, xs, mods_i, w_gu, w_d, nb):
    _, s, d = xs.shape
    tm = min(ROW_TILE, s)
    resident = lambda arr: pl.BlockSpec(arr.shape, lambda b, i: (0, 0), pipeline_mode=pl.Buffered(1))
    return pl.pallas_call(
        _ffn_body,
        grid=(nb, s // tm),
        in_specs=[
            pl.BlockSpec((1, tm, d), lambda b, i: (b, i, 0)),
            pl.BlockSpec((1, tm, d), lambda b, i: (b, i, 0)),
            pl.BlockSpec((1, 6, d), lambda b, i: (b, 0, 0)),
            resident(w_gu), resident(w_d),
        ],
        out_specs=pl.BlockSpec((1, tm, d), lambda b, i: (b, i, 0)),
        out_shape=jax.ShapeDtypeStruct(xs.shape, F32),
        input_output_aliases={1: 0},
        compiler_params=_cparams("arbitrary", "arbitrary"),
        name="ffn_dense",
    )(h2, xs, mods_i, w_gu, w_d)


def _moe_body(te_ref, nu_ref, xs_ref, wg_ref, wu_ref, wd_ref, ys_ref):
    i = pl.program_id(0)

    @pl.when(i < nu_ref[0])
    def _():
        x = xs_ref[...]
        act = (_silu(_dot(x, wg_ref[0])) * _dot(x, wu_ref[0])).astype(BF16)
        ys_ref[...] = _dot(act, wd_ref[0])

    @pl.when(i >= nu_ref[0])
    def _():
        ys_ref[...] = jnp.zeros_like(ys_ref)


def _moe_experts(xs_sorted, tile_expert, n_used, w_gu, w_d, tm):
    p, d = xs_sorted.shape
    f = w_d.shape[1]
    grid_spec = pltpu.PrefetchScalarGridSpec(
        num_scalar_prefetch=2,
        grid=(p // tm,),
        in_specs=[
            pl.BlockSpec((tm, d), lambda i, te, nu: (i, 0)),
            pl.BlockSpec((1, d, f), lambda i, te, nu: (te[i], 0, 0)),
            pl.BlockSpec((1, d, f), lambda i, te, nu: (te[i], 0, 1)),
            pl.BlockSpec((1, f, d), lambda i, te, nu: (te[i], 0, 0)),
        ],
        out_specs=pl.BlockSpec((tm, d), lambda i, te, nu: (i, 0)),
    )
    return pl.pallas_call(
        _moe_body,
        grid_spec=grid_spec,
        out_shape=jax.ShapeDtypeStruct((p, d), F32),
        compiler_params=_cparams("arbitrary"),
        name="moe_experts",
    )(tile_expert, n_used, xs_sorted, w_gu, w_gu, w_d)


def _combine_body(*refs, final):
    if final:
        x_ref, y1_ref, y2_ref, gt_ref, mod_ref, g_ref, o_ref = refs
    else:
        x_ref, y1_ref, y2_ref, gt_ref, mod_ref, o_ref = refs
    gt = gt_ref[0]
    y = gt[:, 0:1] * y1_ref[0] + gt[:, 1:2] * y2_ref[0]
    x = x_ref[0] + mod_ref[0, 5:6, :] * y
    o_ref[0] = _rms(x, g_ref[...]) if final else x


def _moe_combine(xs, y1, y2, gts, mods_i, nb, final_g=None):
    _, s, d = xs.shape
    tm = min(ROW_TILE, s)
    final = final_g is not None
    tok = pl.BlockSpec((1, tm, d), lambda b, i: (b, i, 0))
    in_specs = [tok, tok, tok,
                pl.BlockSpec((1, tm, LANES), lambda b, i: (b, i, 0)),
                pl.BlockSpec((1, 6, d), lambda b, i: (b, 0, 0))]
    args = [xs, y1, y2, gts, mods_i]
    if final:
        in_specs.append(pl.BlockSpec((1, d), lambda b, i: (0, 0)))
        args.append(final_g.reshape(1, d))
    return pl.pallas_call(
        functools.partial(_combine_body, final=final),
        grid=(nb, s // tm),
        in_specs=in_specs,
        out_specs=tok,
        out_shape=jax.ShapeDtypeStruct((nb, s, d) if final else xs.shape, F32),
        input_output_aliases={} if final else {0: 0},
        compiler_params=_cparams("arbitrary", "arbitrary"),
        name="moe_combine",
    )(*args)


def _moe_layer(h2, comb, xs, mods_i, w_gu, w_d, nb, final_g):
    _, s, d = xs.shape
    t = nb * s
    tm = min(MOE_TILE, s)
    p = 2 * t + N_EXPERTS * tm
    cmb = comb.reshape(t, LANES)[:, :N_EXPERTS]
    sel = cmb > 0
    csum = jnp.cumsum(sel.astype(jnp.int32), axis=0)
    cnt = csum[-1]
    padded = ((cnt + tm - 1) // tm) * tm
    off_end = jnp.cumsum(padded)
    pos = (off_end - padded)[None, :] + csum - 1
    tok = jnp.broadcast_to(jnp.arange(t, dtype=jnp.int32)[:, None], (t, N_EXPERTS))
    src = jnp.zeros((p,), jnp.int32).at[jnp.where(sel, pos, p).reshape(-1)].set(tok.reshape(-1), mode="drop")
    tile_start = jnp.arange(p // tm, dtype=jnp.int32) * tm
    tile_expert = jnp.minimum(jnp.sum(tile_start[:, None] >= off_end[None, :], axis=1), N_EXPERTS - 1)
    n_used = (off_end[-1] // tm).reshape(1).astype(jnp.int32)

    eidx = jnp.arange(N_EXPERTS)[None, :]
    first = jnp.argmax(sel, axis=1)[:, None]
    sel2 = sel & (eidx != first)
    second = jnp.argmax(sel2, axis=1)[:, None]
    has2 = jnp.any(sel2, axis=1)
    pos1 = jnp.take_along_axis(pos, first, axis=1)[:, 0]
    pos2 = jnp.where(has2, jnp.take_along_axis(pos, second, axis=1)[:, 0], pos1)
    g1 = jnp.take_along_axis(cmb, first, axis=1)
    g2 = jnp.where(has2[:, None], jnp.take_along_axis(cmb, second, axis=1), 0.0)
    gts = jnp.concatenate([g1, g2, jnp.zeros((t, LANES - 2), F32)], axis=1).reshape(nb, s, LANES)

    xs_sorted = jnp.take(h2.reshape(t, d), src, axis=0)
    ys = _moe_experts(xs_sorted, tile_expert.astype(jnp.int32), n_used, w_gu, w_d, tm)
    y1 = jnp.take(ys, pos1, axis=0).reshape(nb, s, d)
    y2 = jnp.take(ys, pos2, axis=0).reshape(nb, s, d)
    return _moe_combine(xs, y1, y2, gts, mods_i, nb, final_g)


def kernel(x, c, ctx, c_ctx, ada_w, ada_b, norm_g, final_norm_g, mla_w_dq, mla_q_norm_g, mla_w_uq, mla_w_dkv, mla_kv_norm_g, mla_w_ukv, mla_w_o, swa_w_qkv, swa_sinks, swa_w_o, ga_w_qkv, ga_q_norm_g, ga_k_norm_g, ga_w_o, ffn_w_gate_up, ffn_w_down, moe_router_w, moe_router_b, moe_w_gate_up, moe_w_down):
    b, s, d = x.shape
    assert d == D_MODEL and b * ctx.shape[1] == s and s % GRID_W == 0
    nb = b + 1
    xs = jnp.concatenate([x, ctx.reshape(1, s, d)], axis=0)
    c_all = jnp.concatenate([c, c_ctx[None], jnp.zeros((16 - nb, d), F32)], axis=0)
    mods = _mods(c_all, ada_w, ada_b).reshape(DEPTH, 16, 6, d)

    gqa_tab = _gqa_tables(s)
    mla_tab = _mla_tables(s)
    nope_src, rope_src = _mla_lane_src()

    for i in range(DEPTH):
        need_ctx = i < DEPTH - 1
        mods_i = mods[i]
        kind, j = i % 3, i // 3
        if kind == 0:
            dkv = jnp.concatenate([mla_w_dkv[j], jnp.zeros((d, 1), F32)], axis=1)
            kr_cols = np.where(rope_src >= 0, MLA_KV_LORA + rope_src, MLA_KV_LORA + MLA_QK_ROPE)
            w_a = jnp.concatenate([mla_w_dq[j], dkv[:, :MLA_KV_LORA], dkv[:, kr_cols]], axis=1).astype(BF16)
            qd = MLA_QK_NOPE + MLA_QK_ROPE
            q_src = np.where(nope_src >= 0, nope_src, np.where(rope_src >= 0, MLA_QK_NOPE + rope_src, qd))
            uq = jnp.concatenate([mla_w_uq[j].reshape(MLA_Q_LORA, N_HEADS, qd),
                                  jnp.zeros((MLA_Q_LORA, N_HEADS, 1), F32)], axis=2)
            w_uq = uq[:, :, q_src].reshape(MLA_Q_LORA, N_HEADS * LANES).astype(BF16)
            kvd = MLA_QK_NOPE + MLA_V_DIM
            ukv = jnp.concatenate([mla_w_ukv[j].reshape(MLA_KV_LORA, N_HEADS, kvd),
                                   jnp.zeros((MLA_KV_LORA, N_HEADS, 1), F32)], axis=2)
            k_src = np.where(nope_src >= 0, nope_src, kvd)
            w_ukv = jnp.concatenate([ukv[:, :, k_src].reshape(MLA_KV_LORA, N_HEADS * LANES),
                                     ukv[:, :, MLA_QK_NOPE:kvd].reshape(MLA_KV_LORA, N_HEADS * MLA_V_DIM)],
                                    axis=1).astype(BF16)
            q, k, v = _mla_proj(xs, mods_i, norm_g[i, 0], w_a, mla_q_norm_g[j], mla_kv_norm_g[j],
                                w_uq, w_ukv, mla_tab)
            o = _attention(q, k, v, N_HEADS // 2, 1, True, False, None, need_ctx)
            w_o = mla_w_o[j].astype(BF16)
        elif kind == 1:
            cols, o_rows = _gqa_perm(SWA_KV_HEADS, N_HEADS // SWA_KV_HEADS)
            q, k, v = _gqa_proj(xs, mods_i, norm_g[i, 0], swa_w_qkv[j][:, cols].astype(BF16), gqa_tab,
                                SWA_KV_HEADS)
            o = _attention(q, k, v, SWA_KV_HEADS // 2, N_HEADS // SWA_KV_HEADS, False, True,
                           swa_sinks[j], need_ctx)
            w_o = swa_w_o[j][o_rows].astype(BF16)
        else:
            cols, o_rows = _gqa_perm(GA_KV_HEADS, N_HEADS // GA_KV_HEADS)
            q, k, v = _gqa_proj(xs, mods_i, norm_g[i, 0], ga_w_qkv[j][:, cols].astype(BF16), gqa_tab,
                                GA_KV_HEADS, (ga_q_norm_g[j], ga_k_norm_g[j]))
            o = _attention(q, k, v, GA_KV_HEADS // 2, N_HEADS // GA_KV_HEADS, False, False, None, need_ctx)
            w_o = ga_w_o[j][o_rows].astype(BF16)

        n_tok_b = nb if need_ctx else b
        f = i // 2
        if i % 2 == 0:
            xs, h2 = _post(o, xs, mods_i, norm_g[i, 1], w_o, n_tok_b)
            xs = _ffn(h2, xs, mods_i, ffn_w_gate_up[f].astype(BF16), ffn_w_down[f].astype(BF16), n_tok_b)
        else:
            rw = jnp.concatenate([moe_router_w[f], jnp.zeros((d, LANES - N_EXPERTS), F32)], axis=1)
            rw_hi = rw.astype(BF16)
            rw_lo = (rw - rw_hi.astype(F32)).astype(BF16)
            rb = jnp.concatenate([moe_router_b[f], jnp.zeros((LANES - N_EXPERTS,), F32)]).reshape(1, LANES)
            xs, h2, comb = _post(o, xs, mods_i, norm_g[i, 1], w_o, n_tok_b,
                                 (jnp.concatenate([rw_hi, rw_lo], axis=1), rb))
            xs = _moe_layer(h2, comb, xs, mods_i, moe_w_gate_up[f].astype(BF16), moe_w_down[f].astype(BF16),
                            n_tok_b, None if need_ctx else final_norm_g)
    return xs
```

```python
import functools
import math

import numpy as np
import jax
import jax.numpy as jnp
from jax import lax
from jax.experimental import pallas as pl
from jax.experimental.pallas import tpu as pltpu

F32 = jnp.float32
BF16 = jnp.bfloat16

D_MODEL = 1024
GRID_W = 64
HEAD_DIM = 64
N_HEADS = 16
MLA_Q_LORA = 384
MLA_KV_LORA = 256
MLA_QK_NOPE = 64
MLA_QK_ROPE = 32
MLA_V_DIM = 64
SWA_KV_HEADS = 4
SWA_WINDOW = 128
GA_KV_HEADS = 8
D_FF = 2816
N_EXPERTS = 8
EXPERT_FF = 1408
ROPE_THETA = 10000.0
NORM_EPS = 1e-6
NEG_INF = -1e30
DEPTH = 4
LOG2E = math.log2(math.e)

LANES = 128
VMEM_LIMIT = 56 * 2**20
ROW_TILE = 512
Q_TILE = 512
KEY_CHUNK = 512
MOE_TILE = 512
FF_CHUNK = 1408


def _cparams(*sem):
    return pltpu.CompilerParams(dimension_semantics=sem, vmem_limit_bytes=VMEM_LIMIT)


def _silu(x):
    return x * (1.0 / (1.0 + jnp.exp(-x)))


def _rms(x, g):
    ms = jnp.mean(x * x, axis=-1, keepdims=True)
    return x * lax.rsqrt(ms + NORM_EPS) * g


def _norm_mod(x, g, shift, scale):
    return _rms(x, g) * (1.0 + scale) + shift


def _dot(a, b):
    return jnp.dot(a, b, preferred_element_type=F32)


def _dot_t(a, b):
    return lax.dot_general(a, b, (((1,), (1,)), ((), ())), preferred_element_type=F32)


def _mods_body(c_ref, w_ref, b_ref, o_ref):
    sc = _silu(c_ref[...]).astype(BF16)
    o_ref[0] = _dot(sc, w_ref[0].astype(BF16)) + b_ref[0]


def _mods(c_all, ada_w, ada_b):
    depth, d, n = ada_w.shape
    rows = c_all.shape[0]
    tn = 1536
    return pl.pallas_call(
        _mods_body,
        grid=(depth, n // tn),
        in_specs=[
            pl.BlockSpec((rows, d), lambda i, j: (0, 0)),
            pl.BlockSpec((1, d, tn), lambda i, j: (i, 0, j)),
            pl.BlockSpec((1, 1, tn), lambda i, j: (i, 0, j)),
        ],
        out_specs=pl.BlockSpec((1, rows, tn), lambda i, j: (i, 0, j)),
        out_shape=jax.ShapeDtypeStruct((depth, rows, n), F32),
        compiler_params=_cparams("arbitrary", "arbitrary"),
        name="adaln_mods",
    )(c_all, ada_w, ada_b.reshape(depth, 1, n))


def _pair_lanes():
    lane = np.arange(LANES)
    is_b = (lane % 64) >= 32
    dim = (lane % 32) + 32 * (lane // 64)
    return is_b, dim


def _gqa_perm(n_kv, group):
    is_b, dim = _pair_lanes()
    nq = N_HEADS * HEAD_DIM
    nk = n_kv * HEAD_DIM
    q_cols, k_cols, o_rows = [], [], []
    nat = np.arange(HEAD_DIM)
    for j in range(n_kv // 2):
        for g in range(group):
            head_a, head_b = (2 * j) * group + g, (2 * j + 1) * group + g
            q_cols.append(np.where(is_b, head_b, head_a) * HEAD_DIM + dim)
            o_rows.append(np.concatenate([head_a * HEAD_DIM + nat, head_b * HEAD_DIM + nat]))
        k_cols.append(nq + np.where(is_b, 2 * j + 1, 2 * j) * HEAD_DIM + dim)
    cols = np.concatenate(q_cols + k_cols + [nq + nk + np.arange(nk)])
    return cols.astype(np.int32), np.concatenate(o_rows).astype(np.int32)


def _angles(s, rot_dim):
    n_freq = rot_dim // 4
    inv = ROPE_THETA ** (-jnp.arange(n_freq, dtype=F32) / n_freq)
    pos = jnp.arange(s)
    rows = (pos // GRID_W).astype(F32)
    cols = (pos % GRID_W).astype(F32)
    return jnp.concatenate([rows[:, None] * inv, cols[:, None] * inv], axis=-1)


def _rope_tables(cos_l, sin_l, q_scale):
    lat = jnp.stack([cos_l * q_scale, sin_l * q_scale, cos_l, sin_l])
    one = jnp.ones_like(cos_l)
    zero = jnp.zeros_like(cos_l)
    ctx = jnp.stack([one * q_scale, zero, one, zero])
    return jnp.stack([lat, ctx])


def _gqa_tables(s):
    ang = _angles(s, HEAD_DIM)
    lane = np.arange(LANES)
    idx = lane % 32
    sign = jnp.asarray(np.where(lane < 64, -1.0, 1.0), F32)
    return _rope_tables(jnp.cos(ang)[:, idx], jnp.sin(ang)[:, idx] * sign, HEAD_DIM ** -0.5 * LOG2E)


def _mla_lane_src():
    lane = np.arange(LANES)
    nope = np.where((lane >= 16) & (lane < 64), lane - 16,
                    np.where((lane >= 80) & (lane < 96), 48 + lane - 80, -1))
    rope = np.where(lane < 16, lane, np.where((lane >= 64) & (lane < 80), 16 + lane - 64, -1))
    return nope, rope


def _mla_tables(s):
    ang = _angles(s, MLA_QK_ROPE)
    lane = np.arange(LANES)
    is_x1 = lane < 16
    is_x2 = (lane >= 64) & (lane < 80)
    idx = np.where(is_x1, lane, np.where(is_x2, lane - 64, 0))
    rot = jnp.asarray(is_x1 | is_x2)
    sign = jnp.asarray(np.where(is_x1, -1.0, np.where(is_x2, 1.0, 0.0)), F32)
    cos_l = jnp.where(rot, jnp.cos(ang)[:, idx], 1.0)
    sin_l = jnp.sin(ang)[:, idx] * sign
    return _rope_tables(cos_l, sin_l, (MLA_QK_NOPE + MLA_QK_ROPE) ** -0.5 * LOG2E)


def _rope(blk, cos, sin):
    return blk * cos + pltpu.roll(blk, 64, 1) * sin


def _gqa_proj_body(*refs, nqb, nkb, qk_norm):
    if qk_norm:
        x_ref, mod_ref, g_ref, w_ref, tab_ref, gq_ref, gk_ref, ind_ref, q_ref, k_ref, v_ref = refs
    else:
        x_ref, mod_ref, g_ref, w_ref, tab_ref, q_ref, k_ref, v_ref = refs
    h = _norm_mod(x_ref[0], g_ref[...], mod_ref[0, 0:1, :], mod_ref[0, 1:2, :]).astype(BF16)
    qkv = _dot(h, w_ref[...])
    cq, sq, ck, sk = tab_ref[0, 0], tab_ref[0, 1], tab_ref[0, 2], tab_ref[0, 3]

    def head_norm(blk, gain):
        ssq = _dot((blk * blk).astype(BF16), ind_ref[...])
        return blk * lax.rsqrt(ssq * (1.0 / HEAD_DIM) + NORM_EPS) * gain

    for c in range(nqb):
        blk = qkv[:, c * LANES:(c + 1) * LANES]
        if qk_norm:
            blk = head_norm(blk, gq_ref[...])
        q_ref[0, :, c * LANES:(c + 1) * LANES] = _rope(blk, cq, sq).astype(BF16)
    for c in range(nkb):
        blk = qkv[:, (nqb + c) * LANES:(nqb + c + 1) * LANES]
        if qk_norm:
            blk = head_norm(blk, gk_ref[...])
        k_ref[0, :, c * LANES:(c + 1) * LANES] = _rope(blk, ck, sk).astype(BF16)
    v_ref[0] = qkv[:, (nqb + nkb) * LANES:].astype(BF16)


def _gqa_proj(xs, mods_i, norm_g, w_perm, tables, n_kv, qk_gains=None):
    nb, s, d = xs.shape
    tm = min(ROW_TILE, s)
    nqb = N_HEADS * HEAD_DIM // LANES
    nkb = n_kv * HEAD_DIM // LANES
    wn = w_perm.shape[1]
    in_specs = [
        pl.BlockSpec((1, tm, d), lambda b, i: (b, i, 0)),
        pl.BlockSpec((1, 6, d), lambda b, i: (b, 0, 0)),
        pl.BlockSpec((1, d), lambda b, i: (0, 0)),
        pl.BlockSpec((d, wn), lambda b, i: (0, 0)),
        pl.BlockSpec((1, 4, tm, LANES), lambda b, i: (b // (nb - 1), 0, i, 0)),
    ]
    args = [xs, mods_i, norm_g.reshape(1, d), w_perm, tables]
    if qk_gains is not None:
        is_b, dim = _pair_lanes()
        ind = jnp.asarray(is_b[:, None] == is_b[None, :], BF16)
        in_specs += [pl.BlockSpec((1, LANES), lambda b, i: (0, 0)),
                     pl.BlockSpec((1, LANES), lambda b, i: (0, 0)),
                     pl.BlockSpec((LANES, LANES), lambda b, i: (0, 0))]
        args += [qk_gains[0][dim].reshape(1, LANES), qk_gains[1][dim].reshape(1, LANES), ind]
    out_w = (nqb * LANES, nkb * LANES, nkb * LANES)
    return pl.pallas_call(
        functools.partial(_gqa_proj_body, nqb=nqb, nkb=nkb, qk_norm=qk_gains is not None),
        grid=(nb, s // tm),
        in_specs=in_specs,
        out_specs=[pl.BlockSpec((1, tm, w), lambda b, i: (b, i, 0)) for w in out_w],
        out_shape=[jax.ShapeDtypeStruct((nb, s, w), BF16) for w in out_w],
        compiler_params=_cparams("arbitrary", "arbitrary"),
        name="gqa_proj",
    )(*args)


def _mla_proj_body(x_ref, mod_ref, g_ref, wa_ref, gq_ref, gkv_ref, wuq_ref, wukv_ref, tab_ref,
                   q_ref, k_ref, v_ref):
    h = _norm_mod(x_ref[0], g_ref[...], mod_ref[0, 0:1, :], mod_ref[0, 1:2, :]).astype(BF16)
    a = _dot(h, wa_ref[...])
    qn = _rms(a[:, :MLA_Q_LORA], gq_ref[...]).astype(BF16)
    cn = _rms(a[:, MLA_Q_LORA:MLA_Q_LORA + MLA_KV_LORA], gkv_ref[...]).astype(BF16)
    kr = a[:, MLA_Q_LORA + MLA_KV_LORA:]
    cq, sq, ck, sk = tab_ref[0, 0], tab_ref[0, 1], tab_ref[0, 2], tab_ref[0, 3]
    q = _dot(qn, wuq_ref[...])
    kv = _dot(cn, wukv_ref[...])
    kr = _rope(kr, ck, sk)
    for hh in range(N_HEADS):
        sl = slice(hh * LANES, (hh + 1) * LANES)
        q_ref[0, :, sl] = _rope(q[:, sl], cq, sq).astype(BF16)
        k_ref[0, :, sl] = (kv[:, sl] + kr).astype(BF16)
    v_ref[0] = kv[:, N_HEADS * LANES:].astype(BF16)


def _mla_proj(xs, mods_i, norm_g, w_a, gq, gkv, w_uq, w_ukv, tables):
    nb, s, d = xs.shape
    tm = min(ROW_TILE, s)
    full = lambda arr: pl.BlockSpec(arr.shape, lambda b, i: (0,) * arr.ndim)
    gq = gq.reshape(1, -1)
    gkv = gkv.reshape(1, -1)
    g = norm_g.reshape(1, d)
    out_w = (N_HEADS * LANES, N_HEADS * LANES, N_HEADS * MLA_V_DIM)
    return pl.pallas_call(
        _mla_proj_body,
        grid=(nb, s // tm),
        in_specs=[
            pl.BlockSpec((1, tm, d), lambda b, i: (b, i, 0)),
            pl.BlockSpec((1, 6, d), lambda b, i: (b, 0, 0)),
            full(g), full(w_a), full(gq), full(gkv), full(w_uq), full(w_ukv),
            pl.BlockSpec((1, 4, tm, LANES), lambda b, i: (b // (nb - 1), 0, i, 0)),
        ],
        out_specs=[pl.BlockSpec((1, tm, w), lambda b, i: (b, i, 0)) for w in out_w],
        out_shape=[jax.ShapeDtypeStruct((nb, s, w), BF16) for w in out_w],
        compiler_params=_cparams("arbitrary", "arbitrary"),
        name="mla_proj",
    )(xs, mods_i, g, w_a, gq, gkv, w_uq, w_ukv, tables)


def _attn_body(*refs, group, ql, split, mode, tq, band, seq, use_sink):
    refs = list(refs)
    sink_ref = refs.pop(0) if use_sink else None
    q_ref = refs.pop(0)
    if mode != "ctx":
        kl_ref, vl_ref = refs.pop(0), refs.pop(0)
    kc_ref, vc_ref, o_ref = refs
    j = pl.program_id(1)
    qi = pl.program_id(2)
    lane = lax.broadcasted_iota(jnp.int32, (1, LANES), 1)
    in_a = (lane & 63) < 32
    row = lax.broadcasted_iota(jnp.int32, (LANES, 1), 0)
    top = row < 64

    def values_t(v):
        vt = v.astype(F32).T
        return (jnp.where(top, vt, jnp.where(row == 64, 1.0, 0.0)).astype(BF16),
                jnp.where(top, jnp.where(row == 0, 1.0, 0.0), vt).astype(BF16))

    kc = kc_ref[0]
    vc = values_t(vc_ref[0])
    valid = None
    if mode == "dense":
        kl = kl_ref[0]
        vl = values_t(vl_ref[0])
    elif mode == "window":
        start = jnp.clip(qi * tq - SWA_WINDOW, 0, seq - band)
        start = pl.multiple_of(start, LANES)
        kl = kl_ref[0, pl.ds(start, band), :]
        vl = values_t(vl_ref[0, pl.ds(start, band), :])
        kpos = start + lax.broadcasted_iota(jnp.int32, (band, 1), 0)
        qpos = qi * tq + lax.broadcasted_iota(jnp.int32, (1, tq), 1)
        valid = jnp.abs(qpos - kpos) <= SWA_WINDOW

    chunks = [(kc, vc, None)]
    if mode != "ctx":
        for c0 in range(0, kl.shape[0], KEY_CHUNK):
            cs = slice(c0, c0 + KEY_CHUNK)
            chunks.append((kl[cs], (vl[0][:, cs], vl[1][:, cs]), None if valid is None else valid[cs]))

    def head_q(g, half):
        q = q_ref[0, :, g * ql:(g + 1) * ql]
        if split:
            return q[:, half * LANES:(half + 1) * LANES]
        return jnp.where(in_a if half == 0 else jnp.logical_not(in_a), q, jnp.zeros_like(q))

    def scores(item):
        g, half, ci = item
        kch, _, ok = chunks[ci]
        if split:
            kch = kch[:, half * LANES:(half + 1) * LANES]
        s = _dot_t(kch, head_q(g, half))
        return s if ok is None else jnp.where(ok, s, NEG_INF)

    items = [(g, half, ci) for g in range(group) for half in range(2) for ci in range(len(chunks))]
    last = len(chunks) - 1
    outs = []
    acc = None

    def values_step(pend, acc):
        (g, half, ci), p, alpha, m_end, sink = pend
        pv = _dot(chunks[ci][1][half], p)
        acc = pv if ci == 0 else acc * alpha + pv
        if ci == last:
            den = acc[64:65, :] if half == 0 else acc[0:1, :]
            if sink is not None:
                den = den + jnp.exp2(sink - m_end)
            outs.append(acc * (1.0 / den))
            if half == 1:
                o_t = jnp.where(top, outs[0], outs[1])
                o_ref[0, :, g * LANES:(g + 1) * LANES] = o_t.T.astype(BF16)
                outs.clear()
        return acc

    s_next = scores(items[0])
    pending = None
    for idx, item in enumerate(items):
        g, half, ci = item
        s = s_next
        if idx + 1 < len(items):
            s_next = scores(items[idx + 1])
        cm = jnp.max(s, axis=0, keepdims=True)
        if ci == 0:
            sink = sink_ref[(2 * j + half) * group + g] * LOG2E if use_sink else None
            m = cm if sink is None else jnp.maximum(cm, sink)
            alpha = None
        else:
            m_new = jnp.maximum(m, cm)
            alpha = jnp.exp2(m - m_new)
            m = m_new
        p = jnp.exp2(s - m).astype(BF16)
        if pending is not None:
            acc = values_step(pending, acc)
        pending = (item, p, alpha, m, sink)
    values_step(pending, acc)


def _attention(q, k, v, n_pairs, group, split, window, sinks, need_ctx):
    nb, s, _ = q.shape
    b_lat = nb - 1
    ctx_len = s // b_lat
    ql = q.shape[2] // (n_pairs * group)
    klw = k.shape[2] // n_pairs
    ow = n_pairs * group * LANES
    tq = min(Q_TILE, s)
    band = min(tq + 2 * SWA_WINDOW, s)
    use_sink = sinks is not None
    common = dict(group=group, ql=ql, split=split, tq=tq, band=band, seq=s, use_sink=use_sink)
    smem = [pl.BlockSpec(memory_space=pltpu.SMEM)] if use_sink else []
    sink_args = [sinks] if use_sink else []

    lat_specs = smem + [
        pl.BlockSpec((1, tq, group * ql), lambda b, j, i: (b, i, j)),
        pl.BlockSpec((1, s, klw), lambda b, j, i: (b, 0, j)),
        pl.BlockSpec((1, s, LANES), lambda b, j, i: (b, 0, j)),
        pl.BlockSpec((1, ctx_len, klw), lambda b, j, i: (b_lat, b, j)),
        pl.BlockSpec((1, ctx_len, LANES), lambda b, j, i: (b_lat, b, j)),
    ]
    o_lat = pl.pallas_call(
        functools.partial(_attn_body, mode="window" if window else "dense", **common),
        grid=(b_lat, n_pairs, s // tq),
        in_specs=lat_specs,
        out_specs=pl.BlockSpec((1, tq, group * LANES), lambda b, j, i: (b, i, j)),
        out_shape=jax.ShapeDtypeStruct((b_lat, s, ow), BF16),
        compiler_params=_cparams("arbitrary", "arbitrary", "arbitrary"),
        name="attn_latent",
    )(*sink_args, q, k, v, k, v)
    if not need_ctx:
        return o_lat, None

    common["tq"] = ctx_len
    ctx_specs = smem + [
        pl.BlockSpec((1, ctx_len, group * ql), lambda b, j, i: (b_lat, b, j)),
        pl.BlockSpec((1, ctx_len, klw), lambda b, j, i: (b_lat, b, j)),
        pl.BlockSpec((1, ctx_len, LANES), lambda b, j, i: (b_lat, b, j)),
    ]
    o_ctx = pl.pallas_call(
        functools.partial(_attn_body, mode="ctx", **common),
        grid=(b_lat, n_pairs, 1),
        in_specs=ctx_specs,
        out_specs=pl.BlockSpec((1, ctx_len, group * LANES), lambda b, j, i: (0, b, j)),
        out_shape=jax.ShapeDtypeStruct((1, s, ow), BF16),
        compiler_params=_cparams("arbitrary", "arbitrary", "arbitrary"),
        name="attn_context",
    )(*sink_args, q, k, v)
    return o_lat, o_ctx


def _post_body(*refs, moe, b_lat, has_ctx):
    refs = list(refs)
    o_ref = refs.pop(0)
    oc_ref = refs.pop(0) if has_ctx else None
    if moe:
        x_ref, mod_ref, g_ref, wo_ref, rw_ref, rb_ref, xo_ref, h_ref, route_ref = refs
    else:
        x_ref, mod_ref, g_ref, wo_ref, xo_ref, h_ref = refs
    o = o_ref[0]
    if has_ctx:
        o = jnp.where(pl.program_id(0) == b_lat, oc_ref[0], o)
    y = _dot(o, wo_ref[...])
    x = x_ref[0] + mod_ref[0, 2:3, :] * y
    xo_ref[0] = x
    h = _norm_mod(x, g_ref[...], mod_ref[0, 3:4, :], mod_ref[0, 4:5, :])
    h_hi = h.astype(BF16)
    if not moe:
        h_ref[0] = h_hi
        return
    h_ref[0] = h
    h_lo = (h - h_hi.astype(F32)).astype(BF16)
    both = _dot(h_hi, rw_ref[...])
    logits = both[:, :LANES] + both[:, LANES:] + _dot(h_lo, rw_ref[:, :LANES]) + rb_ref[...]
    lane = lax.broadcasted_iota(jnp.int32, logits.shape, 1).astype(F32)
    lg = jnp.where(lane < N_EXPERTS, logits, -jnp.inf)
    v1 = jnp.max(lg, axis=-1, keepdims=True)
    i1 = jnp.min(jnp.where(lg == v1, lane, float(LANES)), axis=-1, keepdims=True)
    lg2 = jnp.where(lane == i1, -jnp.inf, lg)
    v2 = jnp.max(lg2, axis=-1, keepdims=True)
    i2 = jnp.min(jnp.where(lg2 == v2, lane, float(LANES)), axis=-1, keepdims=True)
    e = jnp.exp(v2 - v1)
    g1 = 1.0 / (1.0 + e)
    g2 = e / (1.0 + e)
    route_ref[0] = jnp.where(lane == 0, i1, jnp.where(lane == 1, i2, jnp.where(
        lane == 2, g1, jnp.where(lane == 3, g2, 0.0))))


def _post(o_lat, o_ctx, xs, mods_i, norm_g, w_o, nb, router=None):
    _, s, d = xs.shape
    tm = min(ROW_TILE, s)
    moe = router is not None
    has_ctx = o_ctx is not None
    b_lat = o_lat.shape[0]
    in_specs = [pl.BlockSpec((1, tm, d), lambda b, i: (jnp.minimum(b, b_lat - 1), i, 0))]
    args = [o_lat]
    if has_ctx:
        in_specs.append(pl.BlockSpec((1, tm, d), lambda b, i: (0, i, 0)))
        args.append(o_ctx)
    n_o = len(args)
    in_specs += [
        pl.BlockSpec((1, tm, d), lambda b, i: (b, i, 0)),
        pl.BlockSpec((1, 6, d), lambda b, i: (b, 0, 0)),
        pl.BlockSpec((1, d), lambda b, i: (0, 0)),
        pl.BlockSpec((d, d), lambda b, i: (0, 0)),
    ]
    args += [xs, mods_i, norm_g.reshape(1, d), w_o]
    out_specs = [pl.BlockSpec((1, tm, d), lambda b, i: (b, i, 0)),
                 pl.BlockSpec((1, tm, d), lambda b, i: (b, i, 0))]
    out_shape = [jax.ShapeDtypeStruct(xs.shape, F32), jax.ShapeDtypeStruct((nb, s, d), BF16)]
    if moe:
        in_specs += [pl.BlockSpec((d, 2 * LANES), lambda b, i: (0, 0)),
                     pl.BlockSpec((1, LANES), lambda b, i: (0, 0))]
        args += list(router)
        out_shape[1] = jax.ShapeDtypeStruct((nb, s, d), F32)
        out_specs.append(pl.BlockSpec((1, tm, LANES), lambda b, i: (b, i, 0)))
        out_shape.append(jax.ShapeDtypeStruct((nb, s, LANES), F32))
    return pl.pallas_call(
        functools.partial(_post_body, moe=moe, b_lat=b_lat, has_ctx=has_ctx),
        grid=(nb, s // tm),
        in_specs=in_specs,
        out_specs=out_specs,
        out_shape=out_shape,
        input_output_aliases={n_o: 0},
        compiler_params=_cparams("arbitrary", "arbitrary"),
        name="attn_out_norm",
    )(*args)


def _ffn_body(h_ref, x_ref, mod_ref, wgu_ref, wd_ref, xo_ref):
    h = h_ref[0]
    acc = None
    for c in range(D_FF // FF_CHUNK):
        gate = _dot(h, wgu_ref[:, c * FF_CHUNK:(c + 1) * FF_CHUNK])
        up = _dot(h, wgu_ref[:, D_FF + c * FF_CHUNK:D_FF + (c + 1) * FF_CHUNK])
        act = (_silu(gate) * up).astype(BF16)
        part = _dot(act, wd_ref[c * FF_CHUNK:(c + 1) * FF_CHUNK, :])
        acc = part if acc is None else acc + part
    xo_ref[0] = x_ref[0] + mod_ref[0, 5:6, :] * acc


def _ffn(h2, xs, mods_i, w_gu, w_d, nb):
    _, s, d = xs.shape
    tm = min(ROW_TILE, s)
    resident = lambda arr: pl.BlockSpec(arr.shape, lambda b, i: (0, 0), pipeline_mode=pl.Buffered(1))
    return pl.pallas_call(
        _ffn_body,
        grid=(nb, s // tm),
        in_specs=[
            pl.BlockSpec((1, tm, d), lambda b, i: (b, i, 0)),
            pl.BlockSpec((1, tm, d), lambda b, i: (b, i, 0)),
            pl.BlockSpec((1, 6, d), lambda b, i: (b, 0, 0)),
            resident(w_gu), resident(w_d),
        ],
        out_specs=pl.BlockSpec((1, tm, d), lambda b, i: (b, i, 0)),
        out_shape=jax.ShapeDtypeStruct(xs.shape, F32),
        input_output_aliases={1: 0},
        compiler_params=_cparams("arbitrary", "arbitrary"),
        name="ffn_dense",
    )(h2, xs, mods_i, w_gu, w_d)


def _row_dma_wait(src_ref, dst_ref, sem, rows):
    pltpu.make_async_copy(src_ref.at[pl.ds(0, rows)], dst_ref.at[pl.ds(0, rows)], sem).wait()


def _dispatch_body(pos_ref, h_ref, xs_in_ref, xs_ref, sem):
    del xs_in_ref
    tm = h_ref.shape[0]

    def issue(r, carry):
        for k in range(2):
            dst = pos_ref[0, 0, 2 * r + k]
            pltpu.make_async_copy(h_ref.at[pl.ds(r, 1)], xs_ref.at[pl.ds(dst, 1)], sem).start()
        return carry

    lax.fori_loop(0, tm, issue, 0, unroll=8)
    _row_dma_wait(h_ref, xs_ref, sem, tm)
    _row_dma_wait(h_ref, xs_ref, sem, tm)


def _moe_dispatch(hp, pos, p, tm):
    t, w = hp.shape
    return pl.pallas_call(
        _dispatch_body,
        grid=(t // tm,),
        in_specs=[
            pl.BlockSpec((1, 1, 2 * tm), lambda i: (i, 0, 0), memory_space=pltpu.SMEM),
            pl.BlockSpec((tm, w), lambda i: (i, 0)),
            pl.BlockSpec(memory_space=pl.ANY),
        ],
        out_specs=pl.BlockSpec(memory_space=pl.ANY),
        out_shape=jax.ShapeDtypeStruct((p, w), hp.dtype),
        scratch_shapes=[pltpu.SemaphoreType.DMA(())],
        input_output_aliases={2: 0},
        compiler_params=_cparams("arbitrary"),
        name="moe_dispatch",
    )(pos.reshape(t // tm, 1, 2 * tm), hp, jnp.zeros((p, w), hp.dtype))


def _moe_body(te_ref, nu_ref, xs_ref, wg_ref, wu_ref, wd_ref, ys_ref):
    i = pl.program_id(0)

    @pl.when(i < nu_ref[0])
    def _():
        x = xs_ref[...].astype(BF16)
        act = (_silu(_dot(x, wg_ref[0])) * _dot(x, wu_ref[0])).astype(BF16)
        ys_ref[...] = _dot(act, wd_ref[0])

    @pl.when(i >= nu_ref[0])
    def _():
        ys_ref[...] = jnp.zeros_like(ys_ref)


def _moe_experts(xs_sorted, tile_expert, n_used, w_gu, w_d, tm):
    p = xs_sorted.shape[0]
    f, d = w_d.shape[1:]
    grid_spec = pltpu.PrefetchScalarGridSpec(
        num_scalar_prefetch=2,
        grid=(p // tm,),
        in_specs=[
            pl.BlockSpec((tm, d), lambda i, te, nu: (i, 0)),
            pl.BlockSpec((1, d, f), lambda i, te, nu: (te[i], 0, 0)),
            pl.BlockSpec((1, d, f), lambda i, te, nu: (te[i], 0, 1)),
            pl.BlockSpec((1, f, d), lambda i, te, nu: (te[i], 0, 0)),
        ],
        out_specs=pl.BlockSpec((tm, d), lambda i, te, nu: (i, 0)),
    )
    return pl.pallas_call(
        _moe_body,
        grid_spec=grid_spec,
        out_shape=jax.ShapeDtypeStruct((p, d), F32),
        compiler_params=_cparams("arbitrary"),
        name="moe_experts",
    )(tile_expert, n_used, xs_sorted, w_gu, w_gu, w_d)


def _combine_body(*refs, final):
    if final:
        pos_ref, x_ref, route_ref, mod_ref, g_ref, ys_ref, o_ref, ybuf, sem = refs
    else:
        pos_ref, x_ref, route_ref, mod_ref, ys_ref, o_ref, ybuf, sem = refs
    tm = x_ref.shape[1]

    def issue(r, carry):
        for k in range(2):
            src = pos_ref[0, 0, 2 * r + k]
            pltpu.make_async_copy(ys_ref.at[pl.ds(src, 1)], ybuf.at[pl.ds(k * tm + r, 1)], sem).start()
        return carry

    lax.fori_loop(0, tm, issue, 0, unroll=8)
    _row_dma_wait(ys_ref, ybuf, sem, 2 * tm)
    route = route_ref[0]
    y = route[:, 2:3] * ybuf[0:tm, :] + route[:, 3:4] * ybuf[tm:2 * tm, :]
    x = x_ref[0] + mod_ref[0, 5:6, :] * y
    o_ref[0] = _rms(x, g_ref[...]) if final else x


def _moe_combine(xs, ys, pos, route, mods_i, nb, final_g=None):
    _, s, d = xs.shape
    tm = min(ROW_TILE, s)
    per_b = s // tm
    final = final_g is not None
    tok = pl.BlockSpec((1, tm, d), lambda b, i: (b, i, 0))
    in_specs = [pl.BlockSpec((1, 1, 2 * tm), lambda b, i: (b * per_b + i, 0, 0), memory_space=pltpu.SMEM),
                tok,
                pl.BlockSpec((1, tm, LANES), lambda b, i: (b, i, 0)),
                pl.BlockSpec((1, 6, d), lambda b, i: (b, 0, 0))]
    args = [pos.reshape(nb * per_b, 1, 2 * tm), xs, route, mods_i]
    if final:
        in_specs.append(pl.BlockSpec((1, d), lambda b, i: (0, 0)))
        args.append(final_g.reshape(1, d))
    in_specs.append(pl.BlockSpec(memory_space=pl.ANY))
    args.append(ys)
    return pl.pallas_call(
        functools.partial(_combine_body, final=final),
        grid=(nb, per_b),
        in_specs=in_specs,
        out_specs=tok,
        out_shape=jax.ShapeDtypeStruct((nb, s, d) if final else xs.shape, F32),
        scratch_shapes=[pltpu.VMEM((2 * tm, d), F32), pltpu.SemaphoreType.DMA(())],
        input_output_aliases={} if final else {1: 0},
        compiler_params=_cparams("arbitrary", "arbitrary"),
        name="moe_combine",
    )(*args)


def _moe_layer(hp, route, xs, mods_i, w_gu, w_d, nb, final_g):
    _, s, d = xs.shape
    t = nb * s
    tm = min(MOE_TILE, s)
    p = 2 * t + N_EXPERTS * tm
    expert = route.reshape(t, LANES)[:, :2].astype(jnp.int32).reshape(2 * t)
    onehot = (expert[:, None] == jnp.arange(N_EXPERTS, dtype=jnp.int32)[None, :]).astype(jnp.int32)
    csum = jnp.cumsum(onehot, axis=0)
    padded = ((csum[-1] + tm - 1) // tm) * tm
    off_end = jnp.cumsum(padded)
    pos = jnp.sum(onehot * ((off_end - padded)[None, :] + csum - 1), axis=1)
    tile_start = jnp.arange(p // tm, dtype=jnp.int32) * tm
    tile_expert = jnp.minimum(jnp.sum(tile_start[:, None] >= off_end[None, :], axis=1), N_EXPERTS - 1)
    n_used = (off_end[-1] // tm).reshape(1).astype(jnp.int32)

    xs_sorted = _moe_dispatch(hp.reshape(t, d), pos, p, tm)
    ys = _moe_experts(xs_sorted, tile_expert.astype(jnp.int32), n_used, w_gu, w_d, tm)
    return _moe_combine(xs, ys, pos, route, mods_i, nb, final_g)


def kernel(x, c, ctx, c_ctx, ada_w, ada_b, norm_g, final_norm_g, mla_w_dq, mla_q_norm_g, mla_w_uq, mla_w_dkv, mla_kv_norm_g, mla_w_ukv, mla_w_o, swa_w_qkv, swa_sinks, swa_w_o, ga_w_qkv, ga_q_norm_g, ga_k_norm_g, ga_w_o, ffn_w_gate_up, ffn_w_down, moe_router_w, moe_router_b, moe_w_gate_up, moe_w_down):
    b, s, d = x.shape
    assert d == D_MODEL and b * ctx.shape[1] == s and s % GRID_W == 0
    nb = b + 1
    xs = jnp.concatenate([x, ctx.reshape(1, s, d)], axis=0)
    c_all = jnp.concatenate([c, c_ctx[None], jnp.zeros((16 - nb, d), F32)], axis=0)
    mods = _mods(c_all, ada_w, ada_b).reshape(DEPTH, 16, 6, d)

    gqa_tab = _gqa_tables(s)
    mla_tab = _mla_tables(s)
    nope_src, rope_src = _mla_lane_src()

    for i in range(DEPTH):
        need_ctx = i < DEPTH - 1
        mods_i = mods[i]
        kind, j = i % 3, i // 3
        if kind == 0:
            dkv = jnp.concatenate([mla_w_dkv[j], jnp.zeros((d, 1), F32)], axis=1)
            kr_cols = np.where(rope_src >= 0, MLA_KV_LORA + rope_src, MLA_KV_LORA + MLA_QK_ROPE)
            w_a = jnp.concatenate([mla_w_dq[j], dkv[:, :MLA_KV_LORA], dkv[:, kr_cols]], axis=1).astype(BF16)
            qd = MLA_QK_NOPE + MLA_QK_ROPE
            q_src = np.where(nope_src >= 0, nope_src, np.where(rope_src >= 0, MLA_QK_NOPE + rope_src, qd))
            uq = jnp.concatenate([mla_w_uq[j].reshape(MLA_Q_LORA, N_HEADS, qd),
                                  jnp.zeros((MLA_Q_LORA, N_HEADS, 1), F32)], axis=2)
            w_uq = uq[:, :, q_src].reshape(MLA_Q_LORA, N_HEADS * LANES).astype(BF16)
            kvd = MLA_QK_NOPE + MLA_V_DIM
            ukv = jnp.concatenate([mla_w_ukv[j].reshape(MLA_KV_LORA, N_HEADS, kvd),
                                   jnp.zeros((MLA_KV_LORA, N_HEADS, 1), F32)], axis=2)
            k_src = np.where(nope_src >= 0, nope_src, kvd)
            w_ukv = jnp.concatenate([ukv[:, :, k_src].reshape(MLA_KV_LORA, N_HEADS * LANES),
                                     ukv[:, :, MLA_QK_NOPE:kvd].reshape(MLA_KV_LORA, N_HEADS * MLA_V_DIM)],
                                    axis=1).astype(BF16)
            q, k, v = _mla_proj(xs, mods_i, norm_g[i, 0], w_a, mla_q_norm_g[j], mla_kv_norm_g[j],
                                w_uq, w_ukv, mla_tab)
            o = _attention(q, k, v, N_HEADS // 2, 1, True, False, None, need_ctx)
            w_o = mla_w_o[j].astype(BF16)
        elif kind == 1:
            cols, o_rows = _gqa_perm(SWA_KV_HEADS, N_HEADS // SWA_KV_HEADS)
            q, k, v = _gqa_proj(xs, mods_i, norm_g[i, 0], swa_w_qkv[j][:, cols].astype(BF16), gqa_tab,
                                SWA_KV_HEADS)
            o = _attention(q, k, v, SWA_KV_HEADS // 2, N_HEADS // SWA_KV_HEADS, False, True,
                           swa_sinks[j], need_ctx)
            w_o = swa_w_o[j][o_rows].astype(BF16)
        else:
            cols, o_rows = _gqa_perm(GA_KV_HEADS, N_HEADS // GA_KV_HEADS)
            q, k, v = _gqa_proj(xs, mods_i, norm_g[i, 0], ga_w_qkv[j][:, cols].astype(BF16), gqa_tab,
                                GA_KV_HEADS, (ga_q_norm_g[j], ga_k_norm_g[j]))
            o = _attention(q, k, v, GA_KV_HEADS // 2, N_HEADS // GA_KV_HEADS, False, False, None, need_ctx)
            w_o = ga_w_o[j][o_rows].astype(BF16)

        n_tok_b = nb if need_ctx else b
        f = i // 2
        if i % 2 == 0:
            xs, h2 = _post(*o, xs, mods_i, norm_g[i, 1], w_o, n_tok_b)
            xs = _ffn(h2, xs, mods_i, ffn_w_gate_up[f].astype(BF16), ffn_w_down[f].astype(BF16), n_tok_b)
        else:
            rw = jnp.concatenate([moe_router_w[f], jnp.zeros((d, LANES - N_EXPERTS), F32)], axis=1)
            rw_hi = rw.astype(BF16)
            rw_lo = (rw - rw_hi.astype(F32)).astype(BF16)
            rb = jnp.concatenate([moe_router_b[f], jnp.zeros((LANES - N_EXPERTS,), F32)]).reshape(1, LANES)
            xs, h2, comb = _post(*o, xs, mods_i, norm_g[i, 1], w_o, n_tok_b,
                                 (jnp.concatenate([rw_hi, rw_lo], axis=1), rb))
            xs = _moe_layer(h2, comb, xs, mods_i, moe_w_gate_up[f].astype(BF16), moe_w_down[f].astype(BF16),
                            n_tok_b, None if need_ctx else final_norm_g)
    return xs
```

```python
import functools
import math

import numpy as np
import jax
import jax.numpy as jnp
from jax import lax
from jax.experimental import pallas as pl
from jax.experimental.pallas import tpu as pltpu

F32 = jnp.float32
BF16 = jnp.bfloat16

D_MODEL = 1024
GRID_W = 64
HEAD_DIM = 64
N_HEADS = 16
MLA_Q_LORA = 384
MLA_KV_LORA = 256
MLA_QK_NOPE = 64
MLA_QK_ROPE = 32
MLA_V_DIM = 64
SWA_KV_HEADS = 4
SWA_WINDOW = 128
GA_KV_HEADS = 8
D_FF = 2816
N_EXPERTS = 8
EXPERT_FF = 1408
ROPE_THETA = 10000.0
NORM_EPS = 1e-6
NEG_INF = -1e30
DEPTH = 4
LOG2E = math.log2(math.e)

LANES = 128
VMEM_LIMIT = 56 * 2**20
ROW_TILE = 512
Q_TILE = 512
KEY_CHUNK = 512
HEADS_PER_STEP = 4
MOE_TILE = 512
FF_CHUNK = 1408


def _cparams(*sem):
    return pltpu.CompilerParams(dimension_semantics=sem, vmem_limit_bytes=VMEM_LIMIT)


def _silu(x):
    return x * (1.0 / (1.0 + jnp.exp(-x)))


def _rms(x, g):
    ms = jnp.mean(x * x, axis=-1, keepdims=True)
    return x * lax.rsqrt(ms + NORM_EPS) * g


def _norm_mod(x, g, shift, scale):
    return _rms(x, g) * (1.0 + scale) + shift


def _dot(a, b):
    return jnp.dot(a, b, preferred_element_type=F32)


def _dot_t(a, b):
    return lax.dot_general(a, b, (((1,), (1,)), ((), ())), preferred_element_type=F32)


def _mods_body(c_ref, w_ref, b_ref, o_ref):
    sc = _silu(c_ref[...]).astype(BF16)
    o_ref[0] = _dot(sc, w_ref[0].astype(BF16)) + b_ref[0]


def _mods(c_all, ada_w, ada_b):
    depth, d, n = ada_w.shape
    rows = c_all.shape[0]
    tn = 1536
    return pl.pallas_call(
        _mods_body,
        grid=(depth, n // tn),
        in_specs=[
            pl.BlockSpec((rows, d), lambda i, j: (0, 0)),
            pl.BlockSpec((1, d, tn), lambda i, j: (i, 0, j)),
            pl.BlockSpec((1, 1, tn), lambda i, j: (i, 0, j)),
        ],
        out_specs=pl.BlockSpec((1, rows, tn), lambda i, j: (i, 0, j)),
        out_shape=jax.ShapeDtypeStruct((depth, rows, n), F32),
        compiler_params=_cparams("arbitrary", "arbitrary"),
        name="adaln_mods",
    )(c_all, ada_w, ada_b.reshape(depth, 1, n))


def _pair_lanes():
    lane = np.arange(LANES)
    is_b = (lane % 64) >= 32
    dim = (lane % 32) + 32 * (lane // 64)
    return is_b, dim


def _gqa_perm(n_kv, group):
    is_b, dim = _pair_lanes()
    nq = N_HEADS * HEAD_DIM
    nk = n_kv * HEAD_DIM
    q_cols, k_cols, o_rows = [], [], []
    nat = np.arange(HEAD_DIM)
    for j in range(n_kv // 2):
        for g in range(group):
            head_a, head_b = (2 * j) * group + g, (2 * j + 1) * group + g
            q_cols.append(np.where(is_b, head_b, head_a) * HEAD_DIM + dim)
            o_rows.append(np.concatenate([head_a * HEAD_DIM + nat, head_b * HEAD_DIM + nat]))
        k_cols.append(nq + np.where(is_b, 2 * j + 1, 2 * j) * HEAD_DIM + dim)
    cols = np.concatenate(q_cols + k_cols + [nq + nk + np.arange(nk)])
    return cols.astype(np.int32), np.concatenate(o_rows).astype(np.int32)


def _angles(s, rot_dim):
    n_freq = rot_dim // 4
    inv = ROPE_THETA ** (-jnp.arange(n_freq, dtype=F32) / n_freq)
    pos = jnp.arange(s)
    rows = (pos // GRID_W).astype(F32)
    cols = (pos % GRID_W).astype(F32)
    return jnp.concatenate([rows[:, None] * inv, cols[:, None] * inv], axis=-1)


def _rope_tables(cos_l, sin_l, q_scale):
    lat = jnp.stack([cos_l * q_scale, sin_l * q_scale, cos_l, sin_l])
    one = jnp.ones_like(cos_l)
    zero = jnp.zeros_like(cos_l)
    ctx = jnp.stack([one * q_scale, zero, one, zero])
    return jnp.stack([lat, ctx])


def _gqa_tables(s):
    ang = _angles(s, HEAD_DIM)
    lane = np.arange(LANES)
    idx = lane % 32
    sign = jnp.asarray(np.where(lane < 64, -1.0, 1.0), F32)
    return _rope_tables(jnp.cos(ang)[:, idx], jnp.sin(ang)[:, idx] * sign, HEAD_DIM ** -0.5 * LOG2E)


def _mla_lane_src():
    lane = np.arange(LANES)
    nope = np.where((lane >= 16) & (lane < 64), lane - 16,
                    np.where((lane >= 80) & (lane < 96), 48 + lane - 80, -1))
    rope = np.where(lane < 16, lane, np.where((lane >= 64) & (lane < 80), 16 + lane - 64, -1))
    return nope, rope


def _mla_tables(s):
    ang = _angles(s, MLA_QK_ROPE)
    lane = np.arange(LANES)
    is_x1 = lane < 16
    is_x2 = (lane >= 64) & (lane < 80)
    idx = np.where(is_x1, lane, np.where(is_x2, lane - 64, 0))
    rot = jnp.asarray(is_x1 | is_x2)
    sign = jnp.asarray(np.where(is_x1, -1.0, np.where(is_x2, 1.0, 0.0)), F32)
    cos_l = jnp.where(rot, jnp.cos(ang)[:, idx], 1.0)
    sin_l = jnp.sin(ang)[:, idx] * sign
    return _rope_tables(cos_l, sin_l, (MLA_QK_NOPE + MLA_QK_ROPE) ** -0.5 * LOG2E)


def _rope(blk, cos, sin):
    return blk * cos + pltpu.roll(blk, 64, 1) * sin


def _gqa_proj_body(*refs, nqb, nkb, qk_norm):
    if qk_norm:
        x_ref, mod_ref, g_ref, w_ref, tab_ref, gq_ref, gk_ref, ind_ref, q_ref, k_ref, v_ref = refs
    else:
        x_ref, mod_ref, g_ref, w_ref, tab_ref, q_ref, k_ref, v_ref = refs
    h = _norm_mod(x_ref[0], g_ref[...], mod_ref[0, 0:1, :], mod_ref[0, 1:2, :]).astype(BF16)
    qkv = _dot(h, w_ref[...])
    cq, sq, ck, sk = tab_ref[0, 0], tab_ref[0, 1], tab_ref[0, 2], tab_ref[0, 3]

    def head_norm(blk, gain):
        ssq = _dot((blk * blk).astype(BF16), ind_ref[...])
        return blk * lax.rsqrt(ssq * (1.0 / HEAD_DIM) + NORM_EPS) * gain

    for c in range(nqb):
        blk = qkv[:, c * LANES:(c + 1) * LANES]
        if qk_norm:
            blk = head_norm(blk, gq_ref[...])
        q_ref[0, :, c * LANES:(c + 1) * LANES] = _rope(blk, cq, sq).astype(BF16)
    for c in range(nkb):
        blk = qkv[:, (nqb + c) * LANES:(nqb + c + 1) * LANES]
        if qk_norm:
            blk = head_norm(blk, gk_ref[...])
        k_ref[0, :, c * LANES:(c + 1) * LANES] = _rope(blk, ck, sk).astype(BF16)
    v_ref[0] = qkv[:, (nqb + nkb) * LANES:].astype(BF16)


def _gqa_proj(xs, mods_i, norm_g, w_perm, tables, n_kv, qk_gains=None):
    nb, s, d = xs.shape
    tm = min(ROW_TILE, s)
    nqb = N_HEADS * HEAD_DIM // LANES
    nkb = n_kv * HEAD_DIM // LANES
    wn = w_perm.shape[1]
    in_specs = [
        pl.BlockSpec((1, tm, d), lambda b, i: (b, i, 0)),
        pl.BlockSpec((1, 6, d), lambda b, i: (b, 0, 0)),
        pl.BlockSpec((1, d), lambda b, i: (0, 0)),
        pl.BlockSpec((d, wn), lambda b, i: (0, 0)),
        pl.BlockSpec((1, 4, tm, LANES), lambda b, i: (b // (nb - 1), 0, i, 0)),
    ]
    args = [xs, mods_i, norm_g.reshape(1, d), w_perm, tables]
    if qk_gains is not None:
        is_b, dim = _pair_lanes()
        ind = jnp.asarray(is_b[:, None] == is_b[None, :], BF16)
        in_specs += [pl.BlockSpec((1, LANES), lambda b, i: (0, 0)),
                     pl.BlockSpec((1, LANES), lambda b, i: (0, 0)),
                     pl.BlockSpec((LANES, LANES), lambda b, i: (0, 0))]
        args += [qk_gains[0][dim].reshape(1, LANES), qk_gains[1][dim].reshape(1, LANES), ind]
    out_w = (nqb * LANES, nkb * LANES, nkb * LANES)
    return pl.pallas_call(
        functools.partial(_gqa_proj_body, nqb=nqb, nkb=nkb, qk_norm=qk_gains is not None),
        grid=(nb, s // tm),
        in_specs=in_specs,
        out_specs=[pl.BlockSpec((1, tm, w), lambda b, i: (b, i, 0)) for w in out_w],
        out_shape=[jax.ShapeDtypeStruct((nb, s, w), BF16) for w in out_w],
        compiler_params=_cparams("arbitrary", "arbitrary"),
        name="gqa_proj",
    )(*args)


def _mla_proj_body(x_ref, mod_ref, g_ref, wa_ref, gq_ref, gkv_ref, wuq_ref, wukv_ref, tab_ref,
                   q_ref, k_ref, v_ref):
    h = _norm_mod(x_ref[0], g_ref[...], mod_ref[0, 0:1, :], mod_ref[0, 1:2, :]).astype(BF16)
    a = _dot(h, wa_ref[...])
    qn = _rms(a[:, :MLA_Q_LORA], gq_ref[...]).astype(BF16)
    cn = _rms(a[:, MLA_Q_LORA:MLA_Q_LORA + MLA_KV_LORA], gkv_ref[...]).astype(BF16)
    kr = a[:, MLA_Q_LORA + MLA_KV_LORA:]
    cq, sq, ck, sk = tab_ref[0, 0], tab_ref[0, 1], tab_ref[0, 2], tab_ref[0, 3]
    q = _dot(qn, wuq_ref[...])
    kv = _dot(cn, wukv_ref[...])
    kr = _rope(kr, ck, sk)
    for hh in range(N_HEADS):
        sl = slice(hh * LANES, (hh + 1) * LANES)
        q_ref[0, :, sl] = _rope(q[:, sl], cq, sq).astype(BF16)
        k_ref[0, :, sl] = (kv[:, sl] + kr).astype(BF16)
    v_ref[0] = kv[:, N_HEADS * LANES:].astype(BF16)


def _mla_proj(xs, mods_i, norm_g, w_a, gq, gkv, w_uq, w_ukv, tables):
    nb, s, d = xs.shape
    tm = min(ROW_TILE, s)
    full = lambda arr: pl.BlockSpec(arr.shape, lambda b, i: (0,) * arr.ndim)
    gq = gq.reshape(1, -1)
    gkv = gkv.reshape(1, -1)
    g = norm_g.reshape(1, d)
    out_w = (N_HEADS * LANES, N_HEADS * LANES, N_HEADS * MLA_V_DIM)
    return pl.pallas_call(
        _mla_proj_body,
        grid=(nb, s // tm),
        in_specs=[
            pl.BlockSpec((1, tm, d), lambda b, i: (b, i, 0)),
            pl.BlockSpec((1, 6, d), lambda b, i: (b, 0, 0)),
            full(g), full(w_a), full(gq), full(gkv), full(w_uq), full(w_ukv),
            pl.BlockSpec((1, 4, tm, LANES), lambda b, i: (b // (nb - 1), 0, i, 0)),
        ],
        out_specs=[pl.BlockSpec((1, tm, w), lambda b, i: (b, i, 0)) for w in out_w],
        out_shape=[jax.ShapeDtypeStruct((nb, s, w), BF16) for w in out_w],
        compiler_params=_cparams("arbitrary", "arbitrary"),
        name="mla_proj",
    )(xs, mods_i, g, w_a, gq, gkv, w_uq, w_ukv, tables)


def _attn_body(*refs, pps, group, ql, klw, split, mode, tq, band, seq, use_sink):
    refs = list(refs)
    sink_ref = refs.pop(0) if use_sink else None
    q_ref = refs.pop(0)
    if mode != "ctx":
        kl_ref, vl_ref = refs.pop(0), refs.pop(0)
    kc_ref, vc_ref, o_ref = refs[:3]
    vt_ref = refs[3] if mode == "dense" else None
    j = pl.program_id(1)
    qi = pl.program_id(2)
    lane = lax.broadcasted_iota(jnp.int32, (1, LANES), 1)
    in_a = (lane & 63) < 32
    row = lax.broadcasted_iota(jnp.int32, (LANES, 1), 0)
    top = row < 64

    def values_t(v):
        vt = v.astype(F32).T
        return (jnp.where(top, vt, jnp.where(row == 64, 1.0, 0.0)).astype(BF16),
                jnp.where(top, jnp.where(row == 0, 1.0, 0.0), vt).astype(BF16))

    ctx_len = kc_ref.shape[1]
    valid = None
    if mode == "dense":
        @pl.when(qi == 0)
        def _():
            for pp in range(pps):
                ps = slice(pp * LANES, (pp + 1) * LANES)
                for half, vt in enumerate(values_t(vc_ref[0, :, ps])):
                    vt_ref[pp, half, :, 0:ctx_len] = vt
                for half, vt in enumerate(values_t(vl_ref[0, :, ps])):
                    vt_ref[pp, half, :, ctx_len:ctx_len + seq] = vt
        n_lat = seq
    elif mode == "window":
        start = jnp.clip(qi * tq - SWA_WINDOW, 0, seq - band)
        start = pl.multiple_of(start, LANES)
        kpos = start + lax.broadcasted_iota(jnp.int32, (band, 1), 0)
        qpos = qi * tq + lax.broadcasted_iota(jnp.int32, (1, tq), 1)
        valid = jnp.abs(qpos - kpos) <= SWA_WINDOW
        n_lat = band
    else:
        n_lat = 0

    chunks = []
    for pp in range(pps):
        ks = slice(pp * klw, (pp + 1) * klw)
        ps = slice(pp * LANES, (pp + 1) * LANES)
        mine = []
        if mode == "dense":
            mine.append((lambda ks=ks: kc_ref[0, :, ks],
                         lambda half, pp=pp: vt_ref[pp, half, :, 0:ctx_len], None))
            for c0 in range(0, n_lat, KEY_CHUNK):
                c1 = min(c0 + KEY_CHUNK, n_lat)
                mine.append((lambda ks=ks, c0=c0, c1=c1: kl_ref[0, c0:c1, ks],
                             lambda half, pp=pp, c0=c0, c1=c1: vt_ref[pp, half, :, ctx_len + c0:ctx_len + c1],
                             None))
        else:
            vc = values_t(vc_ref[0, :, ps])
            mine.append((lambda ks=ks: kc_ref[0, :, ks], lambda half, vc=vc: vc[half], None))
            if mode == "window":
                vl = values_t(vl_ref[0, pl.ds(start, band), ps])
                for c0 in range(0, n_lat, KEY_CHUNK):
                    c1 = min(c0 + KEY_CHUNK, n_lat)
                    mine.append((lambda ks=ks, c0=c0, c1=c1: kl_ref[0, pl.ds(start + c0, c1 - c0), ks],
                                 lambda half, vl=vl, c0=c0, c1=c1: vl[half][:, c0:c1],
                                 valid[c0:c1]))
        chunks.append(mine)
    n_chunks = len(chunks[0])

    def head_q(pp, g, half):
        q = q_ref[0, :, (pp * group + g) * ql:(pp * group + g + 1) * ql]
        if split:
            return q[:, half * LANES:(half + 1) * LANES]
        return jnp.where(in_a if half == 0 else jnp.logical_not(in_a), q, jnp.zeros_like(q))

    def scores(item):
        pp, g, half, ci = item
        load_keys, _, ok = chunks[pp][ci]
        kch = load_keys()
        if split:
            kch = kch[:, half * LANES:(half + 1) * LANES]
        s = _dot_t(kch, head_q(pp, g, half))
        return s if ok is None else jnp.where(ok, s, NEG_INF)

    items = [(pp, g, half, ci) for pp in range(pps) for g in range(group) for half in range(2)
             for ci in range(n_chunks)]
    last = n_chunks - 1
    outs = []
    acc = None

    def values_step(pend, acc):
        (pp, g, half, ci), p, alpha, m_end, sink = pend
        pv = _dot(chunks[pp][ci][1](half), p)
        acc = pv if ci == 0 else acc * alpha + pv
        if ci == last:
            den = acc[64:65, :] if half == 0 else acc[0:1, :]
            if sink is not None:
                den = den + jnp.exp2(sink - m_end)
            outs.append(acc * (1.0 / den))
            if half == 1:
                o_t = jnp.where(top, outs[0], outs[1])
                o_ref[0, :, (pp * group + g) * LANES:(pp * group + g + 1) * LANES] = o_t.T.astype(BF16)
                outs.clear()
        return acc

    s_next = scores(items[0])
    pending = None
    for idx, item in enumerate(items):
        pp, g, half, ci = item
        s = s_next
        if idx + 1 < len(items):
            s_next = scores(items[idx + 1])
        cm = jnp.max(s, axis=0, keepdims=True)
        if ci == 0:
            sink = sink_ref[(2 * (j * pps + pp) + half) * group + g] * LOG2E if use_sink else None
            m = cm if sink is None else jnp.maximum(cm, sink)
            alpha = None
        else:
            m_new = jnp.maximum(m, cm)
            alpha = jnp.exp2(m - m_new)
            m = m_new
        p = jnp.exp2(s - m).astype(BF16)
        if pending is not None:
            acc = values_step(pending, acc)
        pending = (item, p, alpha, m, sink)
    values_step(pending, acc)


def _attention(q, k, v, n_pairs, group, split, window, sinks, need_ctx):
    nb, s, _ = q.shape
    b_lat = nb - 1
    ctx_len = s // b_lat
    ql = q.shape[2] // (n_pairs * group)
    klw = k.shape[2] // n_pairs
    ow = n_pairs * group * LANES
    tq = min(Q_TILE, s)
    band = min(tq + 2 * SWA_WINDOW, s)
    use_sink = sinks is not None
    pps = max(1, HEADS_PER_STEP // (2 * group))
    mode = "window" if window else "dense"
    common = dict(pps=pps, group=group, ql=ql, klw=klw, split=split, tq=tq, band=band, seq=s,
                  use_sink=use_sink)
    smem = [pl.BlockSpec(memory_space=pltpu.SMEM)] if use_sink else []
    sink_args = [sinks] if use_sink else []
    qw, kw, vw, o_w = pps * group * ql, pps * klw, pps * LANES, pps * group * LANES

    lat_specs = smem + [
        pl.BlockSpec((1, tq, qw), lambda b, j, i: (b, i, j)),
        pl.BlockSpec((1, s, kw), lambda b, j, i: (b, 0, j)),
        pl.BlockSpec((1, s, vw), lambda b, j, i: (b, 0, j)),
        pl.BlockSpec((1, ctx_len, kw), lambda b, j, i: (b_lat, b, j)),
        pl.BlockSpec((1, ctx_len, vw), lambda b, j, i: (b_lat, b, j)),
    ]
    scratch = [pltpu.VMEM((pps, 2, LANES, ctx_len + s), BF16)] if mode == "dense" else []
    o_lat = pl.pallas_call(
        functools.partial(_attn_body, mode=mode, **common),
        grid=(b_lat, n_pairs // pps, s // tq),
        in_specs=lat_specs,
        out_specs=pl.BlockSpec((1, tq, o_w), lambda b, j, i: (b, i, j)),
        out_shape=jax.ShapeDtypeStruct((b_lat, s, ow), BF16),
        scratch_shapes=scratch,
        compiler_params=_cparams("arbitrary", "arbitrary", "arbitrary"),
        name="attn_latent",
    )(*sink_args, q, k, v, k, v)
    if not need_ctx:
        return o_lat, None

    common["tq"] = ctx_len
    ctx_specs = smem + [
        pl.BlockSpec((1, ctx_len, qw), lambda b, j, i: (b_lat, b, j)),
        pl.BlockSpec((1, ctx_len, kw), lambda b, j, i: (b_lat, b, j)),
        pl.BlockSpec((1, ctx_len, vw), lambda b, j, i: (b_lat, b, j)),
    ]
    o_ctx = pl.pallas_call(
        functools.partial(_attn_body, mode="ctx", **common),
        grid=(b_lat, n_pairs // pps, 1),
        in_specs=ctx_specs,
        out_specs=pl.BlockSpec((1, ctx_len, o_w), lambda b, j, i: (0, b, j)),
        out_shape=jax.ShapeDtypeStruct((1, s, ow), BF16),
        compiler_params=_cparams("arbitrary", "arbitrary", "arbitrary"),
        name="attn_context",
    )(*sink_args, q, k, v)
    return o_lat, o_ctx


def _post_body(*refs, moe, b_lat, has_ctx):
    refs = list(refs)
    o_ref = refs.pop(0)
    oc_ref = refs.pop(0) if has_ctx else None
    if moe:
        x_ref, mod_ref, g_ref, wo_ref, rw_ref, rb_ref, xo_ref, h_ref, route_ref = refs
    else:
        x_ref, mod_ref, g_ref, wo_ref, wgu_ref, wd_ref, xo_ref = refs
    o = o_ref[0]
    if has_ctx:
        o = jnp.where(pl.program_id(0) == b_lat, oc_ref[0], o)
    y = _dot(o, wo_ref[...])
    x = x_ref[0] + mod_ref[0, 2:3, :] * y
    h = _norm_mod(x, g_ref[...], mod_ref[0, 3:4, :], mod_ref[0, 4:5, :])
    h_hi = h.astype(BF16)
    if not moe:
        acc = None
        for c in range(D_FF // FF_CHUNK):
            gate = _dot(h_hi, wgu_ref[:, c * FF_CHUNK:(c + 1) * FF_CHUNK])
            up = _dot(h_hi, wgu_ref[:, D_FF + c * FF_CHUNK:D_FF + (c + 1) * FF_CHUNK])
            act = (_silu(gate) * up).astype(BF16)
            part = _dot(act, wd_ref[c * FF_CHUNK:(c + 1) * FF_CHUNK, :])
            acc = part if acc is None else acc + part
        xo_ref[0] = x + mod_ref[0, 5:6, :] * acc
        return
    xo_ref[0] = x
    h_ref[0] = h
    h_lo = (h - h_hi.astype(F32)).astype(BF16)
    both = _dot(h_hi, rw_ref[...])
    logits = both[:, :LANES] + both[:, LANES:] + _dot(h_lo, rw_ref[:, :LANES]) + rb_ref[...]
    lane = lax.broadcasted_iota(jnp.int32, logits.shape, 1).astype(F32)
    lg = jnp.where(lane < N_EXPERTS, logits, -jnp.inf)
    v1 = jnp.max(lg, axis=-1, keepdims=True)
    i1 = jnp.min(jnp.where(lg == v1, lane, float(LANES)), axis=-1, keepdims=True)
    lg2 = jnp.where(lane == i1, -jnp.inf, lg)
    v2 = jnp.max(lg2, axis=-1, keepdims=True)
    i2 = jnp.min(jnp.where(lg2 == v2, lane, float(LANES)), axis=-1, keepdims=True)
    e = jnp.exp(v2 - v1)
    g1 = 1.0 / (1.0 + e)
    g2 = e / (1.0 + e)
    route_ref[0] = jnp.where(lane == 0, i1, jnp.where(lane == 1, i2, jnp.where(
        lane == 2, g1, jnp.where(lane == 3, g2, 0.0))))


def _post(o_lat, o_ctx, xs, mods_i, norm_g, w_o, nb, router=None, ffn=None):
    _, s, d = xs.shape
    tm = min(ROW_TILE, s)
    moe = router is not None
    assert moe != (ffn is not None)
    has_ctx = o_ctx is not None
    resident = lambda arr: pl.BlockSpec(arr.shape, lambda b, i: (0, 0), pipeline_mode=pl.Buffered(1))
    b_lat = o_lat.shape[0]
    in_specs = [pl.BlockSpec((1, tm, d), lambda b, i: (jnp.minimum(b, b_lat - 1), i, 0))]
    args = [o_lat]
    if has_ctx:
        in_specs.append(pl.BlockSpec((1, tm, d), lambda b, i: (0, i, 0)))
        args.append(o_ctx)
    n_o = len(args)
    in_specs += [
        pl.BlockSpec((1, tm, d), lambda b, i: (b, i, 0)),
        pl.BlockSpec((1, 6, d), lambda b, i: (b, 0, 0)),
        pl.BlockSpec((1, d), lambda b, i: (0, 0)),
        resident(w_o),
    ]
    args += [xs, mods_i, norm_g.reshape(1, d), w_o]
    out_specs = [pl.BlockSpec((1, tm, d), lambda b, i: (b, i, 0))]
    out_shape = [jax.ShapeDtypeStruct(xs.shape, F32)]
    if moe:
        in_specs += [pl.BlockSpec((d, 2 * LANES), lambda b, i: (0, 0)),
                     pl.BlockSpec((1, LANES), lambda b, i: (0, 0))]
        args += list(router)
        out_specs += [pl.BlockSpec((1, tm, d), lambda b, i: (b, i, 0)),
                      pl.BlockSpec((1, tm, LANES), lambda b, i: (b, i, 0))]
        out_shape += [jax.ShapeDtypeStruct((nb, s, d), F32), jax.ShapeDtypeStruct((nb, s, LANES), F32)]
    else:
        in_specs += [resident(w) for w in ffn]
        args += list(ffn)
    return pl.pallas_call(
        functools.partial(_post_body, moe=moe, b_lat=b_lat, has_ctx=has_ctx),
        grid=(nb, s // tm),
        in_specs=in_specs,
        out_specs=out_specs,
        out_shape=out_shape,
        input_output_aliases={n_o: 0},
        compiler_params=_cparams("arbitrary", "arbitrary"),
        name="attn_out_router" if moe else "attn_out_ffn",
    )(*args)


def _row_dma_wait(src_ref, dst_ref, sem, rows):
    pltpu.make_async_copy(src_ref.at[pl.ds(0, rows)], dst_ref.at[pl.ds(0, rows)], sem).wait()


def _dispatch_body(pos_ref, h_ref, xs_in_ref, xs_ref, sem):
    del xs_in_ref
    tm = h_ref.shape[0]

    def issue(r, carry):
        for k in range(2):
            dst = pos_ref[0, 0, 2 * r + k]
            pltpu.make_async_copy(h_ref.at[pl.ds(r, 1)], xs_ref.at[pl.ds(dst, 1)], sem).start(priority=k)
        return carry

    lax.fori_loop(0, tm, issue, 0, unroll=8)
    _row_dma_wait(h_ref, xs_ref, sem, tm)
    _row_dma_wait(h_ref, xs_ref, sem, tm)


def _moe_dispatch(hp, pos, p, tm):
    t, w = hp.shape
    return pl.pallas_call(
        _dispatch_body,
        grid=(t // tm,),
        in_specs=[
            pl.BlockSpec((1, 1, 2 * tm), lambda i: (i, 0, 0), memory_space=pltpu.SMEM),
            pl.BlockSpec((tm, w), lambda i: (i, 0)),
            pl.BlockSpec(memory_space=pl.ANY),
        ],
        out_specs=pl.BlockSpec(memory_space=pl.ANY),
        out_shape=jax.ShapeDtypeStruct((p, w), hp.dtype),
        scratch_shapes=[pltpu.SemaphoreType.DMA(())],
        input_output_aliases={2: 0},
        compiler_params=_cparams("arbitrary"),
        name="moe_dispatch",
    )(pos.reshape(t // tm, 1, 2 * tm), hp, jnp.zeros((p, w), hp.dtype))


def _moe_body(te_ref, nu_ref, xs_ref, wg_ref, wu_ref, wd_ref, ys_ref):
    i = pl.program_id(0)

    @pl.when(i < nu_ref[0])
    def _():
        x = xs_ref[...].astype(BF16)
        act = (_silu(_dot(x, wg_ref[0])) * _dot(x, wu_ref[0])).astype(BF16)
        ys_ref[...] = _dot(act, wd_ref[0])

    @pl.when(i >= nu_ref[0])
    def _():
        ys_ref[...] = jnp.zeros_like(ys_ref)


def _moe_experts(xs_sorted, tile_expert, n_used, w_gu, w_d, tm):
    p = xs_sorted.shape[0]
    f, d = w_d.shape[1:]
    grid_spec = pltpu.PrefetchScalarGridSpec(
        num_scalar_prefetch=2,
        grid=(p // tm,),
        in_specs=[
            pl.BlockSpec((tm, d), lambda i, te, nu: (i, 0)),
            pl.BlockSpec((1, d, f), lambda i, te, nu: (te[i], 0, 0)),
            pl.BlockSpec((1, d, f), lambda i, te, nu: (te[i], 0, 1)),
            pl.BlockSpec((1, f, d), lambda i, te, nu: (te[i], 0, 0)),
        ],
        out_specs=pl.BlockSpec((tm, d), lambda i, te, nu: (i, 0)),
    )
    return pl.pallas_call(
        _moe_body,
        grid_spec=grid_spec,
        out_shape=jax.ShapeDtypeStruct((p, d), F32),
        compiler_params=_cparams("arbitrary"),
        name="moe_experts",
    )(tile_expert, n_used, xs_sorted, w_gu, w_gu, w_d)


def _combine_body(*refs, final):
    if final:
        pos_ref, x_ref, route_ref, mod_ref, g_ref, ys_ref, o_ref, ybuf, sem = refs
    else:
        pos_ref, x_ref, route_ref, mod_ref, ys_ref, o_ref, ybuf, sem = refs
    tm = x_ref.shape[1]

    def issue(r, carry):
        for k in range(2):
            src = pos_ref[0, 0, 2 * r + k]
            pltpu.make_async_copy(ys_ref.at[pl.ds(src, 1)], ybuf.at[pl.ds(k * tm + r, 1)], sem).start(priority=k)
        return carry

    lax.fori_loop(0, tm, issue, 0, unroll=8)
    _row_dma_wait(ys_ref, ybuf, sem, 2 * tm)
    route = route_ref[0]
    y = route[:, 2:3] * ybuf[0:tm, :] + route[:, 3:4] * ybuf[tm:2 * tm, :]
    x = x_ref[0] + mod_ref[0, 5:6, :] * y
    o_ref[0] = _rms(x, g_ref[...]) if final else x


def _moe_combine(xs, ys, pos, route, mods_i, nb, final_g=None):
    _, s, d = xs.shape
    tm = min(ROW_TILE, s)
    per_b = s // tm
    final = final_g is not None
    tok = pl.BlockSpec((1, tm, d), lambda b, i: (b, i, 0))
    in_specs = [pl.BlockSpec((1, 1, 2 * tm), lambda b, i: (b * per_b + i, 0, 0), memory_space=pltpu.SMEM),
                tok,
                pl.BlockSpec((1, tm, LANES), lambda b, i: (b, i, 0)),
                pl.BlockSpec((1, 6, d), lambda b, i: (b, 0, 0))]
    args = [pos.reshape(nb * per_b, 1, 2 * tm), xs, route, mods_i]
    if final:
        in_specs.append(pl.BlockSpec((1, d), lambda b, i: (0, 0)))
        args.append(final_g.reshape(1, d))
    in_specs.append(pl.BlockSpec(memory_space=pl.ANY))
    args.append(ys)
    return pl.pallas_call(
        functools.partial(_combine_body, final=final),
        grid=(nb, per_b),
        in_specs=in_specs,
        out_specs=tok,
        out_shape=jax.ShapeDtypeStruct((nb, s, d) if final else xs.shape, F32),
        scratch_shapes=[pltpu.VMEM((2 * tm, d), F32), pltpu.SemaphoreType.DMA(())],
        input_output_aliases={} if final else {1: 0},
        compiler_params=_cparams("arbitrary", "arbitrary"),
        name="moe_combine",
    )(*args)


def _moe_layer(hp, route, xs, mods_i, w_gu, w_d, nb, final_g):
    _, s, d = xs.shape
    t = nb * s
    tm = min(MOE_TILE, s)
    p = 2 * t + N_EXPERTS * tm
    expert = route.reshape(t, LANES)[:, :2].astype(jnp.int32).reshape(2 * t)
    onehot = (expert[:, None] == jnp.arange(N_EXPERTS, dtype=jnp.int32)[None, :]).astype(jnp.int32)
    csum = jnp.cumsum(onehot, axis=0)
    padded = ((csum[-1] + tm - 1) // tm) * tm
    off_end = jnp.cumsum(padded)
    pos = jnp.sum(onehot * ((off_end - padded)[None, :] + csum - 1), axis=1)
    tile_start = jnp.arange(p // tm, dtype=jnp.int32) * tm
    tile_expert = jnp.minimum(jnp.sum(tile_start[:, None] >= off_end[None, :], axis=1), N_EXPERTS - 1)
    n_used = (off_end[-1] // tm).reshape(1).astype(jnp.int32)

    xs_sorted = _moe_dispatch(hp.reshape(t, d), pos, p, tm)
    ys = _moe_experts(xs_sorted, tile_expert.astype(jnp.int32), n_used, w_gu, w_d, tm)
    return _moe_combine(xs, ys, pos, route, mods_i, nb, final_g)


def kernel(x, c, ctx, c_ctx, ada_w, ada_b, norm_g, final_norm_g, mla_w_dq, mla_q_norm_g, mla_w_uq, mla_w_dkv, mla_kv_norm_g, mla_w_ukv, mla_w_o, swa_w_qkv, swa_sinks, swa_w_o, ga_w_qkv, ga_q_norm_g, ga_k_norm_g, ga_w_o, ffn_w_gate_up, ffn_w_down, moe_router_w, moe_router_b, moe_w_gate_up, moe_w_down):
    b, s, d = x.shape
    assert d == D_MODEL and b * ctx.shape[1] == s and s % GRID_W == 0
    nb = b + 1
    xs = jnp.concatenate([x, ctx.reshape(1, s, d)], axis=0)
    c_all = jnp.concatenate([c, c_ctx[None], jnp.zeros((16 - nb, d), F32)], axis=0)
    mods = _mods(c_all, ada_w, ada_b).reshape(DEPTH, 16, 6, d)

    gqa_tab = _gqa_tables(s)
    mla_tab = _mla_tables(s)
    nope_src, rope_src = _mla_lane_src()

    for i in range(DEPTH):
        need_ctx = i < DEPTH - 1
        mods_i = mods[i]
        kind, j = i % 3, i // 3
        if kind == 0:
            dkv = jnp.concatenate([mla_w_dkv[j], jnp.zeros((d, 1), F32)], axis=1)
            kr_cols = np.where(rope_src >= 0, MLA_KV_LORA + rope_src, MLA_KV_LORA + MLA_QK_ROPE)
            w_a = jnp.concatenate([mla_w_dq[j], dkv[:, :MLA_KV_LORA], dkv[:, kr_cols]], axis=1).astype(BF16)
            qd = MLA_QK_NOPE + MLA_QK_ROPE
            q_src = np.where(nope_src >= 0, nope_src, np.where(rope_src >= 0, MLA_QK_NOPE + rope_src, qd))
            uq = jnp.concatenate([mla_w_uq[j].reshape(MLA_Q_LORA, N_HEADS, qd),
                                  jnp.zeros((MLA_Q_LORA, N_HEADS, 1), F32)], axis=2)
            w_uq = uq[:, :, q_src].reshape(MLA_Q_LORA, N_HEADS * LANES).astype(BF16)
            kvd = MLA_QK_NOPE + MLA_V_DIM
            ukv = jnp.concatenate([mla_w_ukv[j].reshape(MLA_KV_LORA, N_HEADS, kvd),
                                   jnp.zeros((MLA_KV_LORA, N_HEADS, 1), F32)], axis=2)
            k_src = np.where(nope_src >= 0, nope_src, kvd)
            w_ukv = jnp.concatenate([ukv[:, :, k_src].reshape(MLA_KV_LORA, N_HEADS * LANES),
                                     ukv[:, :, MLA_QK_NOPE:kvd].reshape(MLA_KV_LORA, N_HEADS * MLA_V_DIM)],
                                    axis=1).astype(BF16)
            q, k, v = _mla_proj(xs, mods_i, norm_g[i, 0], w_a, mla_q_norm_g[j], mla_kv_norm_g[j],
                                w_uq, w_ukv, mla_tab)
            o = _attention(q, k, v, N_HEADS // 2, 1, True, False, None, need_ctx)
            w_o = mla_w_o[j].astype(BF16)
        elif kind == 1:
            cols, o_rows = _gqa_perm(SWA_KV_HEADS, N_HEADS // SWA_KV_HEADS)
            q, k, v = _gqa_proj(xs, mods_i, norm_g[i, 0], swa_w_qkv[j][:, cols].astype(BF16), gqa_tab,
                                SWA_KV_HEADS)
            o = _attention(q, k, v, SWA_KV_HEADS // 2, N_HEADS // SWA_KV_HEADS, False, True,
                           swa_sinks[j], need_ctx)
            w_o = swa_w_o[j][o_rows].astype(BF16)
        else:
            cols, o_rows = _gqa_perm(GA_KV_HEADS, N_HEADS // GA_KV_HEADS)
            q, k, v = _gqa_proj(xs, mods_i, norm_g[i, 0], ga_w_qkv[j][:, cols].astype(BF16), gqa_tab,
                                GA_KV_HEADS, (ga_q_norm_g[j], ga_k_norm_g[j]))
            o = _attention(q, k, v, GA_KV_HEADS // 2, N_HEADS // GA_KV_HEADS, False, False, None, need_ctx)
            w_o = ga_w_o[j][o_rows].astype(BF16)

        n_tok_b = nb if need_ctx else b
        f = i // 2
        if i % 2 == 0:
            xs, = _post(*o, xs, mods_i, norm_g[i, 1], w_o, n_tok_b,
                        ffn=(ffn_w_gate_up[f].astype(BF16), ffn_w_down[f].astype(BF16)))
        else:
            rw = jnp.concatenate([moe_router_w[f], jnp.zeros((d, LANES - N_EXPERTS), F32)], axis=1)
            rw_hi = rw.astype(BF16)
            rw_lo = (rw - rw_hi.astype(F32)).astype(BF16)
            rb = jnp.concatenate([moe_router_b[f], jnp.zeros((LANES - N_EXPERTS,), F32)]).reshape(1, LANES)
            xs, h2, comb = _post(*o, xs, mods_i, norm_g[i, 1], w_o, n_tok_b,
                                 (jnp.concatenate([rw_hi, rw_lo], axis=1), rb))
            xs = _moe_layer(h2, comb, xs, mods_i, moe_w_gate_up[f].astype(BF16), moe_w_down[f].astype(BF16),
                            n_tok_b, None if need_ctx else final_norm_g)
    return xs
```

```python
import functools
import math

import numpy as np
import jax
import jax.numpy as jnp
from jax import lax
from jax.experimental import pallas as pl
from jax.experimental.pallas import tpu as pltpu

F32 = jnp.float32
BF16 = jnp.bfloat16

D_MODEL = 1024
GRID_W = 64
HEAD_DIM = 64
N_HEADS = 16
MLA_Q_LORA = 384
MLA_KV_LORA = 256
MLA_QK_NOPE = 64
MLA_QK_ROPE = 32
MLA_V_DIM = 64
SWA_KV_HEADS = 4
SWA_WINDOW = 128
GA_KV_HEADS = 8
D_FF = 2816
N_EXPERTS = 8
EXPERT_FF = 1408
ROPE_THETA = 10000.0
NORM_EPS = 1e-6
NEG_INF = -1e30
DEPTH = 4
LOG2E = math.log2(math.e)

LANES = 128
VMEM_LIMIT = 56 * 2**20
ROW_TILE = 512
Q_TILE = 512
KEY_CHUNK = 512
HEADS_PER_STEP = 8
MOE_TILE = 512
FF_CHUNK = 1408


def _cparams(*sem):
    return pltpu.CompilerParams(dimension_semantics=sem, vmem_limit_bytes=VMEM_LIMIT)


def _silu(x):
    return x * (1.0 / (1.0 + jnp.exp(-x)))


def _rms(x, g):
    ms = jnp.mean(x * x, axis=-1, keepdims=True)
    return x * lax.rsqrt(ms + NORM_EPS) * g


def _norm_mod(x, g, shift, scale):
    return _rms(x, g) * (1.0 + scale) + shift


def _dot(a, b):
    return jnp.dot(a, b, preferred_element_type=F32)


def _dot_t(a, b):
    return lax.dot_general(a, b, (((1,), (1,)), ((), ())), preferred_element_type=F32)


def _mods_body(c_ref, w_ref, b_ref, o_ref):
    sc = _silu(c_ref[...]).astype(BF16)
    o_ref[0] = _dot(sc, w_ref[0].astype(BF16)) + b_ref[0]


def _mods(c_all, ada_w, ada_b):
    depth, d, n = ada_w.shape
    rows = c_all.shape[0]
    tn = 1536
    return pl.pallas_call(
        _mods_body,
        grid=(depth, n // tn),
        in_specs=[
            pl.BlockSpec((rows, d), lambda i, j: (0, 0)),
            pl.BlockSpec((1, d, tn), lambda i, j: (i, 0, j)),
            pl.BlockSpec((1, 1, tn), lambda i, j: (i, 0, j)),
        ],
        out_specs=pl.BlockSpec((1, rows, tn), lambda i, j: (i, 0, j)),
        out_shape=jax.ShapeDtypeStruct((depth, rows, n), F32),
        compiler_params=_cparams("arbitrary", "arbitrary"),
        name="adaln_mods",
    )(c_all, ada_w, ada_b.reshape(depth, 1, n))


def _pair_lanes():
    lane = np.arange(LANES)
    is_b = (lane % 64) >= 32
    dim = (lane % 32) + 32 * (lane // 64)
    return is_b, dim


def _gqa_perm(n_kv, group):
    is_b, dim = _pair_lanes()
    nq = N_HEADS * HEAD_DIM
    nk = n_kv * HEAD_DIM
    q_cols, k_cols, o_rows = [], [], []
    nat = np.arange(HEAD_DIM)
    for j in range(n_kv // 2):
        for g in range(group):
            head_a, head_b = (2 * j) * group + g, (2 * j + 1) * group + g
            q_cols.append(np.where(is_b, head_b, head_a) * HEAD_DIM + dim)
            o_rows.append(np.concatenate([head_a * HEAD_DIM + nat, head_b * HEAD_DIM + nat]))
        k_cols.append(nq + np.where(is_b, 2 * j + 1, 2 * j) * HEAD_DIM + dim)
    cols = np.concatenate(q_cols + k_cols + [nq + nk + np.arange(nk)])
    return cols.astype(np.int32), np.concatenate(o_rows).astype(np.int32)


def _angles(s, rot_dim):
    n_freq = rot_dim // 4
    inv = ROPE_THETA ** (-jnp.arange(n_freq, dtype=F32) / n_freq)
    pos = jnp.arange(s)
    rows = (pos // GRID_W).astype(F32)
    cols = (pos % GRID_W).astype(F32)
    return jnp.concatenate([rows[:, None] * inv, cols[:, None] * inv], axis=-1)


def _rope_tables(cos_l, sin_l, q_scale):
    lat = jnp.stack([cos_l * q_scale, sin_l * q_scale, cos_l, sin_l])
    one = jnp.ones_like(cos_l)
    zero = jnp.zeros_like(cos_l)
    ctx = jnp.stack([one * q_scale, zero, one, zero])
    return jnp.stack([lat, ctx])


def _gqa_tables(s):
    ang = _angles(s, HEAD_DIM)
    lane = np.arange(LANES)
    idx = lane % 32
    sign = jnp.asarray(np.where(lane < 64, -1.0, 1.0), F32)
    return _rope_tables(jnp.cos(ang)[:, idx], jnp.sin(ang)[:, idx] * sign, HEAD_DIM ** -0.5 * LOG2E)


def _mla_lane_src():
    lane = np.arange(LANES)
    nope = np.where((lane >= 16) & (lane < 64), lane - 16,
                    np.where((lane >= 80) & (lane < 96), 48 + lane - 80, -1))
    rope = np.where(lane < 16, lane, np.where((lane >= 64) & (lane < 80), 16 + lane - 64, -1))
    return nope, rope


def _mla_tables(s):
    ang = _angles(s, MLA_QK_ROPE)
    lane = np.arange(LANES)
    is_x1 = lane < 16
    is_x2 = (lane >= 64) & (lane < 80)
    idx = np.where(is_x1, lane, np.where(is_x2, lane - 64, 0))
    rot = jnp.asarray(is_x1 | is_x2)
    sign = jnp.asarray(np.where(is_x1, -1.0, np.where(is_x2, 1.0, 0.0)), F32)
    cos_l = jnp.where(rot, jnp.cos(ang)[:, idx], 1.0)
    sin_l = jnp.sin(ang)[:, idx] * sign
    return _rope_tables(cos_l, sin_l, (MLA_QK_NOPE + MLA_QK_ROPE) ** -0.5 * LOG2E)


def _rope(blk, cos, sin):
    return blk * cos + pltpu.roll(blk, 64, 1) * sin


VT_ROWS = 80
DEN_ROW = 64


def _ones_rows(n_cols):
    r = lax.broadcasted_iota(jnp.int32, (VT_ROWS - DEN_ROW, n_cols), 0)
    return jnp.where(r == 0, 1.0, 0.0)


def _store_values_t(vt_ref, v, n_heads):
    v_t = v.T
    tail = _ones_rows(v.shape[0]).astype(vt_ref.dtype)
    for h in range(n_heads):
        vt_ref[0, h * VT_ROWS:h * VT_ROWS + DEN_ROW, :] = v_t[h * DEN_ROW:(h + 1) * DEN_ROW, :].astype(vt_ref.dtype)
        vt_ref[0, h * VT_ROWS + DEN_ROW:(h + 1) * VT_ROWS, :] = tail


def _gqa_proj_body(*refs, nqb, nkb, qk_norm, values_t):
    if qk_norm:
        x_ref, mod_ref, g_ref, w_ref, tab_ref, gq_ref, gk_ref, ind_ref, q_ref, k_ref, v_ref = refs
    else:
        x_ref, mod_ref, g_ref, w_ref, tab_ref, q_ref, k_ref, v_ref = refs
    h = _norm_mod(x_ref[0], g_ref[...], mod_ref[0, 0:1, :], mod_ref[0, 1:2, :]).astype(BF16)
    qkv = _dot(h, w_ref[...])
    cq, sq, ck, sk = tab_ref[0, 0], tab_ref[0, 1], tab_ref[0, 2], tab_ref[0, 3]

    def head_norm(blk, gain):
        ssq = _dot((blk * blk).astype(BF16), ind_ref[...])
        return blk * lax.rsqrt(ssq * (1.0 / HEAD_DIM) + NORM_EPS) * gain

    for c in range(nqb):
        blk = qkv[:, c * LANES:(c + 1) * LANES]
        if qk_norm:
            blk = head_norm(blk, gq_ref[...])
        q_ref[0, :, c * LANES:(c + 1) * LANES] = _rope(blk, cq, sq).astype(BF16)
    for c in range(nkb):
        blk = qkv[:, (nqb + c) * LANES:(nqb + c + 1) * LANES]
        if qk_norm:
            blk = head_norm(blk, gk_ref[...])
        k_ref[0, :, c * LANES:(c + 1) * LANES] = _rope(blk, ck, sk).astype(BF16)
    v = qkv[:, (nqb + nkb) * LANES:]
    if values_t:
        _store_values_t(v_ref, v, 2 * nkb)
    else:
        v_ref[0] = v.astype(BF16)


def _gqa_proj(xs, mods_i, norm_g, w_perm, tables, n_kv, qk_gains=None, values_t=True):
    nb, s, d = xs.shape
    tm = min(ROW_TILE, s)
    nqb = N_HEADS * HEAD_DIM // LANES
    nkb = n_kv * HEAD_DIM // LANES
    wn = w_perm.shape[1]
    in_specs = [
        pl.BlockSpec((1, tm, d), lambda b, i: (b, i, 0)),
        pl.BlockSpec((1, 6, d), lambda b, i: (b, 0, 0)),
        pl.BlockSpec((1, d), lambda b, i: (0, 0)),
        pl.BlockSpec((d, wn), lambda b, i: (0, 0)),
        pl.BlockSpec((1, 4, tm, LANES), lambda b, i: (b // (nb - 1), 0, i, 0)),
    ]
    args = [xs, mods_i, norm_g.reshape(1, d), w_perm, tables]
    if qk_gains is not None:
        is_b, dim = _pair_lanes()
        ind = jnp.asarray(is_b[:, None] == is_b[None, :], BF16)
        in_specs += [pl.BlockSpec((1, LANES), lambda b, i: (0, 0)),
                     pl.BlockSpec((1, LANES), lambda b, i: (0, 0)),
                     pl.BlockSpec((LANES, LANES), lambda b, i: (0, 0))]
        args += [qk_gains[0][dim].reshape(1, LANES), qk_gains[1][dim].reshape(1, LANES), ind]
    out_w = (nqb * LANES, nkb * LANES, nkb * LANES)
    out_specs = [pl.BlockSpec((1, tm, w), lambda b, i: (b, i, 0)) for w in out_w]
    out_shape = [jax.ShapeDtypeStruct((nb, s, w), BF16) for w in out_w]
    if values_t:
        out_specs[2] = pl.BlockSpec((1, n_kv * VT_ROWS, tm), lambda b, i: (b, 0, i))
        out_shape[2] = jax.ShapeDtypeStruct((nb, n_kv * VT_ROWS, s), BF16)
    return pl.pallas_call(
        functools.partial(_gqa_proj_body, nqb=nqb, nkb=nkb, qk_norm=qk_gains is not None, values_t=values_t),
        grid=(nb, s // tm),
        in_specs=in_specs,
        out_specs=out_specs,
        out_shape=out_shape,
        compiler_params=_cparams("arbitrary", "arbitrary"),
        name="gqa_proj",
    )(*args)


def _mla_proj_body(x_ref, mod_ref, g_ref, wa_ref, gq_ref, gkv_ref, wuq_ref, wukv_ref, tab_ref,
                   q_ref, k_ref, v_ref):
    h = _norm_mod(x_ref[0], g_ref[...], mod_ref[0, 0:1, :], mod_ref[0, 1:2, :]).astype(BF16)
    a = _dot(h, wa_ref[...])
    qn = _rms(a[:, :MLA_Q_LORA], gq_ref[...]).astype(BF16)
    cn = _rms(a[:, MLA_Q_LORA:MLA_Q_LORA + MLA_KV_LORA], gkv_ref[...]).astype(BF16)
    kr = a[:, MLA_Q_LORA + MLA_KV_LORA:]
    cq, sq, ck, sk = tab_ref[0, 0], tab_ref[0, 1], tab_ref[0, 2], tab_ref[0, 3]
    q = _dot(qn, wuq_ref[...])
    kv = _dot(cn, wukv_ref[...])
    kr = _rope(kr, ck, sk)
    for hh in range(N_HEADS):
        sl = slice(hh * LANES, (hh + 1) * LANES)
        q_ref[0, :, sl] = _rope(q[:, sl], cq, sq).astype(BF16)
        k_ref[0, :, sl] = (kv[:, sl] + kr).astype(BF16)
    _store_values_t(v_ref, kv[:, N_HEADS * LANES:], N_HEADS)


def _mla_proj(xs, mods_i, norm_g, w_a, gq, gkv, w_uq, w_ukv, tables):
    nb, s, d = xs.shape
    tm = min(ROW_TILE, s)
    full = lambda arr: pl.BlockSpec(arr.shape, lambda b, i: (0,) * arr.ndim)
    gq = gq.reshape(1, -1)
    gkv = gkv.reshape(1, -1)
    g = norm_g.reshape(1, d)
    qk_w = N_HEADS * LANES
    vt_rows = N_HEADS * VT_ROWS
    return pl.pallas_call(
        _mla_proj_body,
        grid=(nb, s // tm),
        in_specs=[
            pl.BlockSpec((1, tm, d), lambda b, i: (b, i, 0)),
            pl.BlockSpec((1, 6, d), lambda b, i: (b, 0, 0)),
            full(g), full(w_a), full(gq), full(gkv), full(w_uq), full(w_ukv),
            pl.BlockSpec((1, 4, tm, LANES), lambda b, i: (b // (nb - 1), 0, i, 0)),
        ],
        out_specs=[pl.BlockSpec((1, tm, qk_w), lambda b, i: (b, i, 0)),
                   pl.BlockSpec((1, tm, qk_w), lambda b, i: (b, i, 0)),
                   pl.BlockSpec((1, vt_rows, tm), lambda b, i: (b, 0, i))],
        out_shape=[jax.ShapeDtypeStruct((nb, s, qk_w), BF16), jax.ShapeDtypeStruct((nb, s, qk_w), BF16),
                   jax.ShapeDtypeStruct((nb, vt_rows, s), BF16)],
        compiler_params=_cparams("arbitrary", "arbitrary"),
        name="mla_proj",
    )(xs, mods_i, g, w_a, gq, gkv, w_uq, w_ukv, tables)


def _attn_body(*refs, pps, group, ql, klw, split, mode, tq, band, seq, use_sink, values_t):
    refs = list(refs)
    sink_ref = refs.pop(0) if use_sink else None
    q_ref = refs.pop(0)
    if mode != "ctx":
        kl_ref, vl_ref = refs.pop(0), refs.pop(0)
    kc_ref, vc_ref, o_ref = refs
    j = pl.program_id(1)
    qi = pl.program_id(2)
    lane = lax.broadcasted_iota(jnp.int32, (1, LANES), 1)
    in_a = (lane & 63) < 32

    def transpose_values(v):
        v_t = v.astype(F32).T
        tail = _ones_rows(v.shape[0])
        return [jnp.concatenate([v_t[h * DEN_ROW:(h + 1) * DEN_ROW], tail], axis=0).astype(BF16)
                for h in range(2)]

    ctx_len = kc_ref.shape[1]
    valid = None
    if mode == "dense":
        n_lat = seq
    elif mode == "window":
        start = jnp.clip(qi * tq - SWA_WINDOW, 0, seq - band)
        start = pl.multiple_of(start, LANES)
        kpos = start + lax.broadcasted_iota(jnp.int32, (band, 1), 0)
        qpos = qi * tq + lax.broadcasted_iota(jnp.int32, (1, tq), 1)
        valid = jnp.abs(qpos - kpos) <= SWA_WINDOW
        n_lat = band
    else:
        n_lat = 0

    chunks = []
    for pp in range(pps):
        ks = slice(pp * klw, (pp + 1) * klw)
        mine = []
        if values_t:
            head_rows = lambda half, pp=pp: slice((2 * pp + half) * VT_ROWS, (2 * pp + half + 1) * VT_ROWS)
            mine.append((lambda ks=ks: kc_ref[0, :, ks], lambda half, hr=head_rows: vc_ref[0, hr(half), :], None))
            for c0 in range(0, n_lat, KEY_CHUNK):
                c1 = min(c0 + KEY_CHUNK, n_lat)
                mine.append((lambda ks=ks, c0=c0, c1=c1: kl_ref[0, c0:c1, ks],
                             lambda half, hr=head_rows, c0=c0, c1=c1: vl_ref[0, hr(half), c0:c1], None))
        else:
            ps = slice(pp * LANES, (pp + 1) * LANES)
            vc = transpose_values(vc_ref[0, :, ps])
            mine.append((lambda ks=ks: kc_ref[0, :, ks], lambda half, vc=vc: vc[half], None))
            if mode != "ctx":
                vl = transpose_values(vl_ref[0, pl.ds(start, band), ps])
                for c0 in range(0, n_lat, KEY_CHUNK):
                    c1 = min(c0 + KEY_CHUNK, n_lat)
                    mine.append((lambda ks=ks, c0=c0, c1=c1: kl_ref[0, pl.ds(start + c0, c1 - c0), ks],
                                 lambda half, vl=vl, c0=c0, c1=c1: vl[half][:, c0:c1],
                                 valid[c0:c1]))
        chunks.append(mine)
    n_chunks = len(chunks[0])

    def head_q(pp, g, half):
        q = q_ref[0, :, (pp * group + g) * ql:(pp * group + g + 1) * ql]
        if split:
            return q[:, half * LANES:(half + 1) * LANES]
        return jnp.where(in_a if half == 0 else jnp.logical_not(in_a), q, jnp.zeros_like(q))

    def scores(item):
        pp, g, half, ci = item
        load_keys, _, ok = chunks[pp][ci]
        kch = load_keys()
        if split:
            kch = kch[:, half * LANES:(half + 1) * LANES]
        s = _dot_t(kch, head_q(pp, g, half))
        return s if ok is None else jnp.where(ok, s, NEG_INF)

    items = [(pp, g, half, ci) for pp in range(pps) for g in range(group) for half in range(2)
             for ci in range(n_chunks)]
    last = n_chunks - 1
    outs = []
    acc = None

    def values_step(pend, acc):
        (pp, g, half, ci), p, alpha, m_end, sink = pend
        pv = _dot(chunks[pp][ci][1](half), p)
        acc = pv if ci == 0 else acc * alpha + pv
        if ci == last:
            den = acc[DEN_ROW:DEN_ROW + 1, :]
            if sink is not None:
                den = den + jnp.exp2(sink - m_end)
            outs.append(acc[:DEN_ROW] * (1.0 / den))
            if half == 1:
                o_t = jnp.concatenate(outs, axis=0)
                o_ref[0, :, (pp * group + g) * LANES:(pp * group + g + 1) * LANES] = o_t.T.astype(BF16)
                outs.clear()
        return acc

    s_next = scores(items[0])
    pending = None
    for idx, item in enumerate(items):
        pp, g, half, ci = item
        s = s_next
        if idx + 1 < len(items):
            s_next = scores(items[idx + 1])
        cm = jnp.max(s, axis=0, keepdims=True)
        if ci == 0:
            sink = sink_ref[(2 * (j * pps + pp) + half) * group + g] * LOG2E if use_sink else None
            m = cm if sink is None else jnp.maximum(cm, sink)
            alpha = None
        else:
            m_new = jnp.maximum(m, cm)
            alpha = jnp.exp2(m - m_new)
            m = m_new
        p = jnp.exp2(s - m).astype(BF16)
        if pending is not None:
            acc = values_step(pending, acc)
        pending = (item, p, alpha, m, sink)
    values_step(pending, acc)


def _attention(q, k, v, n_pairs, group, split, window, sinks, need_ctx):
    nb, s, _ = q.shape
    b_lat = nb - 1
    ctx_len = s // b_lat
    ql = q.shape[2] // (n_pairs * group)
    klw = k.shape[2] // n_pairs
    ow = n_pairs * group * LANES
    tq = min(Q_TILE, s)
    band = min(tq + 2 * SWA_WINDOW, s)
    use_sink = sinks is not None
    pps = max(1, HEADS_PER_STEP // (2 * group))
    mode = "window" if window else "dense"
    values_t = not window
    common = dict(pps=pps, group=group, ql=ql, klw=klw, split=split, tq=tq, band=band, seq=s,
                  use_sink=use_sink, values_t=values_t)
    smem = [pl.BlockSpec(memory_space=pltpu.SMEM)] if use_sink else []
    sink_args = [sinks] if use_sink else []
    qw, kw, o_w = pps * group * ql, pps * klw, pps * group * LANES
    if values_t:
        vr = 2 * pps * VT_ROWS
        v_lat = pl.BlockSpec((1, vr, s), lambda b, j, i: (b, j, 0))
        v_ctx = pl.BlockSpec((1, vr, ctx_len), lambda b, j, i: (b_lat, j, b))
    else:
        v_lat = pl.BlockSpec((1, s, pps * LANES), lambda b, j, i: (b, 0, j))
        v_ctx = pl.BlockSpec((1, ctx_len, pps * LANES), lambda b, j, i: (b_lat, b, j))

    lat_specs = smem + [
        pl.BlockSpec((1, tq, qw), lambda b, j, i: (b, i, j)),
        pl.BlockSpec((1, s, kw), lambda b, j, i: (b, 0, j)),
        v_lat,
        pl.BlockSpec((1, ctx_len, kw), lambda b, j, i: (b_lat, b, j)),
        v_ctx,
    ]
    o_lat = pl.pallas_call(
        functools.partial(_attn_body, mode=mode, **common),
        grid=(b_lat, n_pairs // pps, s // tq),
        in_specs=lat_specs,
        out_specs=pl.BlockSpec((1, tq, o_w), lambda b, j, i: (b, i, j)),
        out_shape=jax.ShapeDtypeStruct((b_lat, s, ow), BF16),
        compiler_params=_cparams("arbitrary", "arbitrary", "arbitrary"),
        name="attn_latent",
    )(*sink_args, q, k, v, k, v)
    if not need_ctx:
        return o_lat, None

    common["tq"] = ctx_len
    ctx_specs = smem + [
        pl.BlockSpec((1, ctx_len, qw), lambda b, j, i: (b_lat, b, j)),
        pl.BlockSpec((1, ctx_len, kw), lambda b, j, i: (b_lat, b, j)),
        v_ctx,
    ]
    o_ctx = pl.pallas_call(
        functools.partial(_attn_body, mode="ctx", **common),
        grid=(b_lat, n_pairs // pps, 1),
        in_specs=ctx_specs,
        out_specs=pl.BlockSpec((1, ctx_len, o_w), lambda b, j, i: (0, b, j)),
        out_shape=jax.ShapeDtypeStruct((1, s, ow), BF16),
        compiler_params=_cparams("arbitrary", "arbitrary", "arbitrary"),
        name="attn_context",
    )(*sink_args, q, k, v)
    return o_lat, o_ctx


def _post_body(*refs, moe, b_lat, has_ctx):
    refs = list(refs)
    o_ref = refs.pop(0)
    oc_ref = refs.pop(0) if has_ctx else None
    if moe:
        x_ref, mod_ref, g_ref, wo_ref, rw_ref, rb_ref, xo_ref, h_ref, route_ref = refs
    else:
        x_ref, mod_ref, g_ref, wo_ref, wgu_ref, wd_ref, xo_ref = refs
    o = o_ref[0]
    if has_ctx:
        o = jnp.where(pl.program_id(0) == b_lat, oc_ref[0], o)
    y = _dot(o, wo_ref[...])
    x = x_ref[0] + mod_ref[0, 2:3, :] * y
    h = _norm_mod(x, g_ref[...], mod_ref[0, 3:4, :], mod_ref[0, 4:5, :])
    h_hi = h.astype(BF16)
    if not moe:
        acc = None
        for c in range(D_FF // FF_CHUNK):
            gate = _dot(h_hi, wgu_ref[:, c * FF_CHUNK:(c + 1) * FF_CHUNK])
            up = _dot(h_hi, wgu_ref[:, D_FF + c * FF_CHUNK:D_FF + (c + 1) * FF_CHUNK])
            act = (_silu(gate) * up).astype(BF16)
            part = _dot(act, wd_ref[c * FF_CHUNK:(c + 1) * FF_CHUNK, :])
            acc = part if acc is None else acc + part
        xo_ref[0] = x + mod_ref[0, 5:6, :] * acc
        return
    xo_ref[0] = x
    _store_rows_as_tiles(h_ref, h)
    h_lo = (h - h_hi.astype(F32)).astype(BF16)
    both = _dot(h_hi, rw_ref[...])
    logits = both[:, :LANES] + both[:, LANES:] + _dot(h_lo, rw_ref[:, :LANES]) + rb_ref[...]
    lane = lax.broadcasted_iota(jnp.int32, logits.shape, 1).astype(F32)
    lg = jnp.where(lane < N_EXPERTS, logits, -jnp.inf)
    v1 = jnp.max(lg, axis=-1, keepdims=True)
    i1 = jnp.min(jnp.where(lg == v1, lane, float(LANES)), axis=-1, keepdims=True)
    lg2 = jnp.where(lane == i1, -jnp.inf, lg)
    v2 = jnp.max(lg2, axis=-1, keepdims=True)
    i2 = jnp.min(jnp.where(lg2 == v2, lane, float(LANES)), axis=-1, keepdims=True)
    e = jnp.exp(v2 - v1)
    g1 = 1.0 / (1.0 + e)
    g2 = e / (1.0 + e)
    route_ref[0] = jnp.where(lane == 0, i1, jnp.where(lane == 1, i2, jnp.where(
        lane == 2, g1, jnp.where(lane == 3, g2, 0.0))))


def _post(o_lat, o_ctx, xs, mods_i, norm_g, w_o, nb, router=None, ffn=None):
    _, s, d = xs.shape
    tm = min(ROW_TILE, s)
    moe = router is not None
    assert moe != (ffn is not None)
    has_ctx = o_ctx is not None
    resident = lambda arr: pl.BlockSpec(arr.shape, lambda b, i: (0, 0), pipeline_mode=pl.Buffered(1))
    b_lat = o_lat.shape[0]
    in_specs = [pl.BlockSpec((1, tm, d), lambda b, i: (jnp.minimum(b, b_lat - 1), i, 0))]
    args = [o_lat]
    if has_ctx:
        in_specs.append(pl.BlockSpec((1, tm, d), lambda b, i: (0, i, 0)))
        args.append(o_ctx)
    n_o = len(args)
    in_specs += [
        pl.BlockSpec((1, tm, d), lambda b, i: (b, i, 0)),
        pl.BlockSpec((1, 6, d), lambda b, i: (b, 0, 0)),
        pl.BlockSpec((1, d), lambda b, i: (0, 0)),
        resident(w_o),
    ]
    args += [xs, mods_i, norm_g.reshape(1, d), w_o]
    out_specs = [pl.BlockSpec((1, tm, d), lambda b, i: (b, i, 0))]
    out_shape = [jax.ShapeDtypeStruct(xs.shape, F32)]
    if moe:
        in_specs += [pl.BlockSpec((d, 2 * LANES), lambda b, i: (0, 0)),
                     pl.BlockSpec((1, LANES), lambda b, i: (0, 0))]
        args += list(router)
        per_b = s // tm
        out_specs += [pl.BlockSpec((tm * ROW_SUB, LANES), lambda b, i: (b * per_b + i, 0)),
                      pl.BlockSpec((1, tm, LANES), lambda b, i: (b, i, 0))]
        out_shape += [jax.ShapeDtypeStruct((nb * s * ROW_SUB, LANES), F32),
                      jax.ShapeDtypeStruct((nb, s, LANES), F32)]
    else:
        in_specs += [resident(w) for w in ffn]
        args += list(ffn)
    return pl.pallas_call(
        functools.partial(_post_body, moe=moe, b_lat=b_lat, has_ctx=has_ctx),
        grid=(nb, s // tm),
        in_specs=in_specs,
        out_specs=out_specs,
        out_shape=out_shape,
        input_output_aliases={n_o: 0},
        compiler_params=_cparams("arbitrary", "arbitrary"),
        name="attn_out_router" if moe else "attn_out_ffn",
    )(*args)


ROW_SUB = D_MODEL // LANES


def _store_rows_as_tiles(ref, x):
    n = x.shape[0]
    for sub in range(ROW_SUB):
        ref[pl.ds(sub, n, stride=ROW_SUB), :] = x[:, sub * LANES:(sub + 1) * LANES]


def _load_row_pieces(ref, first_tile, n):
    return [ref[pl.ds(first_tile * ROW_SUB + sub, n, stride=ROW_SUB), :] for sub in range(ROW_SUB)]


def _tile(ref, row8):
    return ref.at[pl.ds(pl.multiple_of(row8, ROW_SUB), ROW_SUB)]


def _tile_dma_wait(src_ref, dst_ref, sem, tiles):
    n = tiles * ROW_SUB
    pltpu.make_async_copy(src_ref.at[pl.ds(0, n)], dst_ref.at[pl.ds(0, n)], sem).wait()


def _dispatch_body(pos_ref, h_ref, xs_in_ref, xs_ref, sem):
    del xs_in_ref
    tm = h_ref.shape[0] // ROW_SUB

    def issue(r, carry):
        src = _tile(h_ref, r * ROW_SUB)
        for k in range(2):
            pltpu.make_async_copy(src, _tile(xs_ref, pos_ref[0, 0, 2 * r + k]), sem).start()
        return carry

    lax.fori_loop(0, tm, issue, 0, unroll=8)
    _tile_dma_wait(h_ref, xs_ref, sem, tm)
    _tile_dma_wait(h_ref, xs_ref, sem, tm)


def _moe_dispatch(h_tiles, pos8, p, tm):
    t = h_tiles.shape[0] // ROW_SUB
    return pl.pallas_call(
        _dispatch_body,
        grid=(t // tm,),
        in_specs=[
            pl.BlockSpec((1, 1, 2 * tm), lambda i: (i, 0, 0), memory_space=pltpu.SMEM),
            pl.BlockSpec((tm * ROW_SUB, LANES), lambda i: (i, 0)),
            pl.BlockSpec(memory_space=pl.ANY),
        ],
        out_specs=pl.BlockSpec(memory_space=pl.ANY),
        out_shape=jax.ShapeDtypeStruct((p * ROW_SUB, LANES), F32),
        scratch_shapes=[pltpu.SemaphoreType.DMA(())],
        input_output_aliases={2: 0},
        compiler_params=_cparams("arbitrary"),
        name="moe_dispatch",
    )(pos8.reshape(t // tm, 1, 2 * tm), h_tiles, jnp.zeros((p * ROW_SUB, LANES), F32))


def _moe_body(te_ref, nu_ref, xs_ref, wg_ref, wu_ref, wd_ref, ys_ref):
    i = pl.program_id(0)
    tm = xs_ref.shape[0] // ROW_SUB

    @pl.when(i < nu_ref[0])
    def _():
        x = jnp.concatenate(_load_row_pieces(xs_ref, 0, tm), axis=1).astype(BF16)
        act = (_silu(_dot(x, wg_ref[0])) * _dot(x, wu_ref[0])).astype(BF16)
        _store_rows_as_tiles(ys_ref, _dot(act, wd_ref[0]))

    @pl.when(i >= nu_ref[0])
    def _():
        ys_ref[...] = jnp.zeros_like(ys_ref)


def _moe_experts(xs_sorted, tile_expert, n_used, w_gu, w_d, tm):
    p = xs_sorted.shape[0] // ROW_SUB
    f, d = w_d.shape[1:]
    rows = pl.BlockSpec((tm * ROW_SUB, LANES), lambda i, te, nu: (i, 0))
    grid_spec = pltpu.PrefetchScalarGridSpec(
        num_scalar_prefetch=2,
        grid=(p // tm,),
        in_specs=[
            rows,
            pl.BlockSpec((1, d, f), lambda i, te, nu: (te[i], 0, 0)),
            pl.BlockSpec((1, d, f), lambda i, te, nu: (te[i], 0, 1)),
            pl.BlockSpec((1, f, d), lambda i, te, nu: (te[i], 0, 0)),
        ],
        out_specs=rows,
    )
    return pl.pallas_call(
        _moe_body,
        grid_spec=grid_spec,
        out_shape=jax.ShapeDtypeStruct((p * ROW_SUB, LANES), F32),
        compiler_params=_cparams("arbitrary"),
        name="moe_experts",
    )(tile_expert, n_used, xs_sorted, w_gu, w_gu, w_d)


def _combine_body(*refs, final):
    if final:
        pos_ref, x_ref, route_ref, mod_ref, g_ref, ys_ref, o_ref, ybuf, sem = refs
    else:
        pos_ref, x_ref, route_ref, mod_ref, ys_ref, o_ref, ybuf, sem = refs
    tm, d = x_ref.shape[1:]

    def issue(r, carry):
        for k in range(2):
            pltpu.make_async_copy(_tile(ys_ref, pos_ref[0, 0, 2 * r + k]),
                                  _tile(ybuf, (k * tm + r) * ROW_SUB), sem).start()
        return carry

    lax.fori_loop(0, tm, issue, 0, unroll=8)
    _tile_dma_wait(ys_ref, ybuf, sem, 2 * tm)
    route = route_ref[0]
    g1, g2 = route[:, 2:3], route[:, 3:4]
    y1, y2 = _load_row_pieces(ybuf, 0, tm), _load_row_pieces(ybuf, tm, tm)
    cols = [slice(sub * LANES, (sub + 1) * LANES) for sub in range(ROW_SUB)]
    xn = [x_ref[0, :, c] + mod_ref[0, 5:6, c] * (g1 * a + g2 * b) for c, a, b in zip(cols, y1, y2)]
    if final:
        ms = sum(jnp.sum(v * v, axis=-1, keepdims=True) for v in xn) * (1.0 / d)
        scale = lax.rsqrt(ms + NORM_EPS)
        xn = [v * scale * g_ref[:, c] for c, v in zip(cols, xn)]
    for c, v in zip(cols, xn):
        o_ref[0, :, c] = v


def _moe_combine(xs, ys, pos, route, mods_i, nb, final_g=None):
    _, s, d = xs.shape
    tm = min(ROW_TILE, s)
    per_b = s // tm
    final = final_g is not None
    tok = pl.BlockSpec((1, tm, d), lambda b, i: (b, i, 0))
    in_specs = [pl.BlockSpec((1, 1, 2 * tm), lambda b, i: (b * per_b + i, 0, 0), memory_space=pltpu.SMEM),
                tok,
                pl.BlockSpec((1, tm, LANES), lambda b, i: (b, i, 0)),
                pl.BlockSpec((1, 6, d), lambda b, i: (b, 0, 0))]
    args = [pos.reshape(nb * per_b, 1, 2 * tm), xs, route, mods_i]
    if final:
        in_specs.append(pl.BlockSpec((1, d), lambda b, i: (0, 0)))
        args.append(final_g.reshape(1, d))
    in_specs.append(pl.BlockSpec(memory_space=pl.ANY))
    args.append(ys)
    return pl.pallas_call(
        functools.partial(_combine_body, final=final),
        grid=(nb, per_b),
        in_specs=in_specs,
        out_specs=tok,
        out_shape=jax.ShapeDtypeStruct((nb, s, d) if final else xs.shape, F32),
        scratch_shapes=[pltpu.VMEM((2 * tm * ROW_SUB, LANES), F32), pltpu.SemaphoreType.DMA(())],
        input_output_aliases={} if final else {1: 0},
        compiler_params=_cparams("arbitrary", "arbitrary"),
        name="moe_combine",
    )(*args)


def _moe_layer(hp, route, xs, mods_i, w_gu, w_d, nb, final_g):
    _, s, d = xs.shape
    t = nb * s
    tm = min(MOE_TILE, s)
    p = 2 * t + N_EXPERTS * tm
    expert = route.reshape(t, LANES)[:, :2].astype(jnp.int32).reshape(2 * t)
    onehot = (expert[:, None] == jnp.arange(N_EXPERTS, dtype=jnp.int32)[None, :]).astype(jnp.int32)
    csum = jnp.cumsum(onehot, axis=0)
    padded = ((csum[-1] + tm - 1) // tm) * tm
    off_end = jnp.cumsum(padded)
    pos = jnp.sum(onehot * ((off_end - padded)[None, :] + csum - 1), axis=1)
    tile_start = jnp.arange(p // tm, dtype=jnp.int32) * tm
    tile_expert = jnp.minimum(jnp.sum(tile_start[:, None] >= off_end[None, :], axis=1), N_EXPERTS - 1)
    n_used = (off_end[-1] // tm).reshape(1).astype(jnp.int32)

    pos8 = pos * ROW_SUB
    xs_sorted = _moe_dispatch(hp, pos8, p, tm)
    ys = _moe_experts(xs_sorted, tile_expert.astype(jnp.int32), n_used, w_gu, w_d, tm)
    return _moe_combine(xs, ys, pos8, route, mods_i, nb, final_g)


def kernel(x, c, ctx, c_ctx, ada_w, ada_b, norm_g, final_norm_g, mla_w_dq, mla_q_norm_g, mla_w_uq, mla_w_dkv, mla_kv_norm_g, mla_w_ukv, mla_w_o, swa_w_qkv, swa_sinks, swa_w_o, ga_w_qkv, ga_q_norm_g, ga_k_norm_g, ga_w_o, ffn_w_gate_up, ffn_w_down, moe_router_w, moe_router_b, moe_w_gate_up, moe_w_down):
    b, s, d = x.shape
    assert d == D_MODEL and b * ctx.shape[1] == s and s % GRID_W == 0
    nb = b + 1
    xs = jnp.concatenate([x, ctx.reshape(1, s, d)], axis=0)
    c_all = jnp.concatenate([c, c_ctx[None], jnp.zeros((16 - nb, d), F32)], axis=0)
    mods = _mods(c_all, ada_w, ada_b).reshape(DEPTH, 16, 6, d)

    gqa_tab = _gqa_tables(s)
    mla_tab = _mla_tables(s)
    nope_src, rope_src = _mla_lane_src()

    for i in range(DEPTH):
        need_ctx = i < DEPTH - 1
        mods_i = mods[i]
        kind, j = i % 3, i // 3
        if kind == 0:
            dkv = jnp.concatenate([mla_w_dkv[j], jnp.zeros((d, 1), F32)], axis=1)
            kr_cols = np.where(rope_src >= 0, MLA_KV_LORA + rope_src, MLA_KV_LORA + MLA_QK_ROPE)
            w_a = jnp.concatenate([mla_w_dq[j], dkv[:, :MLA_KV_LORA], dkv[:, kr_cols]], axis=1).astype(BF16)
            qd = MLA_QK_NOPE + MLA_QK_ROPE
            q_src = np.where(nope_src >= 0, nope_src, np.where(rope_src >= 0, MLA_QK_NOPE + rope_src, qd))
            uq = jnp.concatenate([mla_w_uq[j].reshape(MLA_Q_LORA, N_HEADS, qd),
                                  jnp.zeros((MLA_Q_LORA, N_HEADS, 1), F32)], axis=2)
            w_uq = uq[:, :, q_src].reshape(MLA_Q_LORA, N_HEADS * LANES).astype(BF16)
            kvd = MLA_QK_NOPE + MLA_V_DIM
            ukv = jnp.concatenate([mla_w_ukv[j].reshape(MLA_KV_LORA, N_HEADS, kvd),
                                   jnp.zeros((MLA_KV_LORA, N_HEADS, 1), F32)], axis=2)
            k_src = np.where(nope_src >= 0, nope_src, kvd)
            w_ukv = jnp.concatenate([ukv[:, :, k_src].reshape(MLA_KV_LORA, N_HEADS * LANES),
                                     ukv[:, :, MLA_QK_NOPE:kvd].reshape(MLA_KV_LORA, N_HEADS * MLA_V_DIM)],
                                    axis=1).astype(BF16)
            q, k, v = _mla_proj(xs, mods_i, norm_g[i, 0], w_a, mla_q_norm_g[j], mla_kv_norm_g[j],
                                w_uq, w_ukv, mla_tab)
            o = _attention(q, k, v, N_HEADS // 2, 1, True, False, None, need_ctx)
            w_o = mla_w_o[j].astype(BF16)
        elif kind == 1:
            cols, o_rows = _gqa_perm(SWA_KV_HEADS, N_HEADS // SWA_KV_HEADS)
            q, k, v = _gqa_proj(xs, mods_i, norm_g[i, 0], swa_w_qkv[j][:, cols].astype(BF16), gqa_tab,
                                SWA_KV_HEADS, values_t=False)
            o = _attention(q, k, v, SWA_KV_HEADS // 2, N_HEADS // SWA_KV_HEADS, False, True,
                           swa_sinks[j], need_ctx)
            w_o = swa_w_o[j][o_rows].astype(BF16)
        else:
            cols, o_rows = _gqa_perm(GA_KV_HEADS, N_HEADS // GA_KV_HEADS)
            q, k, v = _gqa_proj(xs, mods_i, norm_g[i, 0], ga_w_qkv[j][:, cols].astype(BF16), gqa_tab,
                                GA_KV_HEADS, (ga_q_norm_g[j], ga_k_norm_g[j]))
            o = _attention(q, k, v, GA_KV_HEADS // 2, N_HEADS // GA_KV_HEADS, False, False, None, need_ctx)
            w_o = ga_w_o[j][o_rows].astype(BF16)

        n_tok_b = nb if need_ctx else b
        f = i // 2
        if i % 2 == 0:
            xs, = _post(*o, xs, mods_i, norm_g[i, 1], w_o, n_tok_b,
                        ffn=(ffn_w_gate_up[f].astype(BF16), ffn_w_down[f].astype(BF16)))
        else:
            rw = jnp.concatenate([moe_router_w[f], jnp.zeros((d, LANES - N_EXPERTS), F32)], axis=1)
            rw_hi = rw.astype(BF16)
            rw_lo = (rw - rw_hi.astype(F32)).astype(BF16)
            rb = jnp.concatenate([moe_router_b[f], jnp.zeros((LANES - N_EXPERTS,), F32)]).reshape(1, LANES)
            xs, h2, comb = _post(*o, xs, mods_i, norm_g[i, 1], w_o, n_tok_b,
                                 (jnp.concatenate([rw_hi, rw_lo], axis=1), rb))
            xs = _moe_layer(h2, comb, xs, mods_i, moe_w_gate_up[f].astype(BF16), moe_w_down[f].astype(BF16),
                            n_tok_b, None if need_ctx else final_norm_g)
    return xs
```

```python
import functools
import math

import numpy as np
import jax
import jax.numpy as jnp
from jax import lax
from jax.experimental import pallas as pl
from jax.experimental.pallas import tpu as pltpu

F32 = jnp.float32
BF16 = jnp.bfloat16

D_MODEL = 1024
GRID_W = 64
HEAD_DIM = 64
N_HEADS = 16
MLA_Q_LORA = 384
MLA_KV_LORA = 256
MLA_QK_NOPE = 64
MLA_QK_ROPE = 32
MLA_V_DIM = 64
SWA_KV_HEADS = 4
SWA_WINDOW = 128
GA_KV_HEADS = 8
D_FF = 2816
N_EXPERTS = 8
EXPERT_FF = 1408
ROPE_THETA = 10000.0
NORM_EPS = 1e-6
NEG_INF = -1e30
DEPTH = 4
LOG2E = math.log2(math.e)

LANES = 128
VMEM_LIMIT = 56 * 2**20
ROW_TILE = 512
Q_TILE = 512
KEY_CHUNK = 1024
HEADS_PER_STEP = 8
MOE_TILE = 512
FF_CHUNK = 1408


def _cparams(*sem):
    return pltpu.CompilerParams(dimension_semantics=sem, vmem_limit_bytes=VMEM_LIMIT)


def _silu(x):
    return x * (1.0 / (1.0 + jnp.exp(-x)))


def _rms(x, g):
    ms = jnp.mean(x * x, axis=-1, keepdims=True)
    return x * lax.rsqrt(ms + NORM_EPS) * g


def _norm_mod(x, g, shift, scale):
    return _rms(x, g) * (1.0 + scale) + shift


def _dot(a, b):
    return jnp.dot(a, b, preferred_element_type=F32)


def _dot_t(a, b):
    return lax.dot_general(a, b, (((1,), (1,)), ((), ())), preferred_element_type=F32)


def _mods_body(c_ref, w_ref, b_ref, o_ref):
    sc = _silu(c_ref[...]).astype(BF16)
    o_ref[0] = _dot(sc, w_ref[0].astype(BF16)) + b_ref[0]


def _mods(c_all, ada_w, ada_b):
    depth, d, n = ada_w.shape
    rows = c_all.shape[0]
    tn = 1536
    return pl.pallas_call(
        _mods_body,
        grid=(depth, n // tn),
        in_specs=[
            pl.BlockSpec((rows, d), lambda i, j: (0, 0)),
            pl.BlockSpec((1, d, tn), lambda i, j: (i, 0, j)),
            pl.BlockSpec((1, 1, tn), lambda i, j: (i, 0, j)),
        ],
        out_specs=pl.BlockSpec((1, rows, tn), lambda i, j: (i, 0, j)),
        out_shape=jax.ShapeDtypeStruct((depth, rows, n), F32),
        compiler_params=_cparams("arbitrary", "arbitrary"),
        name="adaln_mods",
    )(c_all, ada_w, ada_b.reshape(depth, 1, n))


def _pair_lanes():
    lane = np.arange(LANES)
    is_b = (lane % 64) >= 32
    dim = (lane % 32) + 32 * (lane // 64)
    return is_b, dim


def _gqa_perm(n_kv, group):
    is_b, dim = _pair_lanes()
    nq = N_HEADS * HEAD_DIM
    nk = n_kv * HEAD_DIM
    q_cols, k_cols, o_rows = [], [], []
    nat = np.arange(HEAD_DIM)
    for j in range(n_kv // 2):
        for g in range(group):
            head_a, head_b = (2 * j) * group + g, (2 * j + 1) * group + g
            q_cols.append(np.where(is_b, head_b, head_a) * HEAD_DIM + dim)
            o_rows.append(np.concatenate([head_a * HEAD_DIM + nat, head_b * HEAD_DIM + nat]))
        k_cols.append(nq + np.where(is_b, 2 * j + 1, 2 * j) * HEAD_DIM + dim)
    cols = np.concatenate(q_cols + k_cols + [nq + nk + np.arange(nk)])
    return cols.astype(np.int32), np.concatenate(o_rows).astype(np.int32)


def _angles(s, rot_dim):
    n_freq = rot_dim // 4
    inv = ROPE_THETA ** (-jnp.arange(n_freq, dtype=F32) / n_freq)
    pos = jnp.arange(s)
    rows = (pos // GRID_W).astype(F32)
    cols = (pos % GRID_W).astype(F32)
    return jnp.concatenate([rows[:, None] * inv, cols[:, None] * inv], axis=-1)


def _rope_tables(cos_l, sin_l, q_scale):
    lat = jnp.stack([cos_l * q_scale, sin_l * q_scale, cos_l, sin_l])
    one = jnp.ones_like(cos_l)
    zero = jnp.zeros_like(cos_l)
    ctx = jnp.stack([one * q_scale, zero, one, zero])
    return jnp.stack([lat, ctx])


def _gqa_tables(s):
    ang = _angles(s, HEAD_DIM)
    lane = np.arange(LANES)
    idx = lane % 32
    sign = jnp.asarray(np.where(lane < 64, -1.0, 1.0), F32)
    return _rope_tables(jnp.cos(ang)[:, idx], jnp.sin(ang)[:, idx] * sign, HEAD_DIM ** -0.5 * LOG2E)


def _mla_lane_src():
    lane = np.arange(LANES)
    nope = np.where((lane >= 16) & (lane < 64), lane - 16,
                    np.where((lane >= 80) & (lane < 96), 48 + lane - 80, -1))
    rope = np.where(lane < 16, lane, np.where((lane >= 64) & (lane < 80), 16 + lane - 64, -1))
    return nope, rope


def _mla_tables(s):
    ang = _angles(s, MLA_QK_ROPE)
    lane = np.arange(LANES)
    is_x1 = lane < 16
    is_x2 = (lane >= 64) & (lane < 80)
    idx = np.where(is_x1, lane, np.where(is_x2, lane - 64, 0))
    rot = jnp.asarray(is_x1 | is_x2)
    sign = jnp.asarray(np.where(is_x1, -1.0, np.where(is_x2, 1.0, 0.0)), F32)
    cos_l = jnp.where(rot, jnp.cos(ang)[:, idx], 1.0)
    sin_l = jnp.sin(ang)[:, idx] * sign
    return _rope_tables(cos_l, sin_l, (MLA_QK_NOPE + MLA_QK_ROPE) ** -0.5 * LOG2E)


def _rope(blk, cos, sin):
    return blk * cos + pltpu.roll(blk, 64, 1) * sin


VT_ROWS = 80
DEN_ROW = 64


def _ones_rows(n_cols):
    r = lax.broadcasted_iota(jnp.int32, (VT_ROWS - DEN_ROW, n_cols), 0)
    return jnp.where(r == 0, 1.0, 0.0)


def _store_values_t(vt_ref, v, n_heads):
    v_t = v.T
    tail = _ones_rows(v.shape[0]).astype(vt_ref.dtype)
    for h in range(n_heads):
        vt_ref[0, h * VT_ROWS:h * VT_ROWS + DEN_ROW, :] = v_t[h * DEN_ROW:(h + 1) * DEN_ROW, :].astype(vt_ref.dtype)
        vt_ref[0, h * VT_ROWS + DEN_ROW:(h + 1) * VT_ROWS, :] = tail


def _gqa_proj_body(*refs, nqb, nkb, qk_norm, values_t):
    if qk_norm:
        x_ref, mod_ref, g_ref, w_ref, tab_ref, gq_ref, gk_ref, ind_ref, q_ref, k_ref, v_ref = refs
    else:
        x_ref, mod_ref, g_ref, w_ref, tab_ref, q_ref, k_ref, v_ref = refs
    h = _norm_mod(x_ref[0], g_ref[...], mod_ref[0, 0:1, :], mod_ref[0, 1:2, :]).astype(BF16)
    qkv = _dot(h, w_ref[...])
    cq, sq, ck, sk = tab_ref[0, 0], tab_ref[0, 1], tab_ref[0, 2], tab_ref[0, 3]

    def head_norm(blk, gain):
        ssq = _dot((blk * blk).astype(BF16), ind_ref[...])
        return blk * lax.rsqrt(ssq * (1.0 / HEAD_DIM) + NORM_EPS) * gain

    for c in range(nqb):
        blk = qkv[:, c * LANES:(c + 1) * LANES]
        if qk_norm:
            blk = head_norm(blk, gq_ref[...])
        q_ref[0, :, c * LANES:(c + 1) * LANES] = _rope(blk, cq, sq).astype(BF16)
    for c in range(nkb):
        blk = qkv[:, (nqb + c) * LANES:(nqb + c + 1) * LANES]
        if qk_norm:
            blk = head_norm(blk, gk_ref[...])
        k_ref[0, :, c * LANES:(c + 1) * LANES] = _rope(blk, ck, sk).astype(BF16)
    v = qkv[:, (nqb + nkb) * LANES:]
    if values_t:
        _store_values_t(v_ref, v, 2 * nkb)
    else:
        v_ref[0] = v.astype(BF16)


def _gqa_proj(xs, mods_i, norm_g, w_perm, tables, n_kv, qk_gains=None, values_t=True):
    nb, s, d = xs.shape
    tm = min(ROW_TILE, s)
    nqb = N_HEADS * HEAD_DIM // LANES
    nkb = n_kv * HEAD_DIM // LANES
    wn = w_perm.shape[1]
    in_specs = [
        pl.BlockSpec((1, tm, d), lambda b, i: (b, i, 0)),
        pl.BlockSpec((1, 6, d), lambda b, i: (b, 0, 0)),
        pl.BlockSpec((1, d), lambda b, i: (0, 0)),
        pl.BlockSpec((d, wn), lambda b, i: (0, 0)),
        pl.BlockSpec((1, 4, tm, LANES), lambda b, i: (b // (nb - 1), 0, i, 0)),
    ]
    args = [xs, mods_i, norm_g.reshape(1, d), w_perm, tables]
    if qk_gains is not None:
        is_b, dim = _pair_lanes()
        ind = jnp.asarray(is_b[:, None] == is_b[None, :], BF16)
        in_specs += [pl.BlockSpec((1, LANES), lambda b, i: (0, 0)),
                     pl.BlockSpec((1, LANES), lambda b, i: (0, 0)),
                     pl.BlockSpec((LANES, LANES), lambda b, i: (0, 0))]
        args += [qk_gains[0][dim].reshape(1, LANES), qk_gains[1][dim].reshape(1, LANES), ind]
    out_w = (nqb * LANES, nkb * LANES, nkb * LANES)
    out_specs = [pl.BlockSpec((1, tm, w), lambda b, i: (b, i, 0)) for w in out_w]
    out_shape = [jax.ShapeDtypeStruct((nb, s, w), BF16) for w in out_w]
    if values_t:
        out_specs[2] = pl.BlockSpec((1, n_kv * VT_ROWS, tm), lambda b, i: (b, 0, i))
        out_shape[2] = jax.ShapeDtypeStruct((nb, n_kv * VT_ROWS, s), BF16)
    return pl.pallas_call(
        functools.partial(_gqa_proj_body, nqb=nqb, nkb=nkb, qk_norm=qk_gains is not None, values_t=values_t),
        grid=(nb, s // tm),
        in_specs=in_specs,
        out_specs=out_specs,
        out_shape=out_shape,
        compiler_params=_cparams("arbitrary", "arbitrary"),
        name="gqa_proj",
    )(*args)


def _mla_proj_body(x_ref, mod_ref, g_ref, wa_ref, gq_ref, gkv_ref, wuq_ref, wukv_ref, tab_ref,
                   q_ref, k_ref, v_ref):
    h = _norm_mod(x_ref[0], g_ref[...], mod_ref[0, 0:1, :], mod_ref[0, 1:2, :]).astype(BF16)
    a = _dot(h, wa_ref[...])
    qn = _rms(a[:, :MLA_Q_LORA], gq_ref[...]).astype(BF16)
    cn = _rms(a[:, MLA_Q_LORA:MLA_Q_LORA + MLA_KV_LORA], gkv_ref[...]).astype(BF16)
    kr = a[:, MLA_Q_LORA + MLA_KV_LORA:]
    cq, sq, ck, sk = tab_ref[0, 0], tab_ref[0, 1], tab_ref[0, 2], tab_ref[0, 3]
    q = _dot(qn, wuq_ref[...])
    kv = _dot(cn, wukv_ref[...])
    kr = _rope(kr, ck, sk)
    for hh in range(N_HEADS):
        sl = slice(hh * LANES, (hh + 1) * LANES)
        q_ref[0, :, sl] = _rope(q[:, sl], cq, sq).astype(BF16)
        k_ref[0, :, sl] = (kv[:, sl] + kr).astype(BF16)
    _store_values_t(v_ref, kv[:, N_HEADS * LANES:], N_HEADS)


def _mla_proj(xs, mods_i, norm_g, w_a, gq, gkv, w_uq, w_ukv, tables):
    nb, s, d = xs.shape
    tm = min(ROW_TILE, s)
    full = lambda arr: pl.BlockSpec(arr.shape, lambda b, i: (0,) * arr.ndim)
    gq = gq.reshape(1, -1)
    gkv = gkv.reshape(1, -1)
    g = norm_g.reshape(1, d)
    qk_w = N_HEADS * LANES
    vt_rows = N_HEADS * VT_ROWS
    return pl.pallas_call(
        _mla_proj_body,
        grid=(nb, s // tm),
        in_specs=[
            pl.BlockSpec((1, tm, d), lambda b, i: (b, i, 0)),
            pl.BlockSpec((1, 6, d), lambda b, i: (b, 0, 0)),
            full(g), full(w_a), full(gq), full(gkv), full(w_uq), full(w_ukv),
            pl.BlockSpec((1, 4, tm, LANES), lambda b, i: (b // (nb - 1), 0, i, 0)),
        ],
        out_specs=[pl.BlockSpec((1, tm, qk_w), lambda b, i: (b, i, 0)),
                   pl.BlockSpec((1, tm, qk_w), lambda b, i: (b, i, 0)),
                   pl.BlockSpec((1, vt_rows, tm), lambda b, i: (b, 0, i))],
        out_shape=[jax.ShapeDtypeStruct((nb, s, qk_w), BF16), jax.ShapeDtypeStruct((nb, s, qk_w), BF16),
                   jax.ShapeDtypeStruct((nb, vt_rows, s), BF16)],
        compiler_params=_cparams("arbitrary", "arbitrary"),
        name="mla_proj",
    )(xs, mods_i, g, w_a, gq, gkv, w_uq, w_ukv, tables)


def _attn_body(*refs, pps, group, ql, klw, split, mode, tq, band, seq, use_sink, values_t):
    refs = list(refs)
    sink_ref = refs.pop(0) if use_sink else None
    q_ref = refs.pop(0)
    if mode != "ctx":
        kl_ref, vl_ref = refs.pop(0), refs.pop(0)
    kc_ref, vc_ref, o_ref = refs
    j = pl.program_id(1)
    qi = pl.program_id(2)
    lane = lax.broadcasted_iota(jnp.int32, (1, LANES), 1)
    in_a = (lane & 63) < 32

    def transpose_values(v):
        v_t = v.astype(F32).T
        tail = _ones_rows(v.shape[0])
        return [jnp.concatenate([v_t[h * DEN_ROW:(h + 1) * DEN_ROW], tail], axis=0).astype(BF16)
                for h in range(2)]

    ctx_len = kc_ref.shape[1]
    valid = None
    if mode == "dense":
        n_lat = seq
    elif mode == "window":
        start = jnp.clip(qi * tq - SWA_WINDOW, 0, seq - band)
        start = pl.multiple_of(start, LANES)
        kpos = start + lax.broadcasted_iota(jnp.int32, (band, 1), 0)
        qpos = qi * tq + lax.broadcasted_iota(jnp.int32, (1, tq), 1)
        valid = jnp.abs(qpos - kpos) <= SWA_WINDOW
        n_lat = band
    else:
        n_lat = 0

    chunks = []
    for pp in range(pps):
        ks = slice(pp * klw, (pp + 1) * klw)
        mine = []
        if values_t:
            head_rows = lambda half, pp=pp: slice((2 * pp + half) * VT_ROWS, (2 * pp + half + 1) * VT_ROWS)
            mine.append((lambda ks=ks: kc_ref[0, :, ks], lambda half, hr=head_rows: vc_ref[0, hr(half), :], None))
            for c0 in range(0, n_lat, KEY_CHUNK):
                c1 = min(c0 + KEY_CHUNK, n_lat)
                mine.append((lambda ks=ks, c0=c0, c1=c1: kl_ref[0, c0:c1, ks],
                             lambda half, hr=head_rows, c0=c0, c1=c1: vl_ref[0, hr(half), c0:c1], None))
        else:
            ps = slice(pp * LANES, (pp + 1) * LANES)
            vc = transpose_values(vc_ref[0, :, ps])
            mine.append((lambda ks=ks: kc_ref[0, :, ks], lambda half, vc=vc: vc[half], None))
            if mode != "ctx":
                vl = transpose_values(vl_ref[0, pl.ds(start, band), ps])
                for c0 in range(0, n_lat, KEY_CHUNK):
                    c1 = min(c0 + KEY_CHUNK, n_lat)
                    mine.append((lambda ks=ks, c0=c0, c1=c1: kl_ref[0, pl.ds(start + c0, c1 - c0), ks],
                                 lambda half, vl=vl, c0=c0, c1=c1: vl[half][:, c0:c1],
                                 valid[c0:c1]))
        chunks.append(mine)
    n_chunks = len(chunks[0])

    def head_q(pp, g, half):
        q = q_ref[0, :, (pp * group + g) * ql:(pp * group + g + 1) * ql]
        if split:
            return q[:, half * LANES:(half + 1) * LANES]
        return jnp.where(in_a if half == 0 else jnp.logical_not(in_a), q, jnp.zeros_like(q))

    def scores(item):
        pp, g, half, ci = item
        load_keys, _, ok = chunks[pp][ci]
        kch = load_keys()
        if split:
            kch = kch[:, half * LANES:(half + 1) * LANES]
        s = _dot_t(kch, head_q(pp, g, half))
        return s if ok is None else jnp.where(ok, s, NEG_INF)

    items = [(pp, g, half, ci) for pp in range(pps) for g in range(group) for half in range(2)
             for ci in range(n_chunks)]
    last = n_chunks - 1

    def run(lagged):
        outs = []
        acc = None
        bad = jnp.zeros((1, tq), F32)

        def values_step(pend, acc, bad):
            (pp, g, half, ci), p, scale, m_end, sink = pend
            pv = _dot(chunks[pp][ci][1](half), p)
            if ci == 0:
                acc = pv
            elif lagged:
                acc = (acc + pv) * scale
            else:
                acc = acc * scale + pv
            if ci == last:
                den = acc[DEN_ROW:DEN_ROW + 1, :]
                if sink is not None:
                    den = den + jnp.exp2(sink - m_end)
                out = acc[:DEN_ROW] * (1.0 / den)
                if lagged:
                    finite = (out - out) == 0.0
                    bad = bad + jnp.sum(jnp.where(finite, 0.0, 1.0), axis=0, keepdims=True)
                outs.append(out)
                if half == 1:
                    o_t = jnp.concatenate(outs, axis=0)
                    o_ref[0, :, (pp * group + g) * LANES:(pp * group + g + 1) * LANES] = o_t.T.astype(BF16)
                    outs.clear()
            return acc, bad

        s_next = scores(items[0])
        pending = None
        for idx, item in enumerate(items):
            pp, g, half, ci = item
            s = s_next
            if idx + 1 < len(items):
                s_next = scores(items[idx + 1])
            cm = jnp.max(s, axis=0, keepdims=True)
            if ci == 0:
                sink = sink_ref[(2 * (j * pps + pp) + half) * group + g] * LOG2E if use_sink else None
                m = cm if sink is None else jnp.maximum(cm, sink)
                p = jnp.exp2(s - m).astype(BF16)
                scale = None
            else:
                m_new = jnp.maximum(m, cm)
                scale = jnp.exp2(m - m_new)
                p = jnp.exp2(s - (m if lagged else m_new)).astype(BF16)
                m = m_new
            if pending is not None:
                acc, bad = values_step(pending, acc, bad)
            pending = (item, p, scale, m, sink)
        _, bad = values_step(pending, acc, bad)
        return jnp.max(bad)

    if n_chunks == 1:
        run(lagged=False)
        return
    overflowed = run(lagged=True)

    @pl.when(overflowed > 0.0)
    def _():
        run(lagged=False)


def _attention(q, k, v, n_pairs, group, split, window, sinks, need_ctx):
    nb, s, _ = q.shape
    b_lat = nb - 1
    ctx_len = s // b_lat
    ql = q.shape[2] // (n_pairs * group)
    klw = k.shape[2] // n_pairs
    ow = n_pairs * group * LANES
    tq = min(Q_TILE, s)
    band = min(tq + 2 * SWA_WINDOW, s)
    use_sink = sinks is not None
    pps = max(1, HEADS_PER_STEP // (2 * group))
    mode = "window" if window else "dense"
    values_t = not window
    common = dict(pps=pps, group=group, ql=ql, klw=klw, split=split, tq=tq, band=band, seq=s,
                  use_sink=use_sink, values_t=values_t)
    smem = [pl.BlockSpec(memory_space=pltpu.SMEM)] if use_sink else []
    sink_args = [sinks] if use_sink else []
    qw, kw, o_w = pps * group * ql, pps * klw, pps * group * LANES
    if values_t:
        vr = 2 * pps * VT_ROWS
        v_lat = pl.BlockSpec((1, vr, s), lambda b, j, i: (b, j, 0))
        v_ctx = pl.BlockSpec((1, vr, ctx_len), lambda b, j, i: (b_lat, j, b))
    else:
        v_lat = pl.BlockSpec((1, s, pps * LANES), lambda b, j, i: (b, 0, j))
        v_ctx = pl.BlockSpec((1, ctx_len, pps * LANES), lambda b, j, i: (b_lat, b, j))

    lat_specs = smem + [
        pl.BlockSpec((1, tq, qw), lambda b, j, i: (b, i, j)),
        pl.BlockSpec((1, s, kw), lambda b, j, i: (b, 0, j)),
        v_lat,
        pl.BlockSpec((1, ctx_len, kw), lambda b, j, i: (b_lat, b, j)),
        v_ctx,
    ]
    o_lat = pl.pallas_call(
        functools.partial(_attn_body, mode=mode, **common),
        grid=(b_lat, n_pairs // pps, s // tq),
        in_specs=lat_specs,
        out_specs=pl.BlockSpec((1, tq, o_w), lambda b, j, i: (b, i, j)),
        out_shape=jax.ShapeDtypeStruct((b_lat, s, ow), BF16),
        compiler_params=_cparams("arbitrary", "arbitrary", "arbitrary"),
        name="attn_latent",
    )(*sink_args, q, k, v, k, v)
    if not need_ctx:
        return o_lat, None

    common["tq"] = ctx_len
    ctx_specs = smem + [
        pl.BlockSpec((1, ctx_len, qw), lambda b, j, i: (b_lat, b, j)),
        pl.BlockSpec((1, ctx_len, kw), lambda b, j, i: (b_lat, b, j)),
        v_ctx,
    ]
    o_ctx = pl.pallas_call(
        functools.partial(_attn_body, mode="ctx", **common),
        grid=(b_lat, n_pairs // pps, 1),
        in_specs=ctx_specs,
        out_specs=pl.BlockSpec((1, ctx_len, o_w), lambda b, j, i: (0, b, j)),
        out_shape=jax.ShapeDtypeStruct((1, s, ow), BF16),
        compiler_params=_cparams("arbitrary", "arbitrary", "arbitrary"),
        name="attn_context",
    )(*sink_args, q, k, v)
    return o_lat, o_ctx


def _post_body(*refs, moe, b_lat, has_ctx):
    refs = list(refs)
    o_ref = refs.pop(0)
    oc_ref = refs.pop(0) if has_ctx else None
    if moe:
        x_ref, mod_ref, g_ref, wo_ref, rw_ref, rb_ref, xo_ref, h_ref, route_ref = refs
    else:
        x_ref, mod_ref, g_ref, wo_ref, wgu_ref, wd_ref, xo_ref = refs
    o = o_ref[0]
    if has_ctx:
        o = jnp.where(pl.program_id(0) == b_lat, oc_ref[0], o)
    y = _dot(o, wo_ref[...])
    x = x_ref[0] + mod_ref[0, 2:3, :] * y
    h = _norm_mod(x, g_ref[...], mod_ref[0, 3:4, :], mod_ref[0, 4:5, :])
    h_hi = h.astype(BF16)
    if not moe:
        acc = None
        for c in range(D_FF // FF_CHUNK):
            gate = _dot(h_hi, wgu_ref[:, c * FF_CHUNK:(c + 1) * FF_CHUNK])
            up = _dot(h_hi, wgu_ref[:, D_FF + c * FF_CHUNK:D_FF + (c + 1) * FF_CHUNK])
            act = (_silu(gate) * up).astype(BF16)
            part = _dot(act, wd_ref[c * FF_CHUNK:(c + 1) * FF_CHUNK, :])
            acc = part if acc is None else acc + part
        xo_ref[0] = x + mod_ref[0, 5:6, :] * acc
        return
    xo_ref[0] = x
    _store_rows_as_tiles(h_ref, h)
    h_lo = (h - h_hi.astype(F32)).astype(BF16)
    both = _dot(h_hi, rw_ref[...])
    logits = both[:, :LANES] + both[:, LANES:] + _dot(h_lo, rw_ref[:, :LANES]) + rb_ref[...]
    lane = lax.broadcasted_iota(jnp.int32, logits.shape, 1).astype(F32)
    lg = jnp.where(lane < N_EXPERTS, logits, -jnp.inf)
    v1 = jnp.max(lg, axis=-1, keepdims=True)
    i1 = jnp.min(jnp.where(lg == v1, lane, float(LANES)), axis=-1, keepdims=True)
    lg2 = jnp.where(lane == i1, -jnp.inf, lg)
    v2 = jnp.max(lg2, axis=-1, keepdims=True)
    i2 = jnp.min(jnp.where(lg2 == v2, lane, float(LANES)), axis=-1, keepdims=True)
    e = jnp.exp(v2 - v1)
    g1 = 1.0 / (1.0 + e)
    g2 = e / (1.0 + e)
    route_ref[0] = jnp.where(lane == 0, i1, jnp.where(lane == 1, i2, jnp.where(
        lane == 2, g1, jnp.where(lane == 3, g2, 0.0))))


def _post(o_lat, o_ctx, xs, mods_i, norm_g, w_o, nb, router=None, ffn=None):
    _, s, d = xs.shape
    tm = min(ROW_TILE, s)
    moe = router is not None
    assert moe != (ffn is not None)
    has_ctx = o_ctx is not None
    resident = lambda arr: pl.BlockSpec(arr.shape, lambda b, i: (0, 0), pipeline_mode=pl.Buffered(1))
    b_lat = o_lat.shape[0]
    in_specs = [pl.BlockSpec((1, tm, d), lambda b, i: (jnp.minimum(b, b_lat - 1), i, 0))]
    args = [o_lat]
    if has_ctx:
        in_specs.append(pl.BlockSpec((1, tm, d), lambda b, i: (0, i, 0)))
        args.append(o_ctx)
    n_o = len(args)
    in_specs += [
        pl.BlockSpec((1, tm, d), lambda b, i: (b, i, 0)),
        pl.BlockSpec((1, 6, d), lambda b, i: (b, 0, 0)),
        pl.BlockSpec((1, d), lambda b, i: (0, 0)),
        resident(w_o),
    ]
    args += [xs, mods_i, norm_g.reshape(1, d), w_o]
    out_specs = [pl.BlockSpec((1, tm, d), lambda b, i: (b, i, 0))]
    out_shape = [jax.ShapeDtypeStruct(xs.shape, F32)]
    if moe:
        in_specs += [pl.BlockSpec((d, 2 * LANES), lambda b, i: (0, 0)),
                     pl.BlockSpec((1, LANES), lambda b, i: (0, 0))]
        args += list(router)
        per_b = s // tm
        out_specs += [pl.BlockSpec((tm * ROW_SUB, LANES), lambda b, i: (b * per_b + i, 0)),
                      pl.BlockSpec((1, tm, LANES), lambda b, i: (b, i, 0))]
        out_shape += [jax.ShapeDtypeStruct((nb * s * ROW_SUB, LANES), F32),
                      jax.ShapeDtypeStruct((nb, s, LANES), F32)]
    else:
        in_specs += [resident(w) for w in ffn]
        args += list(ffn)
    return pl.pallas_call(
        functools.partial(_post_body, moe=moe, b_lat=b_lat, has_ctx=has_ctx),
        grid=(nb, s // tm),
        in_specs=in_specs,
        out_specs=out_specs,
        out_shape=out_shape,
        input_output_aliases={n_o: 0},
        compiler_params=_cparams("arbitrary", "arbitrary"),
        name="attn_out_router" if moe else "attn_out_ffn",
    )(*args)


ROW_SUB = D_MODEL // LANES


def _store_rows_as_tiles(ref, x):
    n = x.shape[0]
    for sub in range(ROW_SUB):
        ref[pl.ds(sub, n, stride=ROW_SUB), :] = x[:, sub * LANES:(sub + 1) * LANES]


def _load_row_pieces(ref, first_tile, n):
    return [ref[pl.ds(first_tile * ROW_SUB + sub, n, stride=ROW_SUB), :] for sub in range(ROW_SUB)]


def _tile(ref, row8):
    return ref.at[pl.ds(pl.multiple_of(row8, ROW_SUB), ROW_SUB)]


def _tile_dma_wait(src_ref, dst_ref, sem, tiles):
    n = tiles * ROW_SUB
    pltpu.make_async_copy(src_ref.at[pl.ds(0, n)], dst_ref.at[pl.ds(0, n)], sem).wait()


def _dispatch_body(pos_ref, h_ref, xs_in_ref, xs_ref, sem):
    del xs_in_ref
    tm = h_ref.shape[0] // ROW_SUB

    def issue(r, carry):
        src = _tile(h_ref, r * ROW_SUB)
        for k in range(2):
            pltpu.make_async_copy(src, _tile(xs_ref, pos_ref[0, 0, 2 * r + k]), sem).start()
        return carry

    lax.fori_loop(0, tm, issue, 0, unroll=8)
    _tile_dma_wait(h_ref, xs_ref, sem, tm)
    _tile_dma_wait(h_ref, xs_ref, sem, tm)


def _moe_dispatch(h_tiles, pos8, p, tm):
    t = h_tiles.shape[0] // ROW_SUB
    return pl.pallas_call(
        _dispatch_body,
        grid=(t // tm,),
        in_specs=[
            pl.BlockSpec((1, 1, 2 * tm), lambda i: (i, 0, 0), memory_space=pltpu.SMEM),
            pl.BlockSpec((tm * ROW_SUB, LANES), lambda i: (i, 0)),
            pl.BlockSpec(memory_space=pl.ANY),
        ],
        out_specs=pl.BlockSpec(memory_space=pl.ANY),
        out_shape=jax.ShapeDtypeStruct((p * ROW_SUB, LANES), F32),
        scratch_shapes=[pltpu.SemaphoreType.DMA(())],
        input_output_aliases={2: 0},
        compiler_params=_cparams("arbitrary"),
        name="moe_dispatch",
    )(pos8.reshape(t // tm, 1, 2 * tm), h_tiles, jnp.zeros((p * ROW_SUB, LANES), F32))


def _moe_body(te_ref, nu_ref, xs_ref, wg_ref, wu_ref, wd_ref, ys_ref):
    i = pl.program_id(0)
    tm = xs_ref.shape[0] // ROW_SUB

    @pl.when(i < nu_ref[0])
    def _():
        x = jnp.concatenate(_load_row_pieces(xs_ref, 0, tm), axis=1).astype(BF16)
        act = (_silu(_dot(x, wg_ref[0])) * _dot(x, wu_ref[0])).astype(BF16)
        _store_rows_as_tiles(ys_ref, _dot(act, wd_ref[0]))

    @pl.when(i >= nu_ref[0])
    def _():
        ys_ref[...] = jnp.zeros_like(ys_ref)


def _moe_experts(xs_sorted, tile_expert, n_used, w_gu, w_d, tm):
    p = xs_sorted.shape[0] // ROW_SUB
    f, d = w_d.shape[1:]
    rows = pl.BlockSpec((tm * ROW_SUB, LANES), lambda i, te, nu: (i, 0))
    grid_spec = pltpu.PrefetchScalarGridSpec(
        num_scalar_prefetch=2,
        grid=(p // tm,),
        in_specs=[
            rows,
            pl.BlockSpec((1, d, f), lambda i, te, nu: (te[i], 0, 0)),
            pl.BlockSpec((1, d, f), lambda i, te, nu: (te[i], 0, 1)),
            pl.BlockSpec((1, f, d), lambda i, te, nu: (te[i], 0, 0)),
        ],
        out_specs=rows,
    )
    return pl.pallas_call(
        _moe_body,
        grid_spec=grid_spec,
        out_shape=jax.ShapeDtypeStruct((p * ROW_SUB, LANES), F32),
        compiler_params=_cparams("arbitrary"),
        name="moe_experts",
    )(tile_expert, n_used, xs_sorted, w_gu, w_gu, w_d)


def _combine_body(*refs, final):
    if final:
        pos_ref, x_ref, route_ref, mod_ref, g_ref, ys_ref, o_ref, ybuf, sem = refs
    else:
        pos_ref, x_ref, route_ref, mod_ref, ys_ref, o_ref, ybuf, sem = refs
    tm, d = x_ref.shape[1:]

    def issue(r, carry):
        for k in range(2):
            pltpu.make_async_copy(_tile(ys_ref, pos_ref[0, 0, 2 * r + k]),
                                  _tile(ybuf, (k * tm + r) * ROW_SUB), sem).start()
        return carry

    lax.fori_loop(0, tm, issue, 0, unroll=8)
    _tile_dma_wait(ys_ref, ybuf, sem, 2 * tm)
    route = route_ref[0]
    g1, g2 = route[:, 2:3], route[:, 3:4]
    y1, y2 = _load_row_pieces(ybuf, 0, tm), _load_row_pieces(ybuf, tm, tm)
    cols = [slice(sub * LANES, (sub + 1) * LANES) for sub in range(ROW_SUB)]
    xn = [x_ref[0, :, c] + mod_ref[0, 5:6, c] * (g1 * a + g2 * b) for c, a, b in zip(cols, y1, y2)]
    if final:
        ms = sum(jnp.sum(v * v, axis=-1, keepdims=True) for v in xn) * (1.0 / d)
        scale = lax.rsqrt(ms + NORM_EPS)
        xn = [v * scale * g_ref[:, c] for c, v in zip(cols, xn)]
    for c, v in zip(cols, xn):
        o_ref[0, :, c] = v


def _moe_combine(xs, ys, pos, route, mods_i, nb, final_g=None):
    _, s, d = xs.shape
    tm = min(ROW_TILE, s)
    per_b = s // tm
    final = final_g is not None
    tok = pl.BlockSpec((1, tm, d), lambda b, i: (b, i, 0))
    in_specs = [pl.BlockSpec((1, 1, 2 * tm), lambda b, i: (b * per_b + i, 0, 0), memory_space=pltpu.SMEM),
                tok,
                pl.BlockSpec((1, tm, LANES), lambda b, i: (b, i, 0)),
                pl.BlockSpec((1, 6, d), lambda b, i: (b, 0, 0))]
    args = [pos.reshape(nb * per_b, 1, 2 * tm), xs, route, mods_i]
    if final:
        in_specs.append(pl.BlockSpec((1, d), lambda b, i: (0, 0)))
        args.append(final_g.reshape(1, d))
    in_specs.append(pl.BlockSpec(memory_space=pl.ANY))
    args.append(ys)
    return pl.pallas_call(
        functools.partial(_combine_body, final=final),
        grid=(nb, per_b),
        in_specs=in_specs,
        out_specs=tok,
        out_shape=jax.ShapeDtypeStruct((nb, s, d) if final else xs.shape, F32),
        scratch_shapes=[pltpu.VMEM((2 * tm * ROW_SUB, LANES), F32), pltpu.SemaphoreType.DMA(())],
        input_output_aliases={} if final else {1: 0},
        compiler_params=_cparams("arbitrary", "arbitrary"),
        name="moe_combine",
    )(*args)


def _moe_layer(hp, route, xs, mods_i, w_gu, w_d, nb, final_g):
    _, s, d = xs.shape
    t = nb * s
    tm = min(MOE_TILE, s)
    p = 2 * t + N_EXPERTS * tm
    expert = route.reshape(t, LANES)[:, :2].astype(jnp.int32).reshape(2 * t)
    onehot = (expert[:, None] == jnp.arange(N_EXPERTS, dtype=jnp.int32)[None, :]).astype(jnp.int32)
    csum = jnp.cumsum(onehot, axis=0)
    padded = ((csum[-1] + tm - 1) // tm) * tm
    off_end = jnp.cumsum(padded)
    pos = jnp.sum(onehot * ((off_end - padded)[None, :] + csum - 1), axis=1)
    tile_start = jnp.arange(p // tm, dtype=jnp.int32) * tm
    tile_expert = jnp.minimum(jnp.sum(tile_start[:, None] >= off_end[None, :], axis=1), N_EXPERTS - 1)
    n_used = (off_end[-1] // tm).reshape(1).astype(jnp.int32)

    pos8 = pos * ROW_SUB
    xs_sorted = _moe_dispatch(hp, pos8, p, tm)
    ys = _moe_experts(xs_sorted, tile_expert.astype(jnp.int32), n_used, w_gu, w_d, tm)
    return _moe_combine(xs, ys, pos8, route, mods_i, nb, final_g)


def kernel(x, c, ctx, c_ctx, ada_w, ada_b, norm_g, final_norm_g, mla_w_dq, mla_q_norm_g, mla_w_uq, mla_w_dkv, mla_kv_norm_g, mla_w_ukv, mla_w_o, swa_w_qkv, swa_sinks, swa_w_o, ga_w_qkv, ga_q_norm_g, ga_k_norm_g, ga_w_o, ffn_w_gate_up, ffn_w_down, moe_router_w, moe_router_b, moe_w_gate_up, moe_w_down):
    b, s, d = x.shape
    assert d == D_MODEL and b * ctx.shape[1] == s and s % GRID_W == 0
    nb = b + 1
    xs = jnp.concatenate([x, ctx.reshape(1, s, d)], axis=0)
    c_all = jnp.concatenate([c, c_ctx[None], jnp.zeros((16 - nb, d), F32)], axis=0)
    mods = _mods(c_all, ada_w, ada_b).reshape(DEPTH, 16, 6, d)

    gqa_tab = _gqa_tables(s)
    mla_tab = _mla_tables(s)
    nope_src, rope_src = _mla_lane_src()

    for i in range(DEPTH):
        need_ctx = i < DEPTH - 1
        mods_i = mods[i]
        kind, j = i % 3, i // 3
        if kind == 0:
            dkv = jnp.concatenate([mla_w_dkv[j], jnp.zeros((d, 1), F32)], axis=1)
            kr_cols = np.where(rope_src >= 0, MLA_KV_LORA + rope_src, MLA_KV_LORA + MLA_QK_ROPE)
            w_a = jnp.concatenate([mla_w_dq[j], dkv[:, :MLA_KV_LORA], dkv[:, kr_cols]], axis=1).astype(BF16)
            qd = MLA_QK_NOPE + MLA_QK_ROPE
            q_src = np.where(nope_src >= 0, nope_src, np.where(rope_src >= 0, MLA_QK_NOPE + rope_src, qd))
            uq = jnp.concatenate([mla_w_uq[j].reshape(MLA_Q_LORA, N_HEADS, qd),
                                  jnp.zeros((MLA_Q_LORA, N_HEADS, 1), F32)], axis=2)
            w_uq = uq[:, :, q_src].reshape(MLA_Q_LORA, N_HEADS * LANES).astype(BF16)
            kvd = MLA_QK_NOPE + MLA_V_DIM
            ukv = jnp.concatenate([mla_w_ukv[j].reshape(MLA_KV_LORA, N_HEADS, kvd),
                                   jnp.zeros((MLA_KV_LORA, N_HEADS, 1), F32)], axis=2)
            k_src = np.where(nope_src >= 0, nope_src, kvd)
            w_ukv = jnp.concatenate([ukv[:, :, k_src].reshape(MLA_KV_LORA, N_HEADS * LANES),
                                     ukv[:, :, MLA_QK_NOPE:kvd].reshape(MLA_KV_LORA, N_HEADS * MLA_V_DIM)],
                                    axis=1).astype(BF16)
            q, k, v = _mla_proj(xs, mods_i, norm_g[i, 0], w_a, mla_q_norm_g[j], mla_kv_norm_g[j],
                                w_uq, w_ukv, mla_tab)
            o = _attention(q, k, v, N_HEADS // 2, 1, True, False, None, need_ctx)
            w_o = mla_w_o[j].astype(BF16)
        elif kind == 1:
            cols, o_rows = _gqa_perm(SWA_KV_HEADS, N_HEADS // SWA_KV_HEADS)
            q, k, v = _gqa_proj(xs, mods_i, norm_g[i, 0], swa_w_qkv[j][:, cols].astype(BF16), gqa_tab,
                                SWA_KV_HEADS, values_t=False)
            o = _attention(q, k, v, SWA_KV_HEADS // 2, N_HEADS // SWA_KV_HEADS, False, True,
                           swa_sinks[j], need_ctx)
            w_o = swa_w_o[j][o_rows].astype(BF16)
        else:
            cols, o_rows = _gqa_perm(GA_KV_HEADS, N_HEADS // GA_KV_HEADS)
            q, k, v = _gqa_proj(xs, mods_i, norm_g[i, 0], ga_w_qkv[j][:, cols].astype(BF16), gqa_tab,
                                GA_KV_HEADS, (ga_q_norm_g[j], ga_k_norm_g[j]))
            o = _attention(q, k, v, GA_KV_HEADS // 2, N_HEADS // GA_KV_HEADS, False, False, None, need_ctx)
            w_o = ga_w_o[j][o_rows].astype(BF16)

        n_tok_b = nb if need_ctx else b
        f = i // 2
        if i % 2 == 0:
            xs, = _post(*o, xs, mods_i, norm_g[i, 1], w_o, n_tok_b,
                        ffn=(ffn_w_gate_up[f].astype(BF16), ffn_w_down[f].astype(BF16)))
        else:
            rw = jnp.concatenate([moe_router_w[f], jnp.zeros((d, LANES - N_EXPERTS), F32)], axis=1)
            rw_hi = rw.astype(BF16)
            rw_lo = (rw - rw_hi.astype(F32)).astype(BF16)
            rb = jnp.concatenate([moe_router_b[f], jnp.zeros((LANES - N_EXPERTS,), F32)]).reshape(1, LANES)
            xs, h2, comb = _post(*o, xs, mods_i, norm_g[i, 1], w_o, n_tok_b,
                                 (jnp.concatenate([rw_hi, rw_lo], axis=1), rb))
            xs = _moe_layer(h2, comb, xs, mods_i, moe_w_gate_up[f].astype(BF16), moe_w_down[f].astype(BF16),
                            n_tok_b, None if need_ctx else final_norm_g)
    return xs
```

```python
import functools
import math

import numpy as np
import jax
import jax.numpy as jnp
from jax import lax
from jax.experimental import pallas as pl
from jax.experimental.pallas import tpu as pltpu

F32 = jnp.float32
BF16 = jnp.bfloat16

D_MODEL = 1024
GRID_W = 64
HEAD_DIM = 64
N_HEADS = 16
MLA_Q_LORA = 384
MLA_KV_LORA = 256
MLA_QK_NOPE = 64
MLA_QK_ROPE = 32
MLA_V_DIM = 64
SWA_KV_HEADS = 4
SWA_WINDOW = 128
GA_KV_HEADS = 8
D_FF = 2816
N_EXPERTS = 8
EXPERT_FF = 1408
ROPE_THETA = 10000.0
NORM_EPS = 1e-6
NEG_INF = -1e30
DEPTH = 4
LOG2E = math.log2(math.e)

LANES = 128
VMEM_LIMIT = 56 * 2**20
ROW_TILE = 512
Q_TILE = 512
KEY_CHUNK = 1024
HEADS_PER_STEP = 8
MOE_TILE = 512
FF_CHUNK = 1408


def _cparams(*sem):
    return pltpu.CompilerParams(dimension_semantics=sem, vmem_limit_bytes=VMEM_LIMIT)


def _silu(x):
    return x * (1.0 / (1.0 + jnp.exp(-x)))


def _rms(x, g):
    ms = jnp.mean(x * x, axis=-1, keepdims=True)
    return x * lax.rsqrt(ms + NORM_EPS) * g


def _norm_mod(x, g, shift, scale):
    return _rms(x, g) * (1.0 + scale) + shift


def _dot(a, b):
    return jnp.dot(a, b, preferred_element_type=F32)


def _dot_t(a, b):
    return lax.dot_general(a, b, (((1,), (1,)), ((), ())), preferred_element_type=F32)


def _mods_body(c_ref, w_ref, b_ref, o_ref):
    sc = _silu(c_ref[...]).astype(BF16)
    o_ref[0] = _dot(sc, w_ref[0].astype(BF16)) + b_ref[0]


def _mods(c_all, ada_w, ada_b):
    depth, d, n = ada_w.shape
    rows = c_all.shape[0]
    tn = 1536
    return pl.pallas_call(
        _mods_body,
        grid=(depth, n // tn),
        in_specs=[
            pl.BlockSpec((rows, d), lambda i, j: (0, 0)),
            pl.BlockSpec((1, d, tn), lambda i, j: (i, 0, j)),
            pl.BlockSpec((1, 1, tn), lambda i, j: (i, 0, j)),
        ],
        out_specs=pl.BlockSpec((1, rows, tn), lambda i, j: (i, 0, j)),
        out_shape=jax.ShapeDtypeStruct((depth, rows, n), F32),
        compiler_params=_cparams("arbitrary", "arbitrary"),
        name="adaln_mods",
    )(c_all, ada_w, ada_b.reshape(depth, 1, n))


def _pair_lanes():
    lane = np.arange(LANES)
    is_b = (lane % 64) >= 32
    dim = (lane % 32) + 32 * (lane // 64)
    return is_b, dim


def _gqa_perm(n_kv, group):
    is_b, dim = _pair_lanes()
    nq = N_HEADS * HEAD_DIM
    nk = n_kv * HEAD_DIM
    q_cols, k_cols, o_rows = [], [], []
    nat = np.arange(HEAD_DIM)
    for j in range(n_kv // 2):
        for g in range(group):
            head_a, head_b = (2 * j) * group + g, (2 * j + 1) * group + g
            q_cols.append(np.where(is_b, head_b, head_a) * HEAD_DIM + dim)
            o_rows.append(np.concatenate([head_a * HEAD_DIM + nat, head_b * HEAD_DIM + nat]))
        k_cols.append(nq + np.where(is_b, 2 * j + 1, 2 * j) * HEAD_DIM + dim)
    cols = np.concatenate(q_cols + k_cols + [nq + nk + np.arange(nk)])
    return cols.astype(np.int32), np.concatenate(o_rows).astype(np.int32)


def _angles(s, rot_dim):
    n_freq = rot_dim // 4
    inv = ROPE_THETA ** (-jnp.arange(n_freq, dtype=F32) / n_freq)
    pos = jnp.arange(s)
    rows = (pos // GRID_W).astype(F32)
    cols = (pos % GRID_W).astype(F32)
    return jnp.concatenate([rows[:, None] * inv, cols[:, None] * inv], axis=-1)


def _rope_tables(cos_l, sin_l, q_scale):
    lat = jnp.stack([cos_l * q_scale, sin_l * q_scale, cos_l, sin_l])
    one = jnp.ones_like(cos_l)
    zero = jnp.zeros_like(cos_l)
    ctx = jnp.stack([one * q_scale, zero, one, zero])
    return jnp.stack([lat, ctx])


def _gqa_tables(s):
    ang = _angles(s, HEAD_DIM)
    lane = np.arange(LANES)
    idx = lane % 32
    sign = jnp.asarray(np.where(lane < 64, -1.0, 1.0), F32)
    return _rope_tables(jnp.cos(ang)[:, idx], jnp.sin(ang)[:, idx] * sign, HEAD_DIM ** -0.5 * LOG2E)


def _mla_lane_src():
    lane = np.arange(LANES)
    nope = np.where((lane >= 16) & (lane < 64), lane - 16,
                    np.where((lane >= 80) & (lane < 96), 48 + lane - 80, -1))
    rope = np.where(lane < 16, lane, np.where((lane >= 64) & (lane < 80), 16 + lane - 64, -1))
    return nope, rope


def _mla_tables(s):
    ang = _angles(s, MLA_QK_ROPE)
    lane = np.arange(LANES)
    is_x1 = lane < 16
    is_x2 = (lane >= 64) & (lane < 80)
    idx = np.where(is_x1, lane, np.where(is_x2, lane - 64, 0))
    rot = jnp.asarray(is_x1 | is_x2)
    sign = jnp.asarray(np.where(is_x1, -1.0, np.where(is_x2, 1.0, 0.0)), F32)
    cos_l = jnp.where(rot, jnp.cos(ang)[:, idx], 1.0)
    sin_l = jnp.sin(ang)[:, idx] * sign
    return _rope_tables(cos_l, sin_l, (MLA_QK_NOPE + MLA_QK_ROPE) ** -0.5 * LOG2E)


def _rope(blk, cos, sin):
    return blk * cos + pltpu.roll(blk, 64, 1) * sin


VT_ROWS = 80
DEN_ROW = 64


def _ones_rows(n_cols):
    r = lax.broadcasted_iota(jnp.int32, (VT_ROWS - DEN_ROW, n_cols), 0)
    return jnp.where(r == 0, 1.0, 0.0)


def _store_values_t(vt_ref, v, n_heads):
    v_t = v.T
    tail = _ones_rows(v.shape[0]).astype(vt_ref.dtype)
    for h in range(n_heads):
        vt_ref[0, h * VT_ROWS:h * VT_ROWS + DEN_ROW, :] = v_t[h * DEN_ROW:(h + 1) * DEN_ROW, :].astype(vt_ref.dtype)
        vt_ref[0, h * VT_ROWS + DEN_ROW:(h + 1) * VT_ROWS, :] = tail


def _gqa_proj_body(*refs, nqb, nkb, qk_norm, values_t):
    if qk_norm:
        x_ref, mod_ref, g_ref, w_ref, tab_ref, gq_ref, gk_ref, ind_ref, q_ref, k_ref, v_ref = refs
    else:
        x_ref, mod_ref, g_ref, w_ref, tab_ref, q_ref, k_ref, v_ref = refs
    h = _norm_mod(x_ref[0], g_ref[...], mod_ref[0, 0:1, :], mod_ref[0, 1:2, :]).astype(BF16)
    qkv = _dot(h, w_ref[...])
    cq, sq, ck, sk = tab_ref[0, 0], tab_ref[0, 1], tab_ref[0, 2], tab_ref[0, 3]

    def head_norm(blk, gain):
        ssq = _dot((blk * blk).astype(BF16), ind_ref[...])
        return blk * lax.rsqrt(ssq * (1.0 / HEAD_DIM) + NORM_EPS) * gain

    for c in range(nqb):
        blk = qkv[:, c * LANES:(c + 1) * LANES]
        if qk_norm:
            blk = head_norm(blk, gq_ref[...])
        q_ref[0, :, c * LANES:(c + 1) * LANES] = _rope(blk, cq, sq).astype(BF16)
    for c in range(nkb):
        blk = qkv[:, (nqb + c) * LANES:(nqb + c + 1) * LANES]
        if qk_norm:
            blk = head_norm(blk, gk_ref[...])
        k_ref[0, :, c * LANES:(c + 1) * LANES] = _rope(blk, ck, sk).astype(BF16)
    v = qkv[:, (nqb + nkb) * LANES:]
    if values_t:
        _store_values_t(v_ref, v, 2 * nkb)
    else:
        v_ref[0] = v.astype(BF16)


def _gqa_proj(xs, mods_i, norm_g, w_perm, tables, n_kv, qk_gains=None, values_t=True):
    nb, s, d = xs.shape
    tm = min(ROW_TILE, s)
    nqb = N_HEADS * HEAD_DIM // LANES
    nkb = n_kv * HEAD_DIM // LANES
    wn = w_perm.shape[1]
    in_specs = [
        pl.BlockSpec((1, tm, d), lambda b, i: (b, i, 0)),
        pl.BlockSpec((1, 6, d), lambda b, i: (b, 0, 0)),
        pl.BlockSpec((1, d), lambda b, i: (0, 0)),
        pl.BlockSpec((d, wn), lambda b, i: (0, 0)),
        pl.BlockSpec((1, 4, tm, LANES), lambda b, i: (b // (nb - 1), 0, i, 0)),
    ]
    args = [xs, mods_i, norm_g.reshape(1, d), w_perm, tables]
    if qk_gains is not None:
        is_b, dim = _pair_lanes()
        ind = jnp.asarray(is_b[:, None] == is_b[None, :], BF16)
        in_specs += [pl.BlockSpec((1, LANES), lambda b, i: (0, 0)),
                     pl.BlockSpec((1, LANES), lambda b, i: (0, 0)),
                     pl.BlockSpec((LANES, LANES), lambda b, i: (0, 0))]
        args += [qk_gains[0][dim].reshape(1, LANES), qk_gains[1][dim].reshape(1, LANES), ind]
    out_w = (nqb * LANES, nkb * LANES, nkb * LANES)
    out_specs = [pl.BlockSpec((1, tm, w), lambda b, i: (b, i, 0)) for w in out_w]
    out_shape = [jax.ShapeDtypeStruct((nb, s, w), BF16) for w in out_w]
    if values_t:
        out_specs[2] = pl.BlockSpec((1, n_kv * VT_ROWS, tm), lambda b, i: (b, 0, i))
        out_shape[2] = jax.ShapeDtypeStruct((nb, n_kv * VT_ROWS, s), BF16)
    return pl.pallas_call(
        functools.partial(_gqa_proj_body, nqb=nqb, nkb=nkb, qk_norm=qk_gains is not None, values_t=values_t),
        grid=(nb, s // tm),
        in_specs=in_specs,
        out_specs=out_specs,
        out_shape=out_shape,
        compiler_params=_cparams("arbitrary", "arbitrary"),
        name="gqa_proj",
    )(*args)


def _mla_proj_body(x_ref, mod_ref, g_ref, wa_ref, gq_ref, gkv_ref, wuq_ref, wukv_ref, tab_ref,
                   q_ref, k_ref, v_ref):
    h = _norm_mod(x_ref[0], g_ref[...], mod_ref[0, 0:1, :], mod_ref[0, 1:2, :]).astype(BF16)
    a = _dot(h, wa_ref[...])
    qn = _rms(a[:, :MLA_Q_LORA], gq_ref[...]).astype(BF16)
    cn = _rms(a[:, MLA_Q_LORA:MLA_Q_LORA + MLA_KV_LORA], gkv_ref[...]).astype(BF16)
    kr = a[:, MLA_Q_LORA + MLA_KV_LORA:]
    cq, sq, ck, sk = tab_ref[0, 0], tab_ref[0, 1], tab_ref[0, 2], tab_ref[0, 3]
    q = _dot(qn, wuq_ref[...])
    kv = _dot(cn, wukv_ref[...])
    kr = _rope(kr, ck, sk)
    for hh in range(N_HEADS):
        sl = slice(hh * LANES, (hh + 1) * LANES)
        q_ref[0, :, sl] = _rope(q[:, sl], cq, sq).astype(BF16)
        k_ref[0, :, sl] = (kv[:, sl] + kr).astype(BF16)
    _store_values_t(v_ref, kv[:, N_HEADS * LANES:], N_HEADS)


def _mla_proj(xs, mods_i, norm_g, w_a, gq, gkv, w_uq, w_ukv, tables):
    nb, s, d = xs.shape
    tm = min(ROW_TILE, s)
    full = lambda arr: pl.BlockSpec(arr.shape, lambda b, i: (0,) * arr.ndim)
    gq = gq.reshape(1, -1)
    gkv = gkv.reshape(1, -1)
    g = norm_g.reshape(1, d)
    qk_w = N_HEADS * LANES
    vt_rows = N_HEADS * VT_ROWS
    return pl.pallas_call(
        _mla_proj_body,
        grid=(nb, s // tm),
        in_specs=[
            pl.BlockSpec((1, tm, d), lambda b, i: (b, i, 0)),
            pl.BlockSpec((1, 6, d), lambda b, i: (b, 0, 0)),
            full(g), full(w_a), full(gq), full(gkv), full(w_uq), full(w_ukv),
            pl.BlockSpec((1, 4, tm, LANES), lambda b, i: (b // (nb - 1), 0, i, 0)),
        ],
        out_specs=[pl.BlockSpec((1, tm, qk_w), lambda b, i: (b, i, 0)),
                   pl.BlockSpec((1, tm, qk_w), lambda b, i: (b, i, 0)),
                   pl.BlockSpec((1, vt_rows, tm), lambda b, i: (b, 0, i))],
        out_shape=[jax.ShapeDtypeStruct((nb, s, qk_w), BF16), jax.ShapeDtypeStruct((nb, s, qk_w), BF16),
                   jax.ShapeDtypeStruct((nb, vt_rows, s), BF16)],
        compiler_params=_cparams("arbitrary", "arbitrary"),
        name="mla_proj",
    )(xs, mods_i, g, w_a, gq, gkv, w_uq, w_ukv, tables)


def _attn_body(*refs, pps, group, ql, klw, split, mode, tq, band, seq, use_sink, values_t):
    refs = list(refs)
    sink_ref = refs.pop(0) if use_sink else None
    q_ref = refs.pop(0)
    if mode != "ctx":
        kl_ref, vl_ref = refs.pop(0), refs.pop(0)
    kc_ref, vc_ref, o_ref = refs
    j = pl.program_id(1)
    qi = pl.program_id(2)
    lane = lax.broadcasted_iota(jnp.int32, (1, LANES), 1)
    in_a = (lane & 63) < 32

    def transpose_values(v):
        v_t = v.astype(F32).T
        tail = _ones_rows(v.shape[0])
        return [jnp.concatenate([v_t[h * DEN_ROW:(h + 1) * DEN_ROW], tail], axis=0).astype(BF16)
                for h in range(2)]

    ctx_len = kc_ref.shape[1]
    valid = None
    if mode == "dense":
        n_lat = seq
    elif mode == "window":
        start = jnp.clip(qi * tq - SWA_WINDOW, 0, seq - band)
        start = pl.multiple_of(start, LANES)
        kpos = start + lax.broadcasted_iota(jnp.int32, (band, 1), 0)
        qpos = qi * tq + lax.broadcasted_iota(jnp.int32, (1, tq), 1)
        valid = jnp.abs(qpos - kpos) <= SWA_WINDOW
        n_lat = band
    else:
        n_lat = 0

    chunks = []
    for pp in range(pps):
        ks = slice(pp * klw, (pp + 1) * klw)
        mine = []
        if values_t:
            head_rows = lambda half, pp=pp: slice((2 * pp + half) * VT_ROWS, (2 * pp + half + 1) * VT_ROWS)
            mine.append((lambda ks=ks: kc_ref[0, :, ks], lambda half, hr=head_rows: vc_ref[0, hr(half), :], None))
            for c0 in range(0, n_lat, KEY_CHUNK):
                c1 = min(c0 + KEY_CHUNK, n_lat)
                mine.append((lambda ks=ks, c0=c0, c1=c1: kl_ref[0, c0:c1, ks],
                             lambda half, hr=head_rows, c0=c0, c1=c1: vl_ref[0, hr(half), c0:c1], None))
        else:
            ps = slice(pp * LANES, (pp + 1) * LANES)
            vc = transpose_values(vc_ref[0, :, ps])
            mine.append((lambda ks=ks: kc_ref[0, :, ks], lambda half, vc=vc: vc[half], None))
            if mode != "ctx":
                vl = transpose_values(vl_ref[0, pl.ds(start, band), ps])
                for c0 in range(0, n_lat, KEY_CHUNK):
                    c1 = min(c0 + KEY_CHUNK, n_lat)
                    mine.append((lambda ks=ks, c0=c0, c1=c1: kl_ref[0, pl.ds(start + c0, c1 - c0), ks],
                                 lambda half, vl=vl, c0=c0, c1=c1: vl[half][:, c0:c1],
                                 valid[c0:c1]))
        chunks.append(mine)
    n_chunks = len(chunks[0])

    def head_q(pp, g, half):
        q = q_ref[0, :, (pp * group + g) * ql:(pp * group + g + 1) * ql]
        if split:
            return q[:, half * LANES:(half + 1) * LANES]
        return jnp.where(in_a if half == 0 else jnp.logical_not(in_a), q, jnp.zeros_like(q))

    def scores(item):
        pp, g, half, ci = item
        load_keys, _, ok = chunks[pp][ci]
        kch = load_keys()
        if split:
            kch = kch[:, half * LANES:(half + 1) * LANES]
        s = _dot_t(kch, head_q(pp, g, half))
        return s if ok is None else jnp.where(ok, s, NEG_INF)

    items = [(pp, g, half, ci) for pp in range(pps) for g in range(group) for half in range(2)
             for ci in range(n_chunks)]
    last = n_chunks - 1

    def run(lagged):
        outs = []
        acc = None
        bad = jnp.zeros((1, tq), F32)

        def values_step(pend, acc, bad):
            (pp, g, half, ci), p, scale, m_end, sink = pend
            pv = _dot(chunks[pp][ci][1](half), p)
            if ci == 0:
                acc = pv
            elif lagged:
                acc = (acc + pv) * scale
            else:
                acc = acc * scale + pv
            if ci == last:
                den = acc[DEN_ROW:DEN_ROW + 1, :]
                if sink is not None:
                    den = den + jnp.exp2(sink - m_end)
                out = acc[:DEN_ROW] * (1.0 / den)
                if lagged:
                    finite = (out - out) == 0.0
                    bad = bad + jnp.sum(jnp.where(finite, 0.0, 1.0), axis=0, keepdims=True)
                outs.append(out)
                if half == 1:
                    o_t = jnp.concatenate(outs, axis=0)
                    o_ref[0, :, (pp * group + g) * LANES:(pp * group + g + 1) * LANES] = o_t.T.astype(BF16)
                    outs.clear()
            return acc, bad

        s_next = scores(items[0])
        pending = None
        for idx, item in enumerate(items):
            pp, g, half, ci = item
            s = s_next
            if idx + 1 < len(items):
                s_next = scores(items[idx + 1])
            cm = jnp.max(s, axis=0, keepdims=True)
            if ci == 0:
                sink = sink_ref[(2 * (j * pps + pp) + half) * group + g] * LOG2E if use_sink else None
                m = cm if sink is None else jnp.maximum(cm, sink)
                p = jnp.exp2(s - m).astype(BF16)
                scale = None
            else:
                m_new = jnp.maximum(m, cm)
                scale = jnp.exp2(m - m_new)
                p = jnp.exp2(s - (m if lagged else m_new)).astype(BF16)
                m = m_new
            if pending is not None:
                acc, bad = values_step(pending, acc, bad)
            pending = (item, p, scale, m, sink)
        _, bad = values_step(pending, acc, bad)
        return jnp.max(bad)

    if n_chunks == 1:
        run(lagged=False)
        return
    overflowed = run(lagged=True)

    @pl.when(overflowed > 0.0)
    def _():
        run(lagged=False)


def _attention(q, k, v, n_pairs, group, split, window, sinks, need_ctx):
    nb, s, _ = q.shape
    b_lat = nb - 1
    ctx_len = s // b_lat
    ql = q.shape[2] // (n_pairs * group)
    klw = k.shape[2] // n_pairs
    ow = n_pairs * group * LANES
    tq = min(Q_TILE, s)
    band = min(tq + 2 * SWA_WINDOW, s)
    use_sink = sinks is not None
    pps = max(1, HEADS_PER_STEP // (2 * group))
    mode = "window" if window else "dense"
    values_t = not window
    common = dict(pps=pps, group=group, ql=ql, klw=klw, split=split, tq=tq, band=band, seq=s,
                  use_sink=use_sink, values_t=values_t)
    smem = [pl.BlockSpec(memory_space=pltpu.SMEM)] if use_sink else []
    sink_args = [sinks] if use_sink else []
    qw, kw, o_w = pps * group * ql, pps * klw, pps * group * LANES
    if values_t:
        vr = 2 * pps * VT_ROWS
        v_lat = pl.BlockSpec((1, vr, s), lambda b, j, i: (b, j, 0))
        v_ctx = pl.BlockSpec((1, vr, ctx_len), lambda b, j, i: (b_lat, j, b))
    else:
        v_lat = pl.BlockSpec((1, s, pps * LANES), lambda b, j, i: (b, 0, j))
        v_ctx = pl.BlockSpec((1, ctx_len, pps * LANES), lambda b, j, i: (b_lat, b, j))

    lat_specs = smem + [
        pl.BlockSpec((1, tq, qw), lambda b, j, i: (b, i, j)),
        pl.BlockSpec((1, s, kw), lambda b, j, i: (b, 0, j)),
        v_lat,
        pl.BlockSpec((1, ctx_len, kw), lambda b, j, i: (b_lat, b, j)),
        v_ctx,
    ]
    o_lat = pl.pallas_call(
        functools.partial(_attn_body, mode=mode, **common),
        grid=(b_lat, n_pairs // pps, s // tq),
        in_specs=lat_specs,
        out_specs=pl.BlockSpec((1, tq, o_w), lambda b, j, i: (b, i, j)),
        out_shape=jax.ShapeDtypeStruct((b_lat, s, ow), BF16),
        compiler_params=_cparams("arbitrary", "arbitrary", "arbitrary"),
        name="attn_latent",
    )(*sink_args, q, k, v, k, v)
    if not need_ctx:
        return o_lat, None

    common["tq"] = ctx_len
    ctx_specs = smem + [
        pl.BlockSpec((1, ctx_len, qw), lambda b, j, i: (b_lat, b, j)),
        pl.BlockSpec((1, ctx_len, kw), lambda b, j, i: (b_lat, b, j)),
        v_ctx,
    ]
    o_ctx = pl.pallas_call(
        functools.partial(_attn_body, mode="ctx", **common),
        grid=(b_lat, n_pairs // pps, 1),
        in_specs=ctx_specs,
        out_specs=pl.BlockSpec((1, ctx_len, o_w), lambda b, j, i: (0, b, j)),
        out_shape=jax.ShapeDtypeStruct((1, s, ow), BF16),
        compiler_params=_cparams("arbitrary", "arbitrary", "arbitrary"),
        name="attn_context",
    )(*sink_args, q, k, v)
    return o_lat, o_ctx


def _post_body(*refs, moe, b_lat, has_ctx):
    refs = list(refs)
    o_ref = refs.pop(0)
    oc_ref = refs.pop(0) if has_ctx else None
    if moe:
        x_ref, mod_ref, g_ref, wo_ref, rw_ref, rb_ref, xo_ref, h_ref, route_ref = refs
    else:
        x_ref, mod_ref, g_ref, wo_ref, wgu_ref, wd_ref, xo_ref = refs
    o = o_ref[0]
    if has_ctx:
        o = jnp.where(pl.program_id(0) == b_lat, oc_ref[0], o)
    y = _dot(o, wo_ref[...])
    x = x_ref[0] + mod_ref[0, 2:3, :] * y
    h = _norm_mod(x, g_ref[...], mod_ref[0, 3:4, :], mod_ref[0, 4:5, :])
    h_hi = h.astype(BF16)
    if not moe:
        acc = None
        for c in range(D_FF // FF_CHUNK):
            gate = _dot(h_hi, wgu_ref[:, c * FF_CHUNK:(c + 1) * FF_CHUNK])
            up = _dot(h_hi, wgu_ref[:, D_FF + c * FF_CHUNK:D_FF + (c + 1) * FF_CHUNK])
            act = (_silu(gate) * up).astype(BF16)
            part = _dot(act, wd_ref[c * FF_CHUNK:(c + 1) * FF_CHUNK, :])
            acc = part if acc is None else acc + part
        xo_ref[0] = x + mod_ref[0, 5:6, :] * acc
        return
    xo_ref[0] = x
    _store_rows_as_tiles(h_ref, h)
    h_lo = (h - h_hi.astype(F32)).astype(BF16)
    both = _dot(h_hi, rw_ref[...])
    logits = both[:, :LANES] + both[:, LANES:] + _dot(h_lo, rw_ref[:, :LANES]) + rb_ref[...]
    lane = lax.broadcasted_iota(jnp.int32, logits.shape, 1).astype(F32)
    lg = jnp.where(lane < N_EXPERTS, logits, -jnp.inf)
    v1 = jnp.max(lg, axis=-1, keepdims=True)
    i1 = jnp.min(jnp.where(lg == v1, lane, float(LANES)), axis=-1, keepdims=True)
    lg2 = jnp.where(lane == i1, -jnp.inf, lg)
    v2 = jnp.max(lg2, axis=-1, keepdims=True)
    i2 = jnp.min(jnp.where(lg2 == v2, lane, float(LANES)), axis=-1, keepdims=True)
    e = jnp.exp(v2 - v1)
    g1 = 1.0 / (1.0 + e)
    g2 = e / (1.0 + e)
    route_ref[0] = jnp.where(lane == 0, i1, jnp.where(lane == 1, i2, jnp.where(
        lane == 2, g1, jnp.where(lane == 3, g2, 0.0))))


def _post(o_lat, o_ctx, xs, mods_i, norm_g, w_o, nb, router=None, ffn=None):
    _, s, d = xs.shape
    tm = min(ROW_TILE, s)
    moe = router is not None
    assert moe != (ffn is not None)
    has_ctx = o_ctx is not None
    resident = lambda arr: pl.BlockSpec(arr.shape, lambda b, i: (0, 0), pipeline_mode=pl.Buffered(1))
    b_lat = o_lat.shape[0]
    in_specs = [pl.BlockSpec((1, tm, d), lambda b, i: (jnp.minimum(b, b_lat - 1), i, 0))]
    args = [o_lat]
    if has_ctx:
        in_specs.append(pl.BlockSpec((1, tm, d), lambda b, i: (0, i, 0)))
        args.append(o_ctx)
    n_o = len(args)
    in_specs += [
        pl.BlockSpec((1, tm, d), lambda b, i: (b, i, 0)),
        pl.BlockSpec((1, 6, d), lambda b, i: (b, 0, 0)),
        pl.BlockSpec((1, d), lambda b, i: (0, 0)),
        resident(w_o),
    ]
    args += [xs, mods_i, norm_g.reshape(1, d), w_o]
    out_specs = [pl.BlockSpec((1, tm, d), lambda b, i: (b, i, 0))]
    out_shape = [jax.ShapeDtypeStruct(xs.shape, F32)]
    if moe:
        in_specs += [pl.BlockSpec((d, 2 * LANES), lambda b, i: (0, 0)),
                     pl.BlockSpec((1, LANES), lambda b, i: (0, 0))]
        args += list(router)
        per_b = s // tm
        out_specs += [pl.BlockSpec((tm * ROW_SUB, LANES), lambda b, i: (b * per_b + i, 0)),
                      pl.BlockSpec((1, tm, LANES), lambda b, i: (b, i, 0))]
        out_shape += [jax.ShapeDtypeStruct((nb * s * ROW_SUB, LANES), F32),
                      jax.ShapeDtypeStruct((nb, s, LANES), F32)]
    else:
        layer = ffn[2]
        in_specs += [pl.BlockSpec((None,) + w.shape[1:], lambda b, i: (layer, 0, 0), pipeline_mode=pl.Buffered(1))
                     for w in ffn[:2]]
        args += list(ffn[:2])
    return pl.pallas_call(
        functools.partial(_post_body, moe=moe, b_lat=b_lat, has_ctx=has_ctx),
        grid=(nb, s // tm),
        in_specs=in_specs,
        out_specs=out_specs,
        out_shape=out_shape,
        input_output_aliases={n_o: 0},
        compiler_params=_cparams("arbitrary", "arbitrary"),
        name="attn_out_router" if moe else "attn_out_ffn",
    )(*args)


ROW_SUB = D_MODEL // LANES


def _store_rows_as_tiles(ref, x):
    n = x.shape[0]
    for sub in range(ROW_SUB):
        ref[pl.ds(sub, n, stride=ROW_SUB), :] = x[:, sub * LANES:(sub + 1) * LANES]


def _load_row_pieces(ref, first_tile, n):
    return [ref[pl.ds(first_tile * ROW_SUB + sub, n, stride=ROW_SUB), :] for sub in range(ROW_SUB)]


def _tile(ref, row8):
    return ref.at[pl.ds(pl.multiple_of(row8, ROW_SUB), ROW_SUB)]


def _tile_dma_wait(src_ref, dst_ref, sem, tiles):
    n = tiles * ROW_SUB
    pltpu.make_async_copy(src_ref.at[pl.ds(0, n)], dst_ref.at[pl.ds(0, n)], sem).wait()


def _dispatch_body(pos_ref, h_ref, xs_in_ref, xs_ref, sem):
    del xs_in_ref
    tm = h_ref.shape[0] // ROW_SUB

    def issue(r, carry):
        src = _tile(h_ref, r * ROW_SUB)
        for k in range(2):
            pltpu.make_async_copy(src, _tile(xs_ref, pos_ref[0, 0, 2 * r + k]), sem).start()
        return carry

    lax.fori_loop(0, tm, issue, 0, unroll=8)
    _tile_dma_wait(h_ref, xs_ref, sem, tm)
    _tile_dma_wait(h_ref, xs_ref, sem, tm)


def _moe_dispatch(h_tiles, pos8, p, tm):
    t = h_tiles.shape[0] // ROW_SUB
    return pl.pallas_call(
        _dispatch_body,
        grid=(t // tm,),
        in_specs=[
            pl.BlockSpec((1, 1, 2 * tm), lambda i: (i, 0, 0), memory_space=pltpu.SMEM),
            pl.BlockSpec((tm * ROW_SUB, LANES), lambda i: (i, 0)),
            pl.BlockSpec(memory_space=pl.ANY),
        ],
        out_specs=pl.BlockSpec(memory_space=pl.ANY),
        out_shape=jax.ShapeDtypeStruct((p * ROW_SUB, LANES), F32),
        scratch_shapes=[pltpu.SemaphoreType.DMA(())],
        input_output_aliases={2: 0},
        compiler_params=_cparams("arbitrary"),
        name="moe_dispatch",
    )(pos8.reshape(t // tm, 1, 2 * tm), h_tiles, jnp.zeros((p * ROW_SUB, LANES), F32))


def _moe_body(te_ref, nu_ref, xs_ref, wg_ref, wu_ref, wd_ref, ys_ref):
    i = pl.program_id(0)
    tm = xs_ref.shape[0] // ROW_SUB

    @pl.when(i < nu_ref[0])
    def _():
        x = jnp.concatenate(_load_row_pieces(xs_ref, 0, tm), axis=1).astype(BF16)
        act = (_silu(_dot(x, wg_ref[0])) * _dot(x, wu_ref[0])).astype(BF16)
        _store_rows_as_tiles(ys_ref, _dot(act, wd_ref[0]))

    @pl.when(i >= nu_ref[0])
    def _():
        ys_ref[...] = jnp.zeros_like(ys_ref)


def _moe_experts(xs_sorted, tile_expert, n_used, w_gu, w_d, layer, tm):
    p = xs_sorted.shape[0] // ROW_SUB
    f, d = w_d.shape[2:]
    rows = pl.BlockSpec((tm * ROW_SUB, LANES), lambda i, te, nu: (i, 0))
    grid_spec = pltpu.PrefetchScalarGridSpec(
        num_scalar_prefetch=2,
        grid=(p // tm,),
        in_specs=[
            rows,
            pl.BlockSpec((None, 1, d, f), lambda i, te, nu: (layer, te[i], 0, 0)),
            pl.BlockSpec((None, 1, d, f), lambda i, te, nu: (layer, te[i], 0, 1)),
            pl.BlockSpec((None, 1, f, d), lambda i, te, nu: (layer, te[i], 0, 0)),
        ],
        out_specs=rows,
    )
    return pl.pallas_call(
        _moe_body,
        grid_spec=grid_spec,
        out_shape=jax.ShapeDtypeStruct((p * ROW_SUB, LANES), F32),
        compiler_params=_cparams("arbitrary"),
        name="moe_experts",
    )(tile_expert, n_used, xs_sorted, w_gu, w_gu, w_d)


def _combine_body(*refs, final):
    if final:
        pos_ref, pos_next_ref, x_ref, route_ref, mod_ref, g_ref, ys_ref, o_ref, ybuf, sems = refs
    else:
        pos_ref, pos_next_ref, x_ref, route_ref, mod_ref, ys_ref, o_ref, ybuf, sems = refs
    tm, d = x_ref.shape[1:]
    step = pl.program_id(0) * pl.num_programs(1) + pl.program_id(1)
    n_steps = pl.num_programs(0) * pl.num_programs(1)
    slot = lax.rem(step, 2)

    def gather(p_ref, into):
        def issue(r, carry):
            for k in range(2):
                pltpu.make_async_copy(_tile(ys_ref, p_ref[0, 0, 2 * r + k]),
                                      _tile(ybuf.at[into], (k * tm + r) * ROW_SUB), sems.at[into]).start()
            return carry

        lax.fori_loop(0, tm, issue, 0, unroll=8)

    @pl.when(step == 0)
    def _():
        gather(pos_ref, 0)

    @pl.when(step + 1 < n_steps)
    def _():
        gather(pos_next_ref, 1 - slot)

    mine = ybuf.at[slot]
    _tile_dma_wait(ys_ref, mine, sems.at[slot], 2 * tm)
    route = route_ref[0]
    g1, g2 = route[:, 2:3], route[:, 3:4]
    y1, y2 = _load_row_pieces(mine, 0, tm), _load_row_pieces(mine, tm, tm)
    cols = [slice(sub * LANES, (sub + 1) * LANES) for sub in range(ROW_SUB)]
    xn = [x_ref[0, :, c] + mod_ref[0, 5:6, c] * (g1 * a + g2 * b) for c, a, b in zip(cols, y1, y2)]
    if final:
        ms = sum(jnp.sum(v * v, axis=-1, keepdims=True) for v in xn) * (1.0 / d)
        scale = lax.rsqrt(ms + NORM_EPS)
        xn = [v * scale * g_ref[:, c] for c, v in zip(cols, xn)]
    for c, v in zip(cols, xn):
        o_ref[0, :, c] = v


def _moe_combine(xs, ys, pos, route, mods_i, nb, final_g=None):
    _, s, d = xs.shape
    tm = min(ROW_TILE, s)
    per_b = s // tm
    final = final_g is not None
    tok = pl.BlockSpec((1, tm, d), lambda b, i: (b, i, 0))
    last = nb * per_b - 1
    pos3 = pos.reshape(nb * per_b, 1, 2 * tm)
    in_specs = [pl.BlockSpec((1, 1, 2 * tm), lambda b, i: (b * per_b + i, 0, 0), memory_space=pltpu.SMEM),
                pl.BlockSpec((1, 1, 2 * tm), lambda b, i: (jnp.minimum(b * per_b + i + 1, last), 0, 0),
                             memory_space=pltpu.SMEM),
                tok,
                pl.BlockSpec((1, tm, LANES), lambda b, i: (b, i, 0)),
                pl.BlockSpec((1, 6, d), lambda b, i: (b, 0, 0))]
    args = [pos3, pos3, xs, route, mods_i]
    if final:
        in_specs.append(pl.BlockSpec((1, d), lambda b, i: (0, 0)))
        args.append(final_g.reshape(1, d))
    in_specs.append(pl.BlockSpec(memory_space=pl.ANY))
    args.append(ys)
    return pl.pallas_call(
        functools.partial(_combine_body, final=final),
        grid=(nb, per_b),
        in_specs=in_specs,
        out_specs=tok,
        out_shape=jax.ShapeDtypeStruct((nb, s, d) if final else xs.shape, F32),
        scratch_shapes=[pltpu.VMEM((2, 2 * tm * ROW_SUB, LANES), F32), pltpu.SemaphoreType.DMA((2,))],
        input_output_aliases={} if final else {2: 0},
        compiler_params=_cparams("arbitrary", "arbitrary"),
        name="moe_combine",
    )(*args)


def _moe_layer(hp, route, xs, mods_i, w_gu, w_d, layer, nb, final_g):
    _, s, d = xs.shape
    t = nb * s
    tm = min(MOE_TILE, s)
    p = 2 * t + N_EXPERTS * tm
    expert = route.reshape(t, LANES)[:, :2].astype(jnp.int32).reshape(2 * t)
    onehot = (expert[:, None] == jnp.arange(N_EXPERTS, dtype=jnp.int32)[None, :]).astype(jnp.int32)
    csum = jnp.cumsum(onehot, axis=0)
    padded = ((csum[-1] + tm - 1) // tm) * tm
    off_end = jnp.cumsum(padded)
    pos = jnp.sum(onehot * ((off_end - padded)[None, :] + csum - 1), axis=1)
    tile_start = jnp.arange(p // tm, dtype=jnp.int32) * tm
    tile_expert = jnp.minimum(jnp.sum(tile_start[:, None] >= off_end[None, :], axis=1), N_EXPERTS - 1)
    n_used = (off_end[-1] // tm).reshape(1).astype(jnp.int32)

    pos8 = pos * ROW_SUB
    xs_sorted = _moe_dispatch(hp, pos8, p, tm)
    ys = _moe_experts(xs_sorted, tile_expert.astype(jnp.int32), n_used, w_gu, w_d, layer, tm)
    return _moe_combine(xs, ys, pos8, route, mods_i, nb, final_g)


def kernel(x, c, ctx, c_ctx, ada_w, ada_b, norm_g, final_norm_g, mla_w_dq, mla_q_norm_g, mla_w_uq, mla_w_dkv, mla_kv_norm_g, mla_w_ukv, mla_w_o, swa_w_qkv, swa_sinks, swa_w_o, ga_w_qkv, ga_q_norm_g, ga_k_norm_g, ga_w_o, ffn_w_gate_up, ffn_w_down, moe_router_w, moe_router_b, moe_w_gate_up, moe_w_down):
    b, s, d = x.shape
    assert d == D_MODEL and b * ctx.shape[1] == s and s % GRID_W == 0
    nb = b + 1
    xs = jnp.concatenate([x, ctx.reshape(1, s, d)], axis=0)
    c_all = jnp.concatenate([c, c_ctx[None], jnp.zeros((16 - nb, d), F32)], axis=0)
    mods = _mods(c_all, ada_w, ada_b).reshape(DEPTH, 16, 6, d)

    ffn_gu, ffn_dn = ffn_w_gate_up.astype(BF16), ffn_w_down.astype(BF16)
    moe_gu, moe_dn = moe_w_gate_up.astype(BF16), moe_w_down.astype(BF16)
    gqa_tab = _gqa_tables(s)
    mla_tab = _mla_tables(s)
    nope_src, rope_src = _mla_lane_src()

    for i in range(DEPTH):
        need_ctx = i < DEPTH - 1
        mods_i = mods[i]
        kind, j = i % 3, i // 3
        if kind == 0:
            dkv = jnp.concatenate([mla_w_dkv[j], jnp.zeros((d, 1), F32)], axis=1)
            kr_cols = np.where(rope_src >= 0, MLA_KV_LORA + rope_src, MLA_KV_LORA + MLA_QK_ROPE)
            w_a = jnp.concatenate([mla_w_dq[j], dkv[:, :MLA_KV_LORA], dkv[:, kr_cols]], axis=1).astype(BF16)
            qd = MLA_QK_NOPE + MLA_QK_ROPE
            q_src = np.where(nope_src >= 0, nope_src, np.where(rope_src >= 0, MLA_QK_NOPE + rope_src, qd))
            uq = jnp.concatenate([mla_w_uq[j].reshape(MLA_Q_LORA, N_HEADS, qd),
                                  jnp.zeros((MLA_Q_LORA, N_HEADS, 1), F32)], axis=2)
            w_uq = uq[:, :, q_src].reshape(MLA_Q_LORA, N_HEADS * LANES).astype(BF16)
            kvd = MLA_QK_NOPE + MLA_V_DIM
            ukv = jnp.concatenate([mla_w_ukv[j].reshape(MLA_KV_LORA, N_HEADS, kvd),
                                   jnp.zeros((MLA_KV_LORA, N_HEADS, 1), F32)], axis=2)
            k_src = np.where(nope_src >= 0, nope_src, kvd)
            w_ukv = jnp.concatenate([ukv[:, :, k_src].reshape(MLA_KV_LORA, N_HEADS * LANES),
                                     ukv[:, :, MLA_QK_NOPE:kvd].reshape(MLA_KV_LORA, N_HEADS * MLA_V_DIM)],
                                    axis=1).astype(BF16)
            q, k, v = _mla_proj(xs, mods_i, norm_g[i, 0], w_a, mla_q_norm_g[j], mla_kv_norm_g[j],
                                w_uq, w_ukv, mla_tab)
            o = _attention(q, k, v, N_HEADS // 2, 1, True, False, None, need_ctx)
            w_o = mla_w_o[j].astype(BF16)
        elif kind == 1:
            cols, o_rows = _gqa_perm(SWA_KV_HEADS, N_HEADS // SWA_KV_HEADS)
            q, k, v = _gqa_proj(xs, mods_i, norm_g[i, 0], swa_w_qkv[j][:, cols].astype(BF16), gqa_tab,
                                SWA_KV_HEADS, values_t=False)
            o = _attention(q, k, v, SWA_KV_HEADS // 2, N_HEADS // SWA_KV_HEADS, False, True,
                           swa_sinks[j], need_ctx)
            w_o = swa_w_o[j][o_rows].astype(BF16)
        else:
            cols, o_rows = _gqa_perm(GA_KV_HEADS, N_HEADS // GA_KV_HEADS)
            q, k, v = _gqa_proj(xs, mods_i, norm_g[i, 0], ga_w_qkv[j][:, cols].astype(BF16), gqa_tab,
                                GA_KV_HEADS, (ga_q_norm_g[j], ga_k_norm_g[j]))
            o = _attention(q, k, v, GA_KV_HEADS // 2, N_HEADS // GA_KV_HEADS, False, False, None, need_ctx)
            w_o = ga_w_o[j][o_rows].astype(BF16)

        n_tok_b = nb if need_ctx else b
        f = i // 2
        if i % 2 == 0:
            xs, = _post(*o, xs, mods_i, norm_g[i, 1], w_o, n_tok_b, ffn=(ffn_gu, ffn_dn, f))
        else:
            rw = jnp.concatenate([moe_router_w[f], jnp.zeros((d, LANES - N_EXPERTS), F32)], axis=1)
            rw_hi = rw.astype(BF16)
            rw_lo = (rw - rw_hi.astype(F32)).astype(BF16)
            rb = jnp.concatenate([moe_router_b[f], jnp.zeros((LANES - N_EXPERTS,), F32)]).reshape(1, LANES)
            xs, h2, comb = _post(*o, xs, mods_i, norm_g[i, 1], w_o, n_tok_b,
                                 (jnp.concatenate([rw_hi, rw_lo], axis=1), rb))
            xs = _moe_layer(h2, comb, xs, mods_i, moe_gu, moe_dn, f, n_tok_b,
                            None if need_ctx else final_norm_g)
    return xs
```

```python
import functools
import math

import numpy as np
import jax
import jax.numpy as jnp
from jax import lax
from jax.experimental import pallas as pl
from jax.experimental.pallas import tpu as pltpu

F32 = jnp.float32
BF16 = jnp.bfloat16

D_MODEL = 1024
GRID_W = 64
HEAD_DIM = 64
N_HEADS = 16
MLA_Q_LORA = 384
MLA_KV_LORA = 256
MLA_QK_NOPE = 64
MLA_QK_ROPE = 32
MLA_V_DIM = 64
SWA_KV_HEADS = 4
SWA_WINDOW = 128
GA_KV_HEADS = 8
D_FF = 2816
N_EXPERTS = 8
EXPERT_FF = 1408
ROPE_THETA = 10000.0
NORM_EPS = 1e-6
NEG_INF = -1e30
DEPTH = 4
LOG2E = math.log2(math.e)

LANES = 128
VMEM_LIMIT = 56 * 2**20
ROW_TILE = 512
Q_TILE = 512
KEY_CHUNK = 1024
HEADS_PER_STEP = 8
MOE_TILE = 512
FF_CHUNK = 1408


def _cparams(*sem):
    return pltpu.CompilerParams(dimension_semantics=sem, vmem_limit_bytes=VMEM_LIMIT)


def _silu(x):
    return x * (1.0 / (1.0 + jnp.exp(-x)))


def _rms(x, g):
    ms = jnp.mean(x * x, axis=-1, keepdims=True)
    return x * lax.rsqrt(ms + NORM_EPS) * g


def _norm_mod(x, g, shift, scale):
    return _rms(x, g) * (1.0 + scale) + shift


def _dot(a, b):
    return jnp.dot(a, b, preferred_element_type=F32)


def _dot_t(a, b):
    return lax.dot_general(a, b, (((1,), (1,)), ((), ())), preferred_element_type=F32)


def _mods_body(c_ref, w_ref, b_ref, o_ref):
    sc = _silu(c_ref[...]).astype(BF16)
    o_ref[0] = _dot(sc, w_ref[0].astype(BF16)) + b_ref[0]


def _mods(c_all, ada_w, ada_b):
    depth, d, n = ada_w.shape
    rows = c_all.shape[0]
    tn = 1536
    return pl.pallas_call(
        _mods_body,
        grid=(depth, n // tn),
        in_specs=[
            pl.BlockSpec((rows, d), lambda i, j: (0, 0)),
            pl.BlockSpec((1, d, tn), lambda i, j: (i, 0, j)),
            pl.BlockSpec((1, 1, tn), lambda i, j: (i, 0, j)),
        ],
        out_specs=pl.BlockSpec((1, rows, tn), lambda i, j: (i, 0, j)),
        out_shape=jax.ShapeDtypeStruct((depth, rows, n), F32),
        compiler_params=_cparams("arbitrary", "arbitrary"),
        name="adaln_mods",
    )(c_all, ada_w, ada_b.reshape(depth, 1, n))


def _pair_lanes():
    lane = np.arange(LANES)
    is_b = (lane % 64) >= 32
    dim = (lane % 32) + 32 * (lane // 64)
    return is_b, dim


def _gqa_perm(n_kv, group):
    is_b, dim = _pair_lanes()
    nq = N_HEADS * HEAD_DIM
    nk = n_kv * HEAD_DIM
    q_cols, k_cols, o_rows = [], [], []
    nat = np.arange(HEAD_DIM)
    for j in range(n_kv // 2):
        for g in range(group):
            head_a, head_b = (2 * j) * group + g, (2 * j + 1) * group + g
            q_cols.append(np.where(is_b, head_b, head_a) * HEAD_DIM + dim)
            o_rows.append(np.concatenate([head_a * HEAD_DIM + nat, head_b * HEAD_DIM + nat]))
        k_cols.append(nq + np.where(is_b, 2 * j + 1, 2 * j) * HEAD_DIM + dim)
    cols = np.concatenate(q_cols + k_cols + [nq + nk + np.arange(nk)])
    return cols.astype(np.int32), np.concatenate(o_rows).astype(np.int32)


def _angles(s, rot_dim):
    n_freq = rot_dim // 4
    inv = ROPE_THETA ** (-jnp.arange(n_freq, dtype=F32) / n_freq)
    pos = jnp.arange(s)
    rows = (pos // GRID_W).astype(F32)
    cols = (pos % GRID_W).astype(F32)
    return jnp.concatenate([rows[:, None] * inv, cols[:, None] * inv], axis=-1)


def _rope_tables(cos_l, sin_l, q_scale):
    lat = jnp.stack([cos_l * q_scale, sin_l * q_scale, cos_l, sin_l])
    one = jnp.ones_like(cos_l)
    zero = jnp.zeros_like(cos_l)
    ctx = jnp.stack([one * q_scale, zero, one, zero])
    return jnp.stack([lat, ctx])


def _gqa_tables(s):
    ang = _angles(s, HEAD_DIM)
    lane = np.arange(LANES)
    idx = lane % 32
    sign = jnp.asarray(np.where(lane < 64, -1.0, 1.0), F32)
    return _rope_tables(jnp.cos(ang)[:, idx], jnp.sin(ang)[:, idx] * sign, HEAD_DIM ** -0.5 * LOG2E)


def _mla_lane_src():
    lane = np.arange(LANES)
    nope = np.where((lane >= 16) & (lane < 64), lane - 16,
                    np.where((lane >= 80) & (lane < 96), 48 + lane - 80, -1))
    rope = np.where(lane < 16, lane, np.where((lane >= 64) & (lane < 80), 16 + lane - 64, -1))
    return nope, rope


def _mla_tables(s):
    ang = _angles(s, MLA_QK_ROPE)
    lane = np.arange(LANES)
    is_x1 = lane < 16
    is_x2 = (lane >= 64) & (lane < 80)
    idx = np.where(is_x1, lane, np.where(is_x2, lane - 64, 0))
    rot = jnp.asarray(is_x1 | is_x2)
    sign = jnp.asarray(np.where(is_x1, -1.0, np.where(is_x2, 1.0, 0.0)), F32)
    cos_l = jnp.where(rot, jnp.cos(ang)[:, idx], 1.0)
    sin_l = jnp.sin(ang)[:, idx] * sign
    return _rope_tables(cos_l, sin_l, (MLA_QK_NOPE + MLA_QK_ROPE) ** -0.5 * LOG2E)


def _rope(blk, cos, sin):
    return blk * cos + pltpu.roll(blk, 64, 1) * sin


VT_ROWS = 80
DEN_ROW = 64


def _ones_rows(n_cols):
    r = lax.broadcasted_iota(jnp.int32, (VT_ROWS - DEN_ROW, n_cols), 0)
    return jnp.where(r == 0, 1.0, 0.0)


def _store_values_t(vt_ref, v, n_heads):
    v_t = v.T
    tail = _ones_rows(v.shape[0]).astype(vt_ref.dtype)
    for h in range(n_heads):
        vt_ref[0, h * VT_ROWS:h * VT_ROWS + DEN_ROW, :] = v_t[h * DEN_ROW:(h + 1) * DEN_ROW, :].astype(vt_ref.dtype)
        vt_ref[0, h * VT_ROWS + DEN_ROW:(h + 1) * VT_ROWS, :] = tail


def _gqa_proj_body(*refs, nqb, nkb, qk_norm, values_t):
    if qk_norm:
        x_ref, mod_ref, g_ref, w_ref, tab_ref, gq_ref, gk_ref, ind_ref, q_ref, k_ref, v_ref = refs
    else:
        x_ref, mod_ref, g_ref, w_ref, tab_ref, q_ref, k_ref, v_ref = refs
    h = _norm_mod(x_ref[0], g_ref[...], mod_ref[0, 0:1, :], mod_ref[0, 1:2, :]).astype(BF16)
    qkv = _dot(h, w_ref[...])
    cq, sq, ck, sk = tab_ref[0, 0], tab_ref[0, 1], tab_ref[0, 2], tab_ref[0, 3]

    def head_norm(blk, gain):
        ssq = _dot((blk * blk).astype(BF16), ind_ref[...])
        return blk * lax.rsqrt(ssq * (1.0 / HEAD_DIM) + NORM_EPS) * gain

    for c in range(nqb):
        blk = qkv[:, c * LANES:(c + 1) * LANES]
        if qk_norm:
            blk = head_norm(blk, gq_ref[...])
        q_ref[0, :, c * LANES:(c + 1) * LANES] = _rope(blk, cq, sq).astype(BF16)
    for c in range(nkb):
        blk = qkv[:, (nqb + c) * LANES:(nqb + c + 1) * LANES]
        if qk_norm:
            blk = head_norm(blk, gk_ref[...])
        k_ref[0, :, c * LANES:(c + 1) * LANES] = _rope(blk, ck, sk).astype(BF16)
    v = qkv[:, (nqb + nkb) * LANES:]
    if values_t:
        _store_values_t(v_ref, v, 2 * nkb)
    else:
        v_ref[0] = v.astype(BF16)


def _gqa_proj(xs, mods_i, norm_g, w_perm, tables, n_kv, qk_gains=None, values_t=True):
    nb, s, d = xs.shape
    tm = min(ROW_TILE, s)
    nqb = N_HEADS * HEAD_DIM // LANES
    nkb = n_kv * HEAD_DIM // LANES
    wn = w_perm.shape[1]
    in_specs = [
        pl.BlockSpec((1, tm, d), lambda b, i: (b, i, 0)),
        pl.BlockSpec((1, 6, d), lambda b, i: (b, 0, 0)),
        pl.BlockSpec((1, d), lambda b, i: (0, 0)),
        pl.BlockSpec((d, wn), lambda b, i: (0, 0)),
        pl.BlockSpec((1, 4, tm, LANES), lambda b, i: (b // (nb - 1), 0, i, 0)),
    ]
    args = [xs, mods_i, norm_g.reshape(1, d), w_perm, tables]
    if qk_gains is not None:
        is_b, dim = _pair_lanes()
        ind = jnp.asarray(is_b[:, None] == is_b[None, :], BF16)
        in_specs += [pl.BlockSpec((1, LANES), lambda b, i: (0, 0)),
                     pl.BlockSpec((1, LANES), lambda b, i: (0, 0)),
                     pl.BlockSpec((LANES, LANES), lambda b, i: (0, 0))]
        args += [qk_gains[0][dim].reshape(1, LANES), qk_gains[1][dim].reshape(1, LANES), ind]
    out_w = (nqb * LANES, nkb * LANES, nkb * LANES)
    out_specs = [pl.BlockSpec((1, tm, w), lambda b, i: (b, i, 0)) for w in out_w]
    out_shape = [jax.ShapeDtypeStruct((nb, s, w), BF16) for w in out_w]
    if values_t:
        out_specs[2] = pl.BlockSpec((1, n_kv * VT_ROWS, tm), lambda b, i: (b, 0, i))
        out_shape[2] = jax.ShapeDtypeStruct((nb, n_kv * VT_ROWS, s), BF16)
    return pl.pallas_call(
        functools.partial(_gqa_proj_body, nqb=nqb, nkb=nkb, qk_norm=qk_gains is not None, values_t=values_t),
        grid=(nb, s // tm),
        in_specs=in_specs,
        out_specs=out_specs,
        out_shape=out_shape,
        compiler_params=_cparams("arbitrary", "arbitrary"),
        name="gqa_proj",
    )(*args)


def _split_specs(lat, tm):
    b_lat, s, w = lat.shape
    last_i = s // tm - 1
    return (pl.BlockSpec((1, tm, w), lambda b, i: (jnp.minimum(b, b_lat - 1), jnp.where(b < b_lat, i, last_i), 0)),
            pl.BlockSpec((1, tm, w), lambda b, i: (0, jnp.where(b < b_lat, 0, i), 0)))


def _pick_stream(lat_ref, ctx_ref, b_lat):
    return jnp.where(pl.program_id(0) == b_lat, ctx_ref[0], lat_ref[0])


def _mla_proj_body(*refs, split_x):
    refs = list(refs)
    x_ref = refs.pop(0)
    x = _pick_stream(x_ref, refs.pop(0), pl.num_programs(0) - 1) if split_x else x_ref[0]
    mod_ref, g_ref, wa_ref, gq_ref, gkv_ref, wuq_ref, wukv_ref, tab_ref, q_ref, k_ref, v_ref = refs
    h = _norm_mod(x, g_ref[...], mod_ref[0, 0:1, :], mod_ref[0, 1:2, :]).astype(BF16)
    a = _dot(h, wa_ref[...])
    qn = _rms(a[:, :MLA_Q_LORA], gq_ref[...]).astype(BF16)
    cn = _rms(a[:, MLA_Q_LORA:MLA_Q_LORA + MLA_KV_LORA], gkv_ref[...]).astype(BF16)
    kr = a[:, MLA_Q_LORA + MLA_KV_LORA:]
    cq, sq, ck, sk = tab_ref[0, 0], tab_ref[0, 1], tab_ref[0, 2], tab_ref[0, 3]
    q = _dot(qn, wuq_ref[...])
    kv = _dot(cn, wukv_ref[...])
    kr = _rope(kr, ck, sk)
    for hh in range(N_HEADS):
        sl = slice(hh * LANES, (hh + 1) * LANES)
        q_ref[0, :, sl] = _rope(q[:, sl], cq, sq).astype(BF16)
        k_ref[0, :, sl] = (kv[:, sl] + kr).astype(BF16)
    _store_values_t(v_ref, kv[:, N_HEADS * LANES:], N_HEADS)


def _mla_proj(xs, mods_i, norm_g, w_a, gq, gkv, w_uq, w_ukv, tables):
    split_x = isinstance(xs, tuple)
    if split_x:
        (b_lat, s, d), nb = xs[0].shape, xs[0].shape[0] + 1
    else:
        nb, s, d = xs.shape
    tm = min(ROW_TILE, s)
    full = lambda arr: pl.BlockSpec(arr.shape, lambda b, i: (0,) * arr.ndim)
    gq = gq.reshape(1, -1)
    gkv = gkv.reshape(1, -1)
    g = norm_g.reshape(1, d)
    qk_w = N_HEADS * LANES
    vt_rows = N_HEADS * VT_ROWS
    x_specs = list(_split_specs(xs[0], tm)) if split_x else [pl.BlockSpec((1, tm, d), lambda b, i: (b, i, 0))]
    x_args = list(xs) if split_x else [xs]
    return pl.pallas_call(
        functools.partial(_mla_proj_body, split_x=split_x),
        grid=(nb, s // tm),
        in_specs=x_specs + [
            pl.BlockSpec((1, 6, d), lambda b, i: (b, 0, 0)),
            full(g), full(w_a), full(gq), full(gkv), full(w_uq), full(w_ukv),
            pl.BlockSpec((1, 4, tm, LANES), lambda b, i: (b // (nb - 1), 0, i, 0)),
        ],
        out_specs=[pl.BlockSpec((1, tm, qk_w), lambda b, i: (b, i, 0)),
                   pl.BlockSpec((1, tm, qk_w), lambda b, i: (b, i, 0)),
                   pl.BlockSpec((1, vt_rows, tm), lambda b, i: (b, 0, i))],
        out_shape=[jax.ShapeDtypeStruct((nb, s, qk_w), BF16), jax.ShapeDtypeStruct((nb, s, qk_w), BF16),
                   jax.ShapeDtypeStruct((nb, vt_rows, s), BF16)],
        compiler_params=_cparams("arbitrary", "arbitrary"),
        name="mla_proj",
    )(*x_args, mods_i, g, w_a, gq, gkv, w_uq, w_ukv, tables)


def _attn_body(*refs, pps, group, ql, klw, split, mode, tq, band, seq, use_sink, values_t):
    refs = list(refs)
    sink_ref = refs.pop(0) if use_sink else None
    q_ref = refs.pop(0)
    if mode != "ctx":
        kl_ref, vl_ref = refs.pop(0), refs.pop(0)
    kc_ref, vc_ref, o_ref = refs
    j = pl.program_id(1)
    qi = pl.program_id(2)
    lane = lax.broadcasted_iota(jnp.int32, (1, LANES), 1)
    in_a = (lane & 63) < 32

    def transpose_values(v):
        v_t = v.astype(F32).T
        tail = _ones_rows(v.shape[0])
        return [jnp.concatenate([v_t[h * DEN_ROW:(h + 1) * DEN_ROW], tail], axis=0).astype(BF16)
                for h in range(2)]

    ctx_len = kc_ref.shape[1]
    valid = None
    if mode == "dense":
        n_lat = seq
    elif mode == "window":
        start = jnp.clip(qi * tq - SWA_WINDOW, 0, seq - band)
        start = pl.multiple_of(start, LANES)
        kpos = start + lax.broadcasted_iota(jnp.int32, (band, 1), 0)
        qpos = qi * tq + lax.broadcasted_iota(jnp.int32, (1, tq), 1)
        valid = jnp.abs(qpos - kpos) <= SWA_WINDOW
        n_lat = band
    else:
        n_lat = 0

    chunks = []
    for pp in range(pps):
        ks = slice(pp * klw, (pp + 1) * klw)
        mine = []
        if values_t:
            head_rows = lambda half, pp=pp: slice((2 * pp + half) * VT_ROWS, (2 * pp + half + 1) * VT_ROWS)
            mine.append((lambda ks=ks: kc_ref[0, :, ks], lambda half, hr=head_rows: vc_ref[0, hr(half), :], None))
            for c0 in range(0, n_lat, KEY_CHUNK):
                c1 = min(c0 + KEY_CHUNK, n_lat)
                mine.append((lambda ks=ks, c0=c0, c1=c1: kl_ref[0, c0:c1, ks],
                             lambda half, hr=head_rows, c0=c0, c1=c1: vl_ref[0, hr(half), c0:c1], None))
        else:
            ps = slice(pp * LANES, (pp + 1) * LANES)
            vc = transpose_values(vc_ref[0, :, ps])
            mine.append((lambda ks=ks: kc_ref[0, :, ks], lambda half, vc=vc: vc[half], None))
            if mode != "ctx":
                vl = transpose_values(vl_ref[0, pl.ds(start, band), ps])
                for c0 in range(0, n_lat, KEY_CHUNK):
                    c1 = min(c0 + KEY_CHUNK, n_lat)
                    mine.append((lambda ks=ks, c0=c0, c1=c1: kl_ref[0, pl.ds(start + c0, c1 - c0), ks],
                                 lambda half, vl=vl, c0=c0, c1=c1: vl[half][:, c0:c1],
                                 valid[c0:c1]))
        chunks.append(mine)
    n_chunks = len(chunks[0])

    def head_q(pp, g, half):
        q = q_ref[0, :, (pp * group + g) * ql:(pp * group + g + 1) * ql]
        if split:
            return q[:, half * LANES:(half + 1) * LANES]
        return jnp.where(in_a if half == 0 else jnp.logical_not(in_a), q, jnp.zeros_like(q))

    def scores(item):
        pp, g, half, ci = item
        load_keys, _, ok = chunks[pp][ci]
        kch = load_keys()
        if split:
            kch = kch[:, half * LANES:(half + 1) * LANES]
        s = _dot_t(kch, head_q(pp, g, half))
        return s if ok is None else jnp.where(ok, s, NEG_INF)

    items = [(pp, g, half, ci) for pp in range(pps) for g in range(group) for half in range(2)
             for ci in range(n_chunks)]
    last = n_chunks - 1

    def run(lagged):
        outs = []
        acc = None
        bad = jnp.zeros((1, tq), F32)

        def values_step(pend, acc, bad):
            (pp, g, half, ci), p, scale, m_end, sink = pend
            pv = _dot(chunks[pp][ci][1](half), p)
            if ci == 0:
                acc = pv
            elif lagged:
                acc = (acc + pv) * scale
            else:
                acc = acc * scale + pv
            if ci == last:
                den = acc[DEN_ROW:DEN_ROW + 1, :]
                if sink is not None:
                    den = den + jnp.exp2(sink - m_end)
                out = acc[:DEN_ROW] * (1.0 / den)
                if lagged:
                    finite = (out - out) == 0.0
                    bad = bad + jnp.sum(jnp.where(finite, 0.0, 1.0), axis=0, keepdims=True)
                outs.append(out)
                if half == 1:
                    o_t = jnp.concatenate(outs, axis=0)
                    o_ref[0, :, (pp * group + g) * LANES:(pp * group + g + 1) * LANES] = o_t.T.astype(BF16)
                    outs.clear()
            return acc, bad

        s_next = scores(items[0])
        pending = None
        for idx, item in enumerate(items):
            pp, g, half, ci = item
            s = s_next
            if idx + 1 < len(items):
                s_next = scores(items[idx + 1])
            cm = jnp.max(s, axis=0, keepdims=True)
            if ci == 0:
                sink = sink_ref[(2 * (j * pps + pp) + half) * group + g] * LOG2E if use_sink else None
                m = cm if sink is None else jnp.maximum(cm, sink)
                p = jnp.exp2(s - m).astype(BF16)
                scale = None
            else:
                m_new = jnp.maximum(m, cm)
                scale = jnp.exp2(m - m_new)
                p = jnp.exp2(s - (m if lagged else m_new)).astype(BF16)
                m = m_new
            if pending is not None:
                acc, bad = values_step(pending, acc, bad)
            pending = (item, p, scale, m, sink)
        _, bad = values_step(pending, acc, bad)
        return jnp.max(bad)

    if n_chunks == 1:
        run(lagged=False)
        return
    overflowed = run(lagged=True)

    @pl.when(overflowed > 0.0)
    def _():
        run(lagged=False)


def _attention(q, k, v, n_pairs, group, split, window, sinks, need_ctx):
    nb, s, _ = q.shape
    b_lat = nb - 1
    ctx_len = s // b_lat
    ql = q.shape[2] // (n_pairs * group)
    klw = k.shape[2] // n_pairs
    ow = n_pairs * group * LANES
    tq = min(Q_TILE, s)
    band = min(tq + 2 * SWA_WINDOW, s)
    use_sink = sinks is not None
    pps = max(1, HEADS_PER_STEP // (2 * group))
    mode = "window" if window else "dense"
    values_t = not window
    common = dict(pps=pps, group=group, ql=ql, klw=klw, split=split, tq=tq, band=band, seq=s,
                  use_sink=use_sink, values_t=values_t)
    smem = [pl.BlockSpec(memory_space=pltpu.SMEM)] if use_sink else []
    sink_args = [sinks] if use_sink else []
    qw, kw, o_w = pps * group * ql, pps * klw, pps * group * LANES
    if values_t:
        vr = 2 * pps * VT_ROWS
        v_lat = pl.BlockSpec((1, vr, s), lambda b, j, i: (b, j, 0))
        v_ctx = pl.BlockSpec((1, vr, ctx_len), lambda b, j, i: (b_lat, j, b))
    else:
        v_lat = pl.BlockSpec((1, s, pps * LANES), lambda b, j, i: (b, 0, j))
        v_ctx = pl.BlockSpec((1, ctx_len, pps * LANES), lambda b, j, i: (b_lat, b, j))

    lat_specs = smem + [
        pl.BlockSpec((1, tq, qw), lambda b, j, i: (b, i, j)),
        pl.BlockSpec((1, s, kw), lambda b, j, i: (b, 0, j)),
        v_lat,
        pl.BlockSpec((1, ctx_len, kw), lambda b, j, i: (b_lat, b, j)),
        v_ctx,
    ]
    o_lat = pl.pallas_call(
        functools.partial(_attn_body, mode=mode, **common),
        grid=(b_lat, n_pairs // pps, s // tq),
        in_specs=lat_specs,
        out_specs=pl.BlockSpec((1, tq, o_w), lambda b, j, i: (b, i, j)),
        out_shape=jax.ShapeDtypeStruct((b_lat, s, ow), BF16),
        compiler_params=_cparams("arbitrary", "arbitrary", "arbitrary"),
        name="attn_latent",
    )(*sink_args, q, k, v, k, v)
    if not need_ctx:
        return o_lat, None

    common["tq"] = ctx_len
    ctx_specs = smem + [
        pl.BlockSpec((1, ctx_len, qw), lambda b, j, i: (b_lat, b, j)),
        pl.BlockSpec((1, ctx_len, kw), lambda b, j, i: (b_lat, b, j)),
        v_ctx,
    ]
    o_ctx = pl.pallas_call(
        functools.partial(_attn_body, mode="ctx", **common),
        grid=(b_lat, n_pairs // pps, 1),
        in_specs=ctx_specs,
        out_specs=pl.BlockSpec((1, ctx_len, o_w), lambda b, j, i: (0, b, j)),
        out_shape=jax.ShapeDtypeStruct((1, s, ow), BF16),
        compiler_params=_cparams("arbitrary", "arbitrary", "arbitrary"),
        name="attn_context",
    )(*sink_args, q, k, v)
    return o_lat, o_ctx


def _post_body(*refs, moe, b_lat, has_ctx, split_x):
    refs = list(refs)
    o_ref = refs.pop(0)
    o = _pick_stream(o_ref, refs.pop(0), b_lat) if has_ctx else o_ref[0]
    x_ref = refs.pop(0)
    x_in = _pick_stream(x_ref, refs.pop(0), b_lat) if split_x else x_ref[0]
    if moe:
        mod_ref, g_ref, wo_ref, rw_ref, rb_ref, xo_ref, h_ref, route_ref = refs
    else:
        mod_ref, g_ref, wo_ref, wgu_ref, wd_ref, xo_ref = refs
    y = _dot(o, wo_ref[...])
    x = x_in + mod_ref[0, 2:3, :] * y
    h = _norm_mod(x, g_ref[...], mod_ref[0, 3:4, :], mod_ref[0, 4:5, :])
    h_hi = h.astype(BF16)
    if not moe:
        acc = None
        for c in range(D_FF // FF_CHUNK):
            gate = _dot(h_hi, wgu_ref[:, c * FF_CHUNK:(c + 1) * FF_CHUNK])
            up = _dot(h_hi, wgu_ref[:, D_FF + c * FF_CHUNK:D_FF + (c + 1) * FF_CHUNK])
            act = (_silu(gate) * up).astype(BF16)
            part = _dot(act, wd_ref[c * FF_CHUNK:(c + 1) * FF_CHUNK, :])
            acc = part if acc is None else acc + part
        xo_ref[0] = x + mod_ref[0, 5:6, :] * acc
        return
    xo_ref[0] = x
    _store_rows_as_tiles(h_ref, h)
    h_lo = (h - h_hi.astype(F32)).astype(BF16)
    both = _dot(h_hi, rw_ref[...])
    logits = both[:, :LANES] + both[:, LANES:] + _dot(h_lo, rw_ref[:, :LANES]) + rb_ref[...]
    lane = lax.broadcasted_iota(jnp.int32, logits.shape, 1).astype(F32)
    lg = jnp.where(lane < N_EXPERTS, logits, -jnp.inf)
    v1 = jnp.max(lg, axis=-1, keepdims=True)
    i1 = jnp.min(jnp.where(lg == v1, lane, float(LANES)), axis=-1, keepdims=True)
    lg2 = jnp.where(lane == i1, -jnp.inf, lg)
    v2 = jnp.max(lg2, axis=-1, keepdims=True)
    i2 = jnp.min(jnp.where(lg2 == v2, lane, float(LANES)), axis=-1, keepdims=True)
    e = jnp.exp(v2 - v1)
    g1 = 1.0 / (1.0 + e)
    g2 = e / (1.0 + e)
    route_ref[0] = jnp.where(lane == 0, i1, jnp.where(lane == 1, i2, jnp.where(
        lane == 2, g1, jnp.where(lane == 3, g2, 0.0))))


def _post(o_lat, o_ctx, xs, mods_i, norm_g, w_o, nb, router=None, ffn=None):
    split_x = isinstance(xs, tuple)
    b_lat, s, d = o_lat.shape
    tm = min(ROW_TILE, s)
    moe = router is not None
    assert moe != (ffn is not None)
    has_ctx = o_ctx is not None
    assert has_ctx or not split_x
    resident = lambda arr: pl.BlockSpec(arr.shape, lambda b, i: (0, 0), pipeline_mode=pl.Buffered(1))
    tok = pl.BlockSpec((1, tm, d), lambda b, i: (b, i, 0))
    in_specs = list(_split_specs(o_lat, tm)) if has_ctx else [tok]
    args = [o_lat, o_ctx] if has_ctx else [o_lat]
    n_o = len(args)
    in_specs += list(_split_specs(xs[0], tm)) if split_x else [tok]
    args += list(xs) if split_x else [xs]
    in_specs += [
        pl.BlockSpec((1, 6, d), lambda b, i: (b, 0, 0)),
        pl.BlockSpec((1, d), lambda b, i: (0, 0)),
        resident(w_o),
    ]
    args += [mods_i, norm_g.reshape(1, d), w_o]
    out_specs = [tok]
    out_shape = [jax.ShapeDtypeStruct((b_lat + 1, s, d), F32)]
    if moe:
        in_specs += [pl.BlockSpec((d, 2 * LANES), lambda b, i: (0, 0)),
                     pl.BlockSpec((1, LANES), lambda b, i: (0, 0))]
        args += list(router)
        per_b = s // tm
        out_specs += [pl.BlockSpec((tm * ROW_SUB, LANES), lambda b, i: (b * per_b + i, 0)),
                      pl.BlockSpec((1, tm, LANES), lambda b, i: (b, i, 0))]
        out_shape += [jax.ShapeDtypeStruct((nb * s * ROW_SUB, LANES), F32),
                      jax.ShapeDtypeStruct((nb, s, LANES), F32)]
    else:
        layer = ffn[2]
        in_specs += [pl.BlockSpec((None,) + w.shape[1:], lambda b, i: (layer, 0, 0), pipeline_mode=pl.Buffered(1))
                     for w in ffn[:2]]
        args += list(ffn[:2])
    return pl.pallas_call(
        functools.partial(_post_body, moe=moe, b_lat=b_lat, has_ctx=has_ctx, split_x=split_x),
        grid=(nb, s // tm),
        in_specs=in_specs,
        out_specs=out_specs,
        out_shape=out_shape,
        input_output_aliases={} if split_x else {n_o: 0},
        compiler_params=_cparams("arbitrary", "arbitrary"),
        name="attn_out_router" if moe else "attn_out_ffn",
    )(*args)


ROW_SUB = D_MODEL // LANES


def _store_rows_as_tiles(ref, x):
    n = x.shape[0]
    for sub in range(ROW_SUB):
        ref[pl.ds(sub, n, stride=ROW_SUB), :] = x[:, sub * LANES:(sub + 1) * LANES]


def _load_row_pieces(ref, first_tile, n):
    return [ref[pl.ds(first_tile * ROW_SUB + sub, n, stride=ROW_SUB), :] for sub in range(ROW_SUB)]


def _tile(ref, row8):
    return ref.at[pl.ds(pl.multiple_of(row8, ROW_SUB), ROW_SUB)]


def _tile_dma_wait(src_ref, dst_ref, sem, tiles):
    n = tiles * ROW_SUB
    pltpu.make_async_copy(src_ref.at[pl.ds(0, n)], dst_ref.at[pl.ds(0, n)], sem).wait()


def _dispatch_body(pos_ref, h_ref, xs_in_ref, xs_ref, sem):
    del xs_in_ref
    tm = h_ref.shape[0] // ROW_SUB

    def issue(r, carry):
        src = _tile(h_ref, r * ROW_SUB)
        for k in range(2):
            pltpu.make_async_copy(src, _tile(xs_ref, pos_ref[0, 0, 2 * r + k]), sem).start(priority=k)
        return carry

    lax.fori_loop(0, tm, issue, 0, unroll=8)
    _tile_dma_wait(h_ref, xs_ref, sem, tm)
    _tile_dma_wait(h_ref, xs_ref, sem, tm)


def _moe_dispatch(h_tiles, pos8, p, tm):
    t = h_tiles.shape[0] // ROW_SUB
    return pl.pallas_call(
        _dispatch_body,
        grid=(t // tm,),
        in_specs=[
            pl.BlockSpec((1, 1, 2 * tm), lambda i: (i, 0, 0), memory_space=pltpu.SMEM),
            pl.BlockSpec((tm * ROW_SUB, LANES), lambda i: (i, 0)),
            pl.BlockSpec(memory_space=pl.ANY),
        ],
        out_specs=pl.BlockSpec(memory_space=pl.ANY),
        out_shape=jax.ShapeDtypeStruct((p * ROW_SUB, LANES), F32),
        scratch_shapes=[pltpu.SemaphoreType.DMA(())],
        input_output_aliases={2: 0},
        compiler_params=_cparams("arbitrary"),
        name="moe_dispatch",
    )(pos8.reshape(t // tm, 1, 2 * tm), h_tiles, jnp.zeros((p * ROW_SUB, LANES), F32))


def _moe_body(te_ref, nu_ref, xs_ref, wg_ref, wu_ref, wd_ref, ys_ref):
    i = pl.program_id(0)
    tm = xs_ref.shape[0] // ROW_SUB

    @pl.when(i < nu_ref[0])
    def _():
        x = jnp.concatenate(_load_row_pieces(xs_ref, 0, tm), axis=1).astype(BF16)
        act = (_silu(_dot(x, wg_ref[0])) * _dot(x, wu_ref[0])).astype(BF16)
        _store_rows_as_tiles(ys_ref, _dot(act, wd_ref[0]))

    @pl.when(i >= nu_ref[0])
    def _():
        ys_ref[...] = jnp.zeros_like(ys_ref)


def _moe_experts(xs_sorted, tile_expert, n_used, w_gu, w_d, layer, tm):
    p = xs_sorted.shape[0] // ROW_SUB
    f, d = w_d.shape[2:]
    rows = pl.BlockSpec((tm * ROW_SUB, LANES), lambda i, te, nu: (i, 0))
    grid_spec = pltpu.PrefetchScalarGridSpec(
        num_scalar_prefetch=2,
        grid=(p // tm,),
        in_specs=[
            rows,
            pl.BlockSpec((None, 1, d, f), lambda i, te, nu: (layer, te[i], 0, 0)),
            pl.BlockSpec((None, 1, d, f), lambda i, te, nu: (layer, te[i], 0, 1)),
            pl.BlockSpec((None, 1, f, d), lambda i, te, nu: (layer, te[i], 0, 0)),
        ],
        out_specs=rows,
    )
    return pl.pallas_call(
        _moe_body,
        grid_spec=grid_spec,
        out_shape=jax.ShapeDtypeStruct((p * ROW_SUB, LANES), F32),
        compiler_params=_cparams("arbitrary"),
        name="moe_experts",
    )(tile_expert, n_used, xs_sorted, w_gu, w_gu, w_d)


def _combine_body(*refs, final):
    if final:
        pos_ref, pos_next_ref, x_ref, route_ref, mod_ref, g_ref, ys_ref, o_ref, ybuf, sems = refs
    else:
        pos_ref, pos_next_ref, x_ref, route_ref, mod_ref, ys_ref, o_ref, ybuf, sems = refs
    tm, d = x_ref.shape[1:]
    step = pl.program_id(0) * pl.num_programs(1) + pl.program_id(1)
    n_steps = pl.num_programs(0) * pl.num_programs(1)
    slot = lax.rem(step, 2)

    def gather(p_ref, into):
        def issue(r, carry):
            for k in range(2):
                pltpu.make_async_copy(_tile(ys_ref, p_ref[0, 0, 2 * r + k]),
                                      _tile(ybuf.at[into], (k * tm + r) * ROW_SUB), sems.at[into]).start(priority=k)
            return carry

        lax.fori_loop(0, tm, issue, 0, unroll=8)

    @pl.when(step == 0)
    def _():
        gather(pos_ref, 0)

    @pl.when(step + 1 < n_steps)
    def _():
        gather(pos_next_ref, 1 - slot)

    mine = ybuf.at[slot]
    _tile_dma_wait(ys_ref, mine, sems.at[slot], 2 * tm)
    route = route_ref[0]
    g1, g2 = route[:, 2:3], route[:, 3:4]
    y1, y2 = _load_row_pieces(mine, 0, tm), _load_row_pieces(mine, tm, tm)
    cols = [slice(sub * LANES, (sub + 1) * LANES) for sub in range(ROW_SUB)]
    xn = [x_ref[0, :, c] + mod_ref[0, 5:6, c] * (g1 * a + g2 * b) for c, a, b in zip(cols, y1, y2)]
    if final:
        ms = sum(jnp.sum(v * v, axis=-1, keepdims=True) for v in xn) * (1.0 / d)
        scale = lax.rsqrt(ms + NORM_EPS)
        xn = [v * scale * g_ref[:, c] for c, v in zip(cols, xn)]
    for c, v in zip(cols, xn):
        o_ref[0, :, c] = v


def _moe_combine(xs, ys, pos, route, mods_i, nb, final_g=None):
    _, s, d = xs.shape
    tm = min(ROW_TILE, s)
    per_b = s // tm
    final = final_g is not None
    tok = pl.BlockSpec((1, tm, d), lambda b, i: (b, i, 0))
    last = nb * per_b - 1
    pos3 = pos.reshape(nb * per_b, 1, 2 * tm)
    in_specs = [pl.BlockSpec((1, 1, 2 * tm), lambda b, i: (b * per_b + i, 0, 0), memory_space=pltpu.SMEM),
                pl.BlockSpec((1, 1, 2 * tm), lambda b, i: (jnp.minimum(b * per_b + i + 1, last), 0, 0),
                             memory_space=pltpu.SMEM),
                tok,
                pl.BlockSpec((1, tm, LANES), lambda b, i: (b, i, 0)),
                pl.BlockSpec((1, 6, d), lambda b, i: (b, 0, 0))]
    args = [pos3, pos3, xs, route, mods_i]
    if final:
        in_specs.append(pl.BlockSpec((1, d), lambda b, i: (0, 0)))
        args.append(final_g.reshape(1, d))
    in_specs.append(pl.BlockSpec(memory_space=pl.ANY))
    args.append(ys)
    return pl.pallas_call(
        functools.partial(_combine_body, final=final),
        grid=(nb, per_b),
        in_specs=in_specs,
        out_specs=tok,
        out_shape=jax.ShapeDtypeStruct((nb, s, d) if final else xs.shape, F32),
        scratch_shapes=[pltpu.VMEM((2, 2 * tm * ROW_SUB, LANES), F32), pltpu.SemaphoreType.DMA((2,))],
        input_output_aliases={} if final else {2: 0},
        compiler_params=_cparams("arbitrary", "arbitrary"),
        name="moe_combine",
    )(*args)


def _moe_layer(hp, route, xs, mods_i, w_gu, w_d, layer, nb, final_g):
    _, s, d = xs.shape
    t = nb * s
    tm = min(MOE_TILE, s)
    p = 2 * t + N_EXPERTS * tm
    expert = route.reshape(t, LANES)[:, :2].astype(jnp.int32).reshape(2 * t)
    onehot = (expert[:, None] == jnp.arange(N_EXPERTS, dtype=jnp.int32)[None, :]).astype(jnp.int32)
    csum = jnp.cumsum(onehot, axis=0)
    padded = ((csum[-1] + tm - 1) // tm) * tm
    off_end = jnp.cumsum(padded)
    pos = jnp.sum(onehot * ((off_end - padded)[None, :] + csum - 1), axis=1)
    tile_start = jnp.arange(p // tm, dtype=jnp.int32) * tm
    tile_expert = jnp.minimum(jnp.sum(tile_start[:, None] >= off_end[None, :], axis=1), N_EXPERTS - 1)
    n_used = (off_end[-1] // tm).reshape(1).astype(jnp.int32)

    pos8 = pos * ROW_SUB
    xs_sorted = _moe_dispatch(hp, pos8, p, tm)
    ys = _moe_experts(xs_sorted, tile_expert.astype(jnp.int32), n_used, w_gu, w_d, layer, tm)
    return _moe_combine(xs, ys, pos8, route, mods_i, nb, final_g)


def kernel(x, c, ctx, c_ctx, ada_w, ada_b, norm_g, final_norm_g, mla_w_dq, mla_q_norm_g, mla_w_uq, mla_w_dkv, mla_kv_norm_g, mla_w_ukv, mla_w_o, swa_w_qkv, swa_sinks, swa_w_o, ga_w_qkv, ga_q_norm_g, ga_k_norm_g, ga_w_o, ffn_w_gate_up, ffn_w_down, moe_router_w, moe_router_b, moe_w_gate_up, moe_w_down):
    b, s, d = x.shape
    assert d == D_MODEL and b * ctx.shape[1] == s and s % GRID_W == 0
    nb = b + 1
    xs = (x, ctx.reshape(1, s, d))
    c_all = jnp.concatenate([c, c_ctx[None], jnp.zeros((16 - nb, d), F32)], axis=0)
    mods = _mods(c_all, ada_w, ada_b).reshape(DEPTH, 16, 6, d)

    ffn_gu, ffn_dn = ffn_w_gate_up.astype(BF16), ffn_w_down.astype(BF16)
    moe_gu, moe_dn = moe_w_gate_up.astype(BF16), moe_w_down.astype(BF16)
    gqa_tab = _gqa_tables(s)
    mla_tab = _mla_tables(s)
    nope_src, rope_src = _mla_lane_src()

    for i in range(DEPTH):
        need_ctx = i < DEPTH - 1
        mods_i = mods[i]
        kind, j = i % 3, i // 3
        if kind == 0:
            dkv = jnp.concatenate([mla_w_dkv[j], jnp.zeros((d, 1), F32)], axis=1)
            kr_cols = np.where(rope_src >= 0, MLA_KV_LORA + rope_src, MLA_KV_LORA + MLA_QK_ROPE)
            w_a = jnp.concatenate([mla_w_dq[j], dkv[:, :MLA_KV_LORA], dkv[:, kr_cols]], axis=1).astype(BF16)
            qd = MLA_QK_NOPE + MLA_QK_ROPE
            q_src = np.where(nope_src >= 0, nope_src, np.where(rope_src >= 0, MLA_QK_NOPE + rope_src, qd))
            uq = jnp.concatenate([mla_w_uq[j].reshape(MLA_Q_LORA, N_HEADS, qd),
                                  jnp.zeros((MLA_Q_LORA, N_HEADS, 1), F32)], axis=2)
            w_uq = uq[:, :, q_src].reshape(MLA_Q_LORA, N_HEADS * LANES).astype(BF16)
            kvd = MLA_QK_NOPE + MLA_V_DIM
            ukv = jnp.concatenate([mla_w_ukv[j].reshape(MLA_KV_LORA, N_HEADS, kvd),
                                   jnp.zeros((MLA_KV_LORA, N_HEADS, 1), F32)], axis=2)
            k_src = np.where(nope_src >= 0, nope_src, kvd)
            w_ukv = jnp.concatenate([ukv[:, :, k_src].reshape(MLA_KV_LORA, N_HEADS * LANES),
                                     ukv[:, :, MLA_QK_NOPE:kvd].reshape(MLA_KV_LORA, N_HEADS * MLA_V_DIM)],
                                    axis=1).astype(BF16)
            q, k, v = _mla_proj(xs, mods_i, norm_g[i, 0], w_a, mla_q_norm_g[j], mla_kv_norm_g[j],
                                w_uq, w_ukv, mla_tab)
            o = _attention(q, k, v, N_HEADS // 2, 1, True, False, None, need_ctx)
            w_o = mla_w_o[j].astype(BF16)
        elif kind == 1:
            cols, o_rows = _gqa_perm(SWA_KV_HEADS, N_HEADS // SWA_KV_HEADS)
            q, k, v = _gqa_proj(xs, mods_i, norm_g[i, 0], swa_w_qkv[j][:, cols].astype(BF16), gqa_tab,
                                SWA_KV_HEADS, values_t=False)
            o = _attention(q, k, v, SWA_KV_HEADS // 2, N_HEADS // SWA_KV_HEADS, False, True,
                           swa_sinks[j], need_ctx)
            w_o = swa_w_o[j][o_rows].astype(BF16)
        else:
            cols, o_rows = _gqa_perm(GA_KV_HEADS, N_HEADS // GA_KV_HEADS)
            q, k, v = _gqa_proj(xs, mods_i, norm_g[i, 0], ga_w_qkv[j][:, cols].astype(BF16), gqa_tab,
                                GA_KV_HEADS, (ga_q_norm_g[j], ga_k_norm_g[j]))
            o = _attention(q, k, v, GA_KV_HEADS // 2, N_HEADS // GA_KV_HEADS, False, False, None, need_ctx)
            w_o = ga_w_o[j][o_rows].astype(BF16)

        n_tok_b = nb if need_ctx else b
        f = i // 2
        if i % 2 == 0:
            xs, = _post(*o, xs, mods_i, norm_g[i, 1], w_o, n_tok_b, ffn=(ffn_gu, ffn_dn, f))
        else:
            rw = jnp.concatenate([moe_router_w[f], jnp.zeros((d, LANES - N_EXPERTS), F32)], axis=1)
            rw_hi = rw.astype(BF16)
            rw_lo = (rw - rw_hi.astype(F32)).astype(BF16)
            rb = jnp.concatenate([moe_router_b[f], jnp.zeros((LANES - N_EXPERTS,), F32)]).reshape(1, LANES)
            xs, h2, comb = _post(*o, xs, mods_i, norm_g[i, 1], w_o, n_tok_b,
                                 (jnp.concatenate([rw_hi, rw_lo], axis=1), rb))
            xs = _moe_layer(h2, comb, xs, mods_i, moe_gu, moe_dn, f, n_tok_b,
                            None if need_ctx else final_norm_g)
    return xs
```

```python
import functools
import math

import numpy as np
import jax
import jax.numpy as jnp
from jax import lax
from jax.experimental import pallas as pl
from jax.experimental.pallas import tpu as pltpu

F32 = jnp.float32
BF16 = jnp.bfloat16

D_MODEL = 1024
GRID_W = 64
HEAD_DIM = 64
N_HEADS = 16
MLA_Q_LORA = 384
MLA_KV_LORA = 256
MLA_QK_NOPE = 64
MLA_QK_ROPE = 32
MLA_V_DIM = 64
SWA_KV_HEADS = 4
SWA_WINDOW = 128
GA_KV_HEADS = 8
D_FF = 2816
N_EXPERTS = 8
EXPERT_FF = 1408
ROPE_THETA = 10000.0
NORM_EPS = 1e-6
NEG_INF = -1e30
DEPTH = 4
LOG2E = math.log2(math.e)

LANES = 128
VMEM_LIMIT = 56 * 2**20
ROW_TILE = 512
Q_TILE = 512
KEY_CHUNK = 2048
HEADS_PER_STEP = 8
MOE_TILE = 512
FF_CHUNK = 2816


def _cparams(*sem):
    return pltpu.CompilerParams(dimension_semantics=sem, vmem_limit_bytes=VMEM_LIMIT)


def _silu(x):
    return x * (1.0 / (1.0 + jnp.exp(-x)))


def _rms(x, g):
    ms = jnp.mean(x * x, axis=-1, keepdims=True)
    return x * lax.rsqrt(ms + NORM_EPS) * g


def _norm_mod(x, g, shift, scale):
    return _rms(x, g) * (1.0 + scale) + shift


def _dot(a, b):
    return jnp.dot(a, b, preferred_element_type=F32)


def _dot_t(a, b):
    return lax.dot_general(a, b, (((1,), (1,)), ((), ())), preferred_element_type=F32)


def _mods_body(c_ref, w_ref, b_ref, o_ref):
    sc = _silu(c_ref[...]).astype(BF16)
    o_ref[0] = _dot(sc, w_ref[0].astype(BF16)) + b_ref[0]


def _mods(c_all, ada_w, ada_b):
    depth, d, n = ada_w.shape
    rows = c_all.shape[0]
    tn = 1536
    return pl.pallas_call(
        _mods_body,
        grid=(depth, n // tn),
        in_specs=[
            pl.BlockSpec((rows, d), lambda i, j: (0, 0)),
            pl.BlockSpec((1, d, tn), lambda i, j: (i, 0, j)),
            pl.BlockSpec((1, 1, tn), lambda i, j: (i, 0, j)),
        ],
        out_specs=pl.BlockSpec((1, rows, tn), lambda i, j: (i, 0, j)),
        out_shape=jax.ShapeDtypeStruct((depth, rows, n), F32),
        compiler_params=_cparams("arbitrary", "arbitrary"),
        name="adaln_mods",
    )(c_all, ada_w, ada_b.reshape(depth, 1, n))


def _pair_lanes():
    lane = np.arange(LANES)
    is_b = (lane % 64) >= 32
    dim = (lane % 32) + 32 * (lane // 64)
    return is_b, dim


def _gqa_perm(n_kv, group):
    is_b, dim = _pair_lanes()
    nq = N_HEADS * HEAD_DIM
    nk = n_kv * HEAD_DIM
    q_cols, k_cols, o_rows = [], [], []
    nat = np.arange(HEAD_DIM)
    for j in range(n_kv // 2):
        for g in range(group):
            head_a, head_b = (2 * j) * group + g, (2 * j + 1) * group + g
            q_cols.append(np.where(is_b, head_b, head_a) * HEAD_DIM + dim)
            o_rows.append(np.concatenate([head_a * HEAD_DIM + nat, head_b * HEAD_DIM + nat]))
        k_cols.append(nq + np.where(is_b, 2 * j + 1, 2 * j) * HEAD_DIM + dim)
    cols = np.concatenate(q_cols + k_cols + [nq + nk + np.arange(nk)])
    return cols.astype(np.int32), np.concatenate(o_rows).astype(np.int32)


def _angles(s, rot_dim):
    n_freq = rot_dim // 4
    inv = ROPE_THETA ** (-jnp.arange(n_freq, dtype=F32) / n_freq)
    pos = jnp.arange(s)
    rows = (pos // GRID_W).astype(F32)
    cols = (pos % GRID_W).astype(F32)
    return jnp.concatenate([rows[:, None] * inv, cols[:, None] * inv], axis=-1)


def _rope_tables(cos_l, sin_l, q_scale):
    lat = jnp.stack([cos_l * q_scale, sin_l * q_scale, cos_l, sin_l])
    one = jnp.ones_like(cos_l)
    zero = jnp.zeros_like(cos_l)
    ctx = jnp.stack([one * q_scale, zero, one, zero])
    return jnp.stack([lat, ctx])


def _gqa_tables(s):
    ang = _angles(s, HEAD_DIM)
    lane = np.arange(LANES)
    idx = lane % 32
    sign = jnp.asarray(np.where(lane < 64, -1.0, 1.0), F32)
    return _rope_tables(jnp.cos(ang)[:, idx], jnp.sin(ang)[:, idx] * sign, HEAD_DIM ** -0.5 * LOG2E)


def _mla_lane_src():
    lane = np.arange(LANES)
    nope = np.where((lane >= 16) & (lane < 64), lane - 16,
                    np.where((lane >= 80) & (lane < 96), 48 + lane - 80, -1))
    rope = np.where(lane < 16, lane, np.where((lane >= 64) & (lane < 80), 16 + lane - 64, -1))
    return nope, rope


def _mla_tables(s):
    ang = _angles(s, MLA_QK_ROPE)
    lane = np.arange(LANES)
    is_x1 = lane < 16
    is_x2 = (lane >= 64) & (lane < 80)
    idx = np.where(is_x1, lane, np.where(is_x2, lane - 64, 0))
    rot = jnp.asarray(is_x1 | is_x2)
    sign = jnp.asarray(np.where(is_x1, -1.0, np.where(is_x2, 1.0, 0.0)), F32)
    cos_l = jnp.where(rot, jnp.cos(ang)[:, idx], 1.0)
    sin_l = jnp.sin(ang)[:, idx] * sign
    return _rope_tables(cos_l, sin_l, (MLA_QK_NOPE + MLA_QK_ROPE) ** -0.5 * LOG2E)


def _rope(blk, cos, sin):
    return blk * cos + pltpu.roll(blk, 64, 1) * sin


VT_ROWS = 80
DEN_ROW = 64


def _ones_rows(n_cols):
    r = lax.broadcasted_iota(jnp.int32, (VT_ROWS - DEN_ROW, n_cols), 0)
    return jnp.where(r == 0, 1.0, 0.0)


def _store_values_t(vt_ref, v, n_heads):
    v_t = v.T
    tail = _ones_rows(v.shape[0]).astype(vt_ref.dtype)
    for h in range(n_heads):
        vt_ref[0, h * VT_ROWS:h * VT_ROWS + DEN_ROW, :] = v_t[h * DEN_ROW:(h + 1) * DEN_ROW, :].astype(vt_ref.dtype)
        vt_ref[0, h * VT_ROWS + DEN_ROW:(h + 1) * VT_ROWS, :] = tail


def _gqa_proj_body(*refs, nqb, nkb, qk_norm, values_t):
    if qk_norm:
        x_ref, mod_ref, g_ref, w_ref, tab_ref, gq_ref, gk_ref, ind_ref, q_ref, k_ref, v_ref = refs
    else:
        x_ref, mod_ref, g_ref, w_ref, tab_ref, q_ref, k_ref, v_ref = refs
    h = _norm_mod(x_ref[0], g_ref[...], mod_ref[0, 0:1, :], mod_ref[0, 1:2, :]).astype(BF16)
    qkv = _dot(h, w_ref[...])
    cq, sq, ck, sk = tab_ref[0, 0], tab_ref[0, 1], tab_ref[0, 2], tab_ref[0, 3]

    def head_norm(blk, gain):
        ssq = _dot((blk * blk).astype(BF16), ind_ref[...])
        return blk * lax.rsqrt(ssq * (1.0 / HEAD_DIM) + NORM_EPS) * gain

    for c in range(nqb):
        blk = qkv[:, c * LANES:(c + 1) * LANES]
        if qk_norm:
            blk = head_norm(blk, gq_ref[...])
        q_ref[0, :, c * LANES:(c + 1) * LANES] = _rope(blk, cq, sq).astype(BF16)
    for c in range(nkb):
        blk = qkv[:, (nqb + c) * LANES:(nqb + c + 1) * LANES]
        if qk_norm:
            blk = head_norm(blk, gk_ref[...])
        k_ref[0, :, c * LANES:(c + 1) * LANES] = _rope(blk, ck, sk).astype(BF16)
    v = qkv[:, (nqb + nkb) * LANES:]
    if values_t:
        _store_values_t(v_ref, v, 2 * nkb)
    else:
        v_ref[0] = v.astype(BF16)


def _gqa_proj(xs, mods_i, norm_g, w_perm, tables, n_kv, qk_gains=None, values_t=True):
    nb, s, d = xs.shape
    tm = min(ROW_TILE, s)
    nqb = N_HEADS * HEAD_DIM // LANES
    nkb = n_kv * HEAD_DIM // LANES
    wn = w_perm.shape[1]
    in_specs = [
        pl.BlockSpec((1, tm, d), lambda b, i: (b, i, 0)),
        pl.BlockSpec((1, 6, d), lambda b, i: (b, 0, 0)),
        pl.BlockSpec((1, d), lambda b, i: (0, 0)),
        pl.BlockSpec((d, wn), lambda b, i: (0, 0)),
        pl.BlockSpec((1, 4, tm, LANES), lambda b, i: (b // (nb - 1), 0, i, 0)),
    ]
    args = [xs, mods_i, norm_g.reshape(1, d), w_perm, tables]
    if qk_gains is not None:
        is_b, dim = _pair_lanes()
        ind = jnp.asarray(is_b[:, None] == is_b[None, :], BF16)
        in_specs += [pl.BlockSpec((1, LANES), lambda b, i: (0, 0)),
                     pl.BlockSpec((1, LANES), lambda b, i: (0, 0)),
                     pl.BlockSpec((LANES, LANES), lambda b, i: (0, 0))]
        args += [qk_gains[0][dim].reshape(1, LANES), qk_gains[1][dim].reshape(1, LANES), ind]
    out_w = (nqb * LANES, nkb * LANES, nkb * LANES)
    out_specs = [pl.BlockSpec((1, tm, w), lambda b, i: (b, i, 0)) for w in out_w]
    out_shape = [jax.ShapeDtypeStruct((nb, s, w), BF16) for w in out_w]
    if values_t:
        out_specs[2] = pl.BlockSpec((1, n_kv * VT_ROWS, tm), lambda b, i: (b, 0, i))
        out_shape[2] = jax.ShapeDtypeStruct((nb, n_kv * VT_ROWS, s), BF16)
    return pl.pallas_call(
        functools.partial(_gqa_proj_body, nqb=nqb, nkb=nkb, qk_norm=qk_gains is not None, values_t=values_t),
        grid=(nb, s // tm),
        in_specs=in_specs,
        out_specs=out_specs,
        out_shape=out_shape,
        compiler_params=_cparams("arbitrary", "arbitrary"),
        name="gqa_proj",
    )(*args)


def _split_specs(lat, tm):
    b_lat, s, w = lat.shape
    last_i = s // tm - 1
    return (pl.BlockSpec((1, tm, w), lambda b, i: (jnp.minimum(b, b_lat - 1), jnp.where(b < b_lat, i, last_i), 0)),
            pl.BlockSpec((1, tm, w), lambda b, i: (0, jnp.where(b < b_lat, 0, i), 0)))


def _pick_stream(lat_ref, ctx_ref, b_lat):
    return jnp.where(pl.program_id(0) == b_lat, ctx_ref[0], lat_ref[0])


def _mla_proj_body(*refs, split_x):
    refs = list(refs)
    x_ref = refs.pop(0)
    x = _pick_stream(x_ref, refs.pop(0), pl.num_programs(0) - 1) if split_x else x_ref[0]
    mod_ref, g_ref, wa_ref, gq_ref, gkv_ref, wuq_ref, wukv_ref, tab_ref, q_ref, k_ref, v_ref = refs
    h = _norm_mod(x, g_ref[...], mod_ref[0, 0:1, :], mod_ref[0, 1:2, :]).astype(BF16)
    a = _dot(h, wa_ref[...])
    qn = _rms(a[:, :MLA_Q_LORA], gq_ref[...]).astype(BF16)
    cn = _rms(a[:, MLA_Q_LORA:MLA_Q_LORA + MLA_KV_LORA], gkv_ref[...]).astype(BF16)
    kr = a[:, MLA_Q_LORA + MLA_KV_LORA:]
    cq, sq, ck, sk = tab_ref[0, 0], tab_ref[0, 1], tab_ref[0, 2], tab_ref[0, 3]
    q = _dot(qn, wuq_ref[...])
    kv = _dot(cn, wukv_ref[...])
    kr = _rope(kr, ck, sk)
    for hh in range(N_HEADS):
        sl = slice(hh * LANES, (hh + 1) * LANES)
        q_ref[0, :, sl] = _rope(q[:, sl], cq, sq).astype(BF16)
        k_ref[0, :, sl] = (kv[:, sl] + kr).astype(BF16)
    _store_values_t(v_ref, kv[:, N_HEADS * LANES:], N_HEADS)


def _mla_proj(xs, mods_i, norm_g, w_a, gq, gkv, w_uq, w_ukv, tables):
    split_x = isinstance(xs, tuple)
    if split_x:
        (b_lat, s, d), nb = xs[0].shape, xs[0].shape[0] + 1
    else:
        nb, s, d = xs.shape
    tm = min(ROW_TILE, s)
    full = lambda arr: pl.BlockSpec(arr.shape, lambda b, i: (0,) * arr.ndim)
    gq = gq.reshape(1, -1)
    gkv = gkv.reshape(1, -1)
    g = norm_g.reshape(1, d)
    qk_w = N_HEADS * LANES
    vt_rows = N_HEADS * VT_ROWS
    x_specs = list(_split_specs(xs[0], tm)) if split_x else [pl.BlockSpec((1, tm, d), lambda b, i: (b, i, 0))]
    x_args = list(xs) if split_x else [xs]
    return pl.pallas_call(
        functools.partial(_mla_proj_body, split_x=split_x),
        grid=(nb, s // tm),
        in_specs=x_specs + [
            pl.BlockSpec((1, 6, d), lambda b, i: (b, 0, 0)),
            full(g), full(w_a), full(gq), full(gkv), full(w_uq), full(w_ukv),
            pl.BlockSpec((1, 4, tm, LANES), lambda b, i: (b // (nb - 1), 0, i, 0)),
        ],
        out_specs=[pl.BlockSpec((1, tm, qk_w), lambda b, i: (b, i, 0)),
                   pl.BlockSpec((1, tm, qk_w), lambda b, i: (b, i, 0)),
                   pl.BlockSpec((1, vt_rows, tm), lambda b, i: (b, 0, i))],
        out_shape=[jax.ShapeDtypeStruct((nb, s, qk_w), BF16), jax.ShapeDtypeStruct((nb, s, qk_w), BF16),
                   jax.ShapeDtypeStruct((nb, vt_rows, s), BF16)],
        compiler_params=_cparams("arbitrary", "arbitrary"),
        name="mla_proj",
    )(*x_args, mods_i, g, w_a, gq, gkv, w_uq, w_ukv, tables)


def _attn_body(*refs, pps, group, ql, klw, split, mode, tq, band, seq, use_sink, values_t):
    refs = list(refs)
    sink_ref = refs.pop(0) if use_sink else None
    q_ref = refs.pop(0)
    if mode != "ctx":
        kl_ref, vl_ref = refs.pop(0), refs.pop(0)
    kc_ref, vc_ref, o_ref = refs
    j = pl.program_id(1)
    qi = pl.program_id(2)
    lane = lax.broadcasted_iota(jnp.int32, (1, LANES), 1)
    in_a = (lane & 63) < 32

    def transpose_values(v):
        v_t = v.astype(F32).T
        tail = _ones_rows(v.shape[0])
        return [jnp.concatenate([v_t[h * DEN_ROW:(h + 1) * DEN_ROW], tail], axis=0).astype(BF16)
                for h in range(2)]

    ctx_len = kc_ref.shape[1]
    valid = None
    if mode == "dense":
        n_lat = seq
    elif mode == "window":
        start = jnp.clip(qi * tq - SWA_WINDOW, 0, seq - band)
        start = pl.multiple_of(start, LANES)
        kpos = start + lax.broadcasted_iota(jnp.int32, (band, 1), 0)
        qpos = qi * tq + lax.broadcasted_iota(jnp.int32, (1, tq), 1)
        valid = jnp.abs(qpos - kpos) <= SWA_WINDOW
        n_lat = band
    else:
        n_lat = 0

    chunks = []
    for pp in range(pps):
        ks = slice(pp * klw, (pp + 1) * klw)
        mine = []
        if values_t:
            head_rows = lambda half, pp=pp: slice((2 * pp + half) * VT_ROWS, (2 * pp + half + 1) * VT_ROWS)
            mine.append((lambda ks=ks: kc_ref[0, :, ks], lambda half, hr=head_rows: vc_ref[0, hr(half), :], None))
            for c0 in range(0, n_lat, KEY_CHUNK):
                c1 = min(c0 + KEY_CHUNK, n_lat)
                mine.append((lambda ks=ks, c0=c0, c1=c1: kl_ref[0, c0:c1, ks],
                             lambda half, hr=head_rows, c0=c0, c1=c1: vl_ref[0, hr(half), c0:c1], None))
        else:
            ps = slice(pp * LANES, (pp + 1) * LANES)
            vc = transpose_values(vc_ref[0, :, ps])
            mine.append((lambda ks=ks: kc_ref[0, :, ks], lambda half, vc=vc: vc[half], None))
            if mode != "ctx":
                vl = transpose_values(vl_ref[0, pl.ds(start, band), ps])
                for c0 in range(0, n_lat, KEY_CHUNK):
                    c1 = min(c0 + KEY_CHUNK, n_lat)
                    mine.append((lambda ks=ks, c0=c0, c1=c1: kl_ref[0, pl.ds(start + c0, c1 - c0), ks],
                                 lambda half, vl=vl, c0=c0, c1=c1: vl[half][:, c0:c1],
                                 valid[c0:c1]))
        chunks.append(mine)
    n_chunks = len(chunks[0])

    def head_q(pp, g, half):
        q = q_ref[0, :, (pp * group + g) * ql:(pp * group + g + 1) * ql]
        if split:
            return q[:, half * LANES:(half + 1) * LANES]
        return jnp.where(in_a if half == 0 else jnp.logical_not(in_a), q, jnp.zeros_like(q))

    def scores(item):
        pp, g, half, ci = item
        load_keys, _, ok = chunks[pp][ci]
        kch = load_keys()
        if split:
            kch = kch[:, half * LANES:(half + 1) * LANES]
        s = _dot_t(kch, head_q(pp, g, half))
        return s if ok is None else jnp.where(ok, s, NEG_INF)

    items = [(pp, g, half, ci) for pp in range(pps) for g in range(group) for half in range(2)
             for ci in range(n_chunks)]
    last = n_chunks - 1

    def run(lagged):
        outs = []
        acc = None
        bad = jnp.zeros((1, tq), F32)

        def values_step(pend, acc, bad):
            (pp, g, half, ci), p, scale, m_end, sink = pend
            pv = _dot(chunks[pp][ci][1](half), p)
            if ci == 0:
                acc = pv
            elif lagged:
                acc = (acc + pv) * scale
            else:
                acc = acc * scale + pv
            if ci == last:
                den = acc[DEN_ROW:DEN_ROW + 1, :]
                if sink is not None:
                    den = den + jnp.exp2(sink - m_end)
                out = acc[:DEN_ROW] * (1.0 / den)
                if lagged:
                    finite = (out - out) == 0.0
                    bad = bad + jnp.sum(jnp.where(finite, 0.0, 1.0), axis=0, keepdims=True)
                outs.append(out)
                if half == 1:
                    o_t = jnp.concatenate(outs, axis=0)
                    o_ref[0, :, (pp * group + g) * LANES:(pp * group + g + 1) * LANES] = o_t.T.astype(BF16)
                    outs.clear()
            return acc, bad

        s_next = scores(items[0])
        pending = None
        for idx, item in enumerate(items):
            pp, g, half, ci = item
            s = s_next
            if idx + 1 < len(items):
                s_next = scores(items[idx + 1])
            cm = jnp.max(s, axis=0, keepdims=True)
            if ci == 0:
                sink = sink_ref[(2 * (j * pps + pp) + half) * group + g] * LOG2E if use_sink else None
                m = cm if sink is None else jnp.maximum(cm, sink)
                p = jnp.exp2(s - m).astype(BF16)
                scale = None
            else:
                m_new = jnp.maximum(m, cm)
                scale = jnp.exp2(m - m_new)
                p = jnp.exp2(s - (m if lagged else m_new)).astype(BF16)
                m = m_new
            if pending is not None:
                acc, bad = values_step(pending, acc, bad)
            pending = (item, p, scale, m, sink)
        _, bad = values_step(pending, acc, bad)
        return jnp.max(bad)

    if n_chunks == 1:
        run(lagged=False)
        return
    overflowed = run(lagged=True)

    @pl.when(overflowed > 0.0)
    def _():
        run(lagged=False)


def _attention(q, k, v, n_pairs, group, split, window, sinks, need_ctx):
    nb, s, _ = q.shape
    b_lat = nb - 1
    ctx_len = s // b_lat
    ql = q.shape[2] // (n_pairs * group)
    klw = k.shape[2] // n_pairs
    ow = n_pairs * group * LANES
    tq = min(Q_TILE, s)
    band = min(tq + 2 * SWA_WINDOW, s)
    use_sink = sinks is not None
    pps = max(1, HEADS_PER_STEP // (2 * group))
    mode = "window" if window else "dense"
    values_t = not window
    common = dict(pps=pps, group=group, ql=ql, klw=klw, split=split, tq=tq, band=band, seq=s,
                  use_sink=use_sink, values_t=values_t)
    smem = [pl.BlockSpec(memory_space=pltpu.SMEM)] if use_sink else []
    sink_args = [sinks] if use_sink else []
    qw, kw, o_w = pps * group * ql, pps * klw, pps * group * LANES
    if values_t:
        vr = 2 * pps * VT_ROWS
        v_lat = pl.BlockSpec((1, vr, s), lambda b, j, i: (b, j, 0))
        v_ctx = pl.BlockSpec((1, vr, ctx_len), lambda b, j, i: (b_lat, j, b))
    else:
        v_lat = pl.BlockSpec((1, s, pps * LANES), lambda b, j, i: (b, 0, j))
        v_ctx = pl.BlockSpec((1, ctx_len, pps * LANES), lambda b, j, i: (b_lat, b, j))

    lat_specs = smem + [
        pl.BlockSpec((1, tq, qw), lambda b, j, i: (b, i, j)),
        pl.BlockSpec((1, s, kw), lambda b, j, i: (b, 0, j)),
        v_lat,
        pl.BlockSpec((1, ctx_len, kw), lambda b, j, i: (b_lat, b, j)),
        v_ctx,
    ]
    o_lat = pl.pallas_call(
        functools.partial(_attn_body, mode=mode, **common),
        grid=(b_lat, n_pairs // pps, s // tq),
        in_specs=lat_specs,
        out_specs=pl.BlockSpec((1, tq, o_w), lambda b, j, i: (b, i, j)),
        out_shape=jax.ShapeDtypeStruct((b_lat, s, ow), BF16),
        compiler_params=_cparams("arbitrary", "arbitrary", "arbitrary"),
        name="attn_latent",
    )(*sink_args, q, k, v, k, v)
    if not need_ctx:
        return o_lat, None

    common["tq"] = ctx_len
    ctx_specs = smem + [
        pl.BlockSpec((1, ctx_len, qw), lambda b, j, i: (b_lat, b, j)),
        pl.BlockSpec((1, ctx_len, kw), lambda b, j, i: (b_lat, b, j)),
        v_ctx,
    ]
    o_ctx = pl.pallas_call(
        functools.partial(_attn_body, mode="ctx", **common),
        grid=(b_lat, n_pairs // pps, 1),
        in_specs=ctx_specs,
        out_specs=pl.BlockSpec((1, ctx_len, o_w), lambda b, j, i: (0, b, j)),
        out_shape=jax.ShapeDtypeStruct((1, s, ow), BF16),
        compiler_params=_cparams("arbitrary", "arbitrary", "arbitrary"),
        name="attn_context",
    )(*sink_args, q, k, v)
    return o_lat, o_ctx


def _post_body(*refs, moe, b_lat, has_ctx, split_x):
    refs = list(refs)
    o_ref = refs.pop(0)
    o = _pick_stream(o_ref, refs.pop(0), b_lat) if has_ctx else o_ref[0]
    x_ref = refs.pop(0)
    x_in = _pick_stream(x_ref, refs.pop(0), b_lat) if split_x else x_ref[0]
    if moe:
        mod_ref, g_ref, wo_ref, rw_ref, rb_ref, xo_ref, h_ref, route_ref = refs
    else:
        mod_ref, g_ref, wo_ref, wgu_ref, wd_ref, xo_ref = refs
    y = _dot(o, wo_ref[...])
    x = x_in + mod_ref[0, 2:3, :] * y
    h = _norm_mod(x, g_ref[...], mod_ref[0, 3:4, :], mod_ref[0, 4:5, :])
    h_hi = h.astype(BF16)
    if not moe:
        acc = None
        for c in range(D_FF // FF_CHUNK):
            gate = _dot(h_hi, wgu_ref[:, c * FF_CHUNK:(c + 1) * FF_CHUNK])
            up = _dot(h_hi, wgu_ref[:, D_FF + c * FF_CHUNK:D_FF + (c + 1) * FF_CHUNK])
            act = (_silu(gate) * up).astype(BF16)
            part = _dot(act, wd_ref[c * FF_CHUNK:(c + 1) * FF_CHUNK, :])
            acc = part if acc is None else acc + part
        xo_ref[0] = x + mod_ref[0, 5:6, :] * acc
        return
    xo_ref[0] = x
    _store_rows_as_tiles(h_ref, h)
    h_lo = (h - h_hi.astype(F32)).astype(BF16)
    both = _dot(h_hi, rw_ref[...])
    logits = both[:, :LANES] + both[:, LANES:] + _dot(h_lo, rw_ref[:, :LANES]) + rb_ref[...]
    lane = lax.broadcasted_iota(jnp.int32, logits.shape, 1).astype(F32)
    lg = jnp.where(lane < N_EXPERTS, logits, -jnp.inf)
    v1 = jnp.max(lg, axis=-1, keepdims=True)
    i1 = jnp.min(jnp.where(lg == v1, lane, float(LANES)), axis=-1, keepdims=True)
    lg2 = jnp.where(lane == i1, -jnp.inf, lg)
    v2 = jnp.max(lg2, axis=-1, keepdims=True)
    i2 = jnp.min(jnp.where(lg2 == v2, lane, float(LANES)), axis=-1, keepdims=True)
    e = jnp.exp(v2 - v1)
    g1 = 1.0 / (1.0 + e)
    g2 = e / (1.0 + e)
    route_ref[0] = jnp.where(lane == 0, i1, jnp.where(lane == 1, i2, jnp.where(
        lane == 2, g1, jnp.where(lane == 3, g2, 0.0))))


def _post(o_lat, o_ctx, xs, mods_i, norm_g, w_o, nb, router=None, ffn=None):
    split_x = isinstance(xs, tuple)
    b_lat, s, d = o_lat.shape
    tm = min(ROW_TILE, s)
    moe = router is not None
    assert moe != (ffn is not None)
    has_ctx = o_ctx is not None
    assert has_ctx or not split_x
    resident = lambda arr: pl.BlockSpec(arr.shape, lambda b, i: (0, 0), pipeline_mode=pl.Buffered(1))
    tok = pl.BlockSpec((1, tm, d), lambda b, i: (b, i, 0))
    in_specs = list(_split_specs(o_lat, tm)) if has_ctx else [tok]
    args = [o_lat, o_ctx] if has_ctx else [o_lat]
    n_o = len(args)
    in_specs += list(_split_specs(xs[0], tm)) if split_x else [tok]
    args += list(xs) if split_x else [xs]
    in_specs += [
        pl.BlockSpec((1, 6, d), lambda b, i: (b, 0, 0)),
        pl.BlockSpec((1, d), lambda b, i: (0, 0)),
        resident(w_o),
    ]
    args += [mods_i, norm_g.reshape(1, d), w_o]
    out_specs = [tok]
    out_shape = [jax.ShapeDtypeStruct((b_lat + 1, s, d), F32)]
    if moe:
        in_specs += [pl.BlockSpec((d, 2 * LANES), lambda b, i: (0, 0)),
                     pl.BlockSpec((1, LANES), lambda b, i: (0, 0))]
        args += list(router)
        per_b = s // tm
        out_specs += [pl.BlockSpec((tm * ROW_SUB, LANES), lambda b, i: (b * per_b + i, 0)),
                      pl.BlockSpec((1, tm, LANES), lambda b, i: (b, i, 0))]
        out_shape += [jax.ShapeDtypeStruct((nb * s * ROW_SUB, LANES), F32),
                      jax.ShapeDtypeStruct((nb, s, LANES), F32)]
    else:
        layer = ffn[2]
        in_specs += [pl.BlockSpec((None,) + w.shape[1:], lambda b, i: (layer, 0, 0), pipeline_mode=pl.Buffered(1))
                     for w in ffn[:2]]
        args += list(ffn[:2])
    return pl.pallas_call(
        functools.partial(_post_body, moe=moe, b_lat=b_lat, has_ctx=has_ctx, split_x=split_x),
        grid=(nb, s // tm),
        in_specs=in_specs,
        out_specs=out_specs,
        out_shape=out_shape,
        input_output_aliases={} if split_x else {n_o: 0},
        compiler_params=_cparams("arbitrary", "arbitrary"),
        name="attn_out_router" if moe else "attn_out_ffn",
    )(*args)


ROW_SUB = D_MODEL // LANES


def _store_rows_as_tiles(ref, x):
    n = x.shape[0]
    for sub in range(ROW_SUB):
        ref[pl.ds(sub, n, stride=ROW_SUB), :] = x[:, sub * LANES:(sub + 1) * LANES]


def _load_row_pieces(ref, first_tile, n):
    return [ref[pl.ds(first_tile * ROW_SUB + sub, n, stride=ROW_SUB), :] for sub in range(ROW_SUB)]


def _tile(ref, row8):
    return ref.at[pl.ds(pl.multiple_of(row8, ROW_SUB), ROW_SUB)]


def _tile_dma_wait(src_ref, dst_ref, sem, tiles):
    n = tiles * ROW_SUB
    pltpu.make_async_copy(src_ref.at[pl.ds(0, n)], dst_ref.at[pl.ds(0, n)], sem).wait()


def _dispatch_body(pos_ref, h_ref, xs_in_ref, xs_ref, sem):
    del xs_in_ref
    tm = h_ref.shape[0] // ROW_SUB

    def issue(r, carry):
        src = _tile(h_ref, r * ROW_SUB)
        for k in range(2):
            pltpu.make_async_copy(src, _tile(xs_ref, pos_ref[0, 0, 2 * r + k]), sem).start(priority=k)
        return carry

    lax.fori_loop(0, tm, issue, 0, unroll=8)
    _tile_dma_wait(h_ref, xs_ref, sem, tm)
    _tile_dma_wait(h_ref, xs_ref, sem, tm)


def _moe_dispatch(h_tiles, pos8, p, tm):
    t = h_tiles.shape[0] // ROW_SUB
    return pl.pallas_call(
        _dispatch_body,
        grid=(t // tm,),
        in_specs=[
            pl.BlockSpec((1, 1, 2 * tm), lambda i: (i, 0, 0), memory_space=pltpu.SMEM),
            pl.BlockSpec((tm * ROW_SUB, LANES), lambda i: (i, 0)),
            pl.BlockSpec(memory_space=pl.ANY),
        ],
        out_specs=pl.BlockSpec(memory_space=pl.ANY),
        out_shape=jax.ShapeDtypeStruct((p * ROW_SUB, LANES), F32),
        scratch_shapes=[pltpu.SemaphoreType.DMA(())],
        input_output_aliases={2: 0},
        compiler_params=_cparams("arbitrary"),
        name="moe_dispatch",
    )(pos8.reshape(t // tm, 1, 2 * tm), h_tiles, jnp.zeros((p * ROW_SUB, LANES), F32))


def _moe_body(te_ref, nu_ref, xs_ref, wg_ref, wu_ref, wd_ref, ys_ref):
    i = pl.program_id(0)
    tm = xs_ref.shape[0] // ROW_SUB

    @pl.when(i < nu_ref[0])
    def _():
        x = jnp.concatenate(_load_row_pieces(xs_ref, 0, tm), axis=1).astype(BF16)
        act = (_silu(_dot(x, wg_ref[0])) * _dot(x, wu_ref[0])).astype(BF16)
        _store_rows_as_tiles(ys_ref, _dot(act, wd_ref[0]))

    @pl.when(i >= nu_ref[0])
    def _():
        ys_ref[...] = jnp.zeros_like(ys_ref)


def _moe_experts(xs_sorted, tile_expert, n_used, w_gu, w_d, layer, tm):
    p = xs_sorted.shape[0] // ROW_SUB
    f, d = w_d.shape[2:]
    rows = pl.BlockSpec((tm * ROW_SUB, LANES), lambda i, te, nu: (i, 0))
    grid_spec = pltpu.PrefetchScalarGridSpec(
        num_scalar_prefetch=2,
        grid=(p // tm,),
        in_specs=[
            rows,
            pl.BlockSpec((None, 1, d, f), lambda i, te, nu: (layer, te[i], 0, 0)),
            pl.BlockSpec((None, 1, d, f), lambda i, te, nu: (layer, te[i], 0, 1)),
            pl.BlockSpec((None, 1, f, d), lambda i, te, nu: (layer, te[i], 0, 0)),
        ],
        out_specs=rows,
    )
    return pl.pallas_call(
        _moe_body,
        grid_spec=grid_spec,
        out_shape=jax.ShapeDtypeStruct((p * ROW_SUB, LANES), F32),
        compiler_params=_cparams("arbitrary"),
        name="moe_experts",
    )(tile_expert, n_used, xs_sorted, w_gu, w_gu, w_d)


def _combine_body(*refs, final):
    if final:
        pos_ref, pos_next_ref, x_ref, route_ref, mod_ref, g_ref, ys_ref, o_ref, ybuf, sems = refs
    else:
        pos_ref, pos_next_ref, x_ref, route_ref, mod_ref, ys_ref, o_ref, ybuf, sems = refs
    tm, d = x_ref.shape[1:]
    step = pl.program_id(0) * pl.num_programs(1) + pl.program_id(1)
    n_steps = pl.num_programs(0) * pl.num_programs(1)
    slot = lax.rem(step, 2)

    def gather(p_ref, into):
        def issue(r, carry):
            for k in range(2):
                pltpu.make_async_copy(_tile(ys_ref, p_ref[0, 0, 2 * r + k]),
                                      _tile(ybuf.at[into], (k * tm + r) * ROW_SUB), sems.at[into]).start(priority=k)
            return carry

        lax.fori_loop(0, tm, issue, 0, unroll=8)

    @pl.when(step == 0)
    def _():
        gather(pos_ref, 0)

    @pl.when(step + 1 < n_steps)
    def _():
        gather(pos_next_ref, 1 - slot)

    mine = ybuf.at[slot]
    _tile_dma_wait(ys_ref, mine, sems.at[slot], 2 * tm)
    route = route_ref[0]
    g1, g2 = route[:, 2:3], route[:, 3:4]
    y1, y2 = _load_row_pieces(mine, 0, tm), _load_row_pieces(mine, tm, tm)
    cols = [slice(sub * LANES, (sub + 1) * LANES) for sub in range(ROW_SUB)]
    xn = [x_ref[0, :, c] + mod_ref[0, 5:6, c] * (g1 * a + g2 * b) for c, a, b in zip(cols, y1, y2)]
    if final:
        ms = sum(jnp.sum(v * v, axis=-1, keepdims=True) for v in xn) * (1.0 / d)
        scale = lax.rsqrt(ms + NORM_EPS)
        xn = [v * scale * g_ref[:, c] for c, v in zip(cols, xn)]
    for c, v in zip(cols, xn):
        o_ref[0, :, c] = v


def _moe_combine(xs, ys, pos, route, mods_i, nb, final_g=None):
    _, s, d = xs.shape
    tm = min(ROW_TILE, s)
    per_b = s // tm
    final = final_g is not None
    tok = pl.BlockSpec((1, tm, d), lambda b, i: (b, i, 0))
    last = nb * per_b - 1
    pos3 = pos.reshape(nb * per_b, 1, 2 * tm)
    in_specs = [pl.BlockSpec((1, 1, 2 * tm), lambda b, i: (b * per_b + i, 0, 0), memory_space=pltpu.SMEM),
                pl.BlockSpec((1, 1, 2 * tm), lambda b, i: (jnp.minimum(b * per_b + i + 1, last), 0, 0),
                             memory_space=pltpu.SMEM),
                tok,
                pl.BlockSpec((1, tm, LANES), lambda b, i: (b, i, 0)),
                pl.BlockSpec((1, 6, d), lambda b, i: (b, 0, 0))]
    args = [pos3, pos3, xs, route, mods_i]
    if final:
        in_specs.append(pl.BlockSpec((1, d), lambda b, i: (0, 0)))
        args.append(final_g.reshape(1, d))
    in_specs.append(pl.BlockSpec(memory_space=pl.ANY))
    args.append(ys)
    return pl.pallas_call(
        functools.partial(_combine_body, final=final),
        grid=(nb, per_b),
        in_specs=in_specs,
        out_specs=tok,
        out_shape=jax.ShapeDtypeStruct((nb, s, d) if final else xs.shape, F32),
        scratch_shapes=[pltpu.VMEM((2, 2 * tm * ROW_SUB, LANES), F32), pltpu.SemaphoreType.DMA((2,))],
        input_output_aliases={} if final else {2: 0},
        compiler_params=_cparams("arbitrary", "arbitrary"),
        name="moe_combine",
    )(*args)


def _moe_layer(hp, route, xs, mods_i, w_gu, w_d, layer, nb, final_g):
    _, s, d = xs.shape
    t = nb * s
    tm = min(MOE_TILE, s)
    p = 2 * t + N_EXPERTS * tm
    expert = route.reshape(t, LANES)[:, :2].astype(jnp.int32).reshape(2 * t)
    onehot = (expert[:, None] == jnp.arange(N_EXPERTS, dtype=jnp.int32)[None, :]).astype(jnp.int32)
    csum = jnp.cumsum(onehot, axis=0)
    padded = ((csum[-1] + tm - 1) // tm) * tm
    off_end = jnp.cumsum(padded)
    pos = jnp.sum(onehot * ((off_end - padded)[None, :] + csum - 1), axis=1)
    tile_start = jnp.arange(p // tm, dtype=jnp.int32) * tm
    tile_expert = jnp.minimum(jnp.sum(tile_start[:, None] >= off_end[None, :], axis=1), N_EXPERTS - 1)
    n_used = (off_end[-1] // tm).reshape(1).astype(jnp.int32)

    pos8 = pos * ROW_SUB
    xs_sorted = _moe_dispatch(hp, pos8, p, tm)
    ys = _moe_experts(xs_sorted, tile_expert.astype(jnp.int32), n_used, w_gu, w_d, layer, tm)
    return _moe_combine(xs, ys, pos8, route, mods_i, nb, final_g)


def kernel(x, c, ctx, c_ctx, ada_w, ada_b, norm_g, final_norm_g, mla_w_dq, mla_q_norm_g, mla_w_uq, mla_w_dkv, mla_kv_norm_g, mla_w_ukv, mla_w_o, swa_w_qkv, swa_sinks, swa_w_o, ga_w_qkv, ga_q_norm_g, ga_k_norm_g, ga_w_o, ffn_w_gate_up, ffn_w_down, moe_router_w, moe_router_b, moe_w_gate_up, moe_w_down):
    b, s, d = x.shape
    assert d == D_MODEL and b * ctx.shape[1] == s and s % GRID_W == 0
    nb = b + 1
    xs = (x, ctx.reshape(1, s, d))
    c_all = jnp.concatenate([c, c_ctx[None], jnp.zeros((16 - nb, d), F32)], axis=0)
    mods = _mods(c_all, ada_w, ada_b).reshape(DEPTH, 16, 6, d)

    ffn_gu, ffn_dn = ffn_w_gate_up.astype(BF16), ffn_w_down.astype(BF16)
    moe_gu, moe_dn = moe_w_gate_up.astype(BF16), moe_w_down.astype(BF16)
    gqa_tab = _gqa_tables(s)
    mla_tab = _mla_tables(s)
    nope_src, rope_src = _mla_lane_src()

    for i in range(DEPTH):
        need_ctx = i < DEPTH - 1
        mods_i = mods[i]
        kind, j = i % 3, i // 3
        if kind == 0:
            dkv = jnp.concatenate([mla_w_dkv[j], jnp.zeros((d, 1), F32)], axis=1)
            kr_cols = np.where(rope_src >= 0, MLA_KV_LORA + rope_src, MLA_KV_LORA + MLA_QK_ROPE)
            w_a = jnp.concatenate([mla_w_dq[j], dkv[:, :MLA_KV_LORA], dkv[:, kr_cols]], axis=1).astype(BF16)
            qd = MLA_QK_NOPE + MLA_QK_ROPE
            q_src = np.where(nope_src >= 0, nope_src, np.where(rope_src >= 0, MLA_QK_NOPE + rope_src, qd))
            uq = jnp.concatenate([mla_w_uq[j].reshape(MLA_Q_LORA, N_HEADS, qd),
                                  jnp.zeros((MLA_Q_LORA, N_HEADS, 1), F32)], axis=2)
            w_uq = uq[:, :, q_src].reshape(MLA_Q_LORA, N_HEADS * LANES).astype(BF16)
            kvd = MLA_QK_NOPE + MLA_V_DIM
            ukv = jnp.concatenate([mla_w_ukv[j].reshape(MLA_KV_LORA, N_HEADS, kvd),
                                   jnp.zeros((MLA_KV_LORA, N_HEADS, 1), F32)], axis=2)
            k_src = np.where(nope_src >= 0, nope_src, kvd)
            w_ukv = jnp.concatenate([ukv[:, :, k_src].reshape(MLA_KV_LORA, N_HEADS * LANES),
                                     ukv[:, :, MLA_QK_NOPE:kvd].reshape(MLA_KV_LORA, N_HEADS * MLA_V_DIM)],
                                    axis=1).astype(BF16)
            q, k, v = _mla_proj(xs, mods_i, norm_g[i, 0], w_a, mla_q_norm_g[j], mla_kv_norm_g[j],
                                w_uq, w_ukv, mla_tab)
            o = _attention(q, k, v, N_HEADS // 2, 1, True, False, None, need_ctx)
            w_o = mla_w_o[j].astype(BF16)
        elif kind == 1:
            cols, o_rows = _gqa_perm(SWA_KV_HEADS, N_HEADS // SWA_KV_HEADS)
            q, k, v = _gqa_proj(xs, mods_i, norm_g[i, 0], swa_w_qkv[j][:, cols].astype(BF16), gqa_tab,
                                SWA_KV_HEADS, values_t=False)
            o = _attention(q, k, v, SWA_KV_HEADS // 2, N_HEADS // SWA_KV_HEADS, False, True,
                           swa_sinks[j], need_ctx)
            w_o = swa_w_o[j][o_rows].astype(BF16)
        else:
            cols, o_rows = _gqa_perm(GA_KV_HEADS, N_HEADS // GA_KV_HEADS)
            q, k, v = _gqa_proj(xs, mods_i, norm_g[i, 0], ga_w_qkv[j][:, cols].astype(BF16), gqa_tab,
                                GA_KV_HEADS, (ga_q_norm_g[j], ga_k_norm_g[j]))
            o = _attention(q, k, v, GA_KV_HEADS // 2, N_HEADS // GA_KV_HEADS, False, False, None, need_ctx)
            w_o = ga_w_o[j][o_rows].astype(BF16)

        n_tok_b = nb if need_ctx else b
        f = i // 2
        if i % 2 == 0:
            xs, = _post(*o, xs, mods_i, norm_g[i, 1], w_o, n_tok_b, ffn=(ffn_gu, ffn_dn, f))
        else:
            rw = jnp.concatenate([moe_router_w[f], jnp.zeros((d, LANES - N_EXPERTS), F32)], axis=1)
            rw_hi = rw.astype(BF16)
            rw_lo = (rw - rw_hi.astype(F32)).astype(BF16)
            rb = jnp.concatenate([moe_router_b[f], jnp.zeros((LANES - N_EXPERTS,), F32)]).reshape(1, LANES)
            xs, h2, comb = _post(*o, xs, mods_i, norm_g[i, 1], w_o, n_tok_b,
                                 (jnp.concatenate([rw_hi, rw_lo], axis=1), rb))
            xs = _moe_layer(h2, comb, xs, mods_i, moe_gu, moe_dn, f, n_tok_b,
                            None if need_ctx else final_norm_g)
    return xs
```

```python
import functools
import math

import numpy as np
import jax
import jax.numpy as jnp
from jax import lax
from jax.experimental import pallas as pl
from jax.experimental.pallas import tpu as pltpu

F32 = jnp.float32
BF16 = jnp.bfloat16

D_MODEL = 1024
GRID_W = 64
HEAD_DIM = 64
N_HEADS = 16
MLA_Q_LORA = 384
MLA_KV_LORA = 256
MLA_QK_NOPE = 64
MLA_QK_ROPE = 32
MLA_V_DIM = 64
SWA_KV_HEADS = 4
SWA_WINDOW = 128
GA_KV_HEADS = 8
D_FF = 2816
N_EXPERTS = 8
EXPERT_FF = 1408
ROPE_THETA = 10000.0
NORM_EPS = 1e-6
NEG_INF = -1e30
DEPTH = 4
LOG2E = math.log2(math.e)

LANES = 128
VMEM_LIMIT = 56 * 2**20
ROW_TILE = 512
Q_TILE = 512
KEY_CHUNK = 2048
HEADS_PER_STEP = 8
MOE_TILE = 512
FF_CHUNK = 2816


def _cparams(*sem):
    return pltpu.CompilerParams(dimension_semantics=sem, vmem_limit_bytes=VMEM_LIMIT)


def _silu(x):
    return x * (1.0 / (1.0 + jnp.exp(-x)))


def _rms(x, g):
    ms = jnp.mean(x * x, axis=-1, keepdims=True)
    return x * lax.rsqrt(ms + NORM_EPS) * g


def _norm_mod(x, g, shift, scale):
    return _rms(x, g) * (1.0 + scale) + shift


def _dot(a, b):
    return jnp.dot(a, b, preferred_element_type=F32)


def _dot_t(a, b):
    return lax.dot_general(a, b, (((1,), (1,)), ((), ())), preferred_element_type=F32)


def _mods_body(c_ref, w_ref, b_ref, o_ref):
    sc = _silu(c_ref[...]).astype(BF16)
    o_ref[0] = _dot(sc, w_ref[0].astype(BF16)) + b_ref[0]


def _mods(c_all, ada_w, ada_b):
    depth, d, n = ada_w.shape
    rows = c_all.shape[0]
    tn = 1536
    return pl.pallas_call(
        _mods_body,
        grid=(depth, n // tn),
        in_specs=[
            pl.BlockSpec((rows, d), lambda i, j: (0, 0)),
            pl.BlockSpec((1, d, tn), lambda i, j: (i, 0, j)),
            pl.BlockSpec((1, 1, tn), lambda i, j: (i, 0, j)),
        ],
        out_specs=pl.BlockSpec((1, rows, tn), lambda i, j: (i, 0, j)),
        out_shape=jax.ShapeDtypeStruct((depth, rows, n), F32),
        compiler_params=_cparams("arbitrary", "arbitrary"),
        name="adaln_mods",
    )(c_all, ada_w, ada_b.reshape(depth, 1, n))


def _pair_lanes():
    lane = np.arange(LANES)
    is_b = (lane % 64) >= 32
    dim = (lane % 32) + 32 * (lane // 64)
    return is_b, dim


def _gqa_perm(n_kv, group):
    is_b, dim = _pair_lanes()
    nq = N_HEADS * HEAD_DIM
    nk = n_kv * HEAD_DIM
    q_cols, k_cols, o_rows = [], [], []
    nat = np.arange(HEAD_DIM)
    for j in range(n_kv // 2):
        for g in range(group):
            head_a, head_b = (2 * j) * group + g, (2 * j + 1) * group + g
            q_cols.append(np.where(is_b, head_b, head_a) * HEAD_DIM + dim)
            o_rows.append(np.concatenate([head_a * HEAD_DIM + nat, head_b * HEAD_DIM + nat]))
        k_cols.append(nq + np.where(is_b, 2 * j + 1, 2 * j) * HEAD_DIM + dim)
    cols = np.concatenate(q_cols + k_cols + [nq + nk + np.arange(nk)])
    return cols.astype(np.int32), np.concatenate(o_rows).astype(np.int32)


def _angles(s, rot_dim):
    n_freq = rot_dim // 4
    inv = ROPE_THETA ** (-jnp.arange(n_freq, dtype=F32) / n_freq)
    pos = jnp.arange(s)
    rows = (pos // GRID_W).astype(F32)
    cols = (pos % GRID_W).astype(F32)
    return jnp.concatenate([rows[:, None] * inv, cols[:, None] * inv], axis=-1)


def _rope_tables(cos_l, sin_l, q_scale):
    lat = jnp.stack([cos_l * q_scale, sin_l * q_scale, cos_l, sin_l])
    one = jnp.ones_like(cos_l)
    zero = jnp.zeros_like(cos_l)
    ctx = jnp.stack([one * q_scale, zero, one, zero])
    return jnp.stack([lat, ctx])


def _gqa_tables(s):
    ang = _angles(s, HEAD_DIM)
    lane = np.arange(LANES)
    idx = lane % 32
    sign = jnp.asarray(np.where(lane < 64, -1.0, 1.0), F32)
    return _rope_tables(jnp.cos(ang)[:, idx], jnp.sin(ang)[:, idx] * sign, HEAD_DIM ** -0.5 * LOG2E)


def _mla_lane_src():
    lane = np.arange(LANES)
    nope = np.where((lane >= 16) & (lane < 64), lane - 16,
                    np.where((lane >= 80) & (lane < 96), 48 + lane - 80, -1))
    rope = np.where(lane < 16, lane, np.where((lane >= 64) & (lane < 80), 16 + lane - 64, -1))
    return nope, rope


def _mla_tables(s):
    ang = _angles(s, MLA_QK_ROPE)
    lane = np.arange(LANES)
    is_x1 = lane < 16
    is_x2 = (lane >= 64) & (lane < 80)
    idx = np.where(is_x1, lane, np.where(is_x2, lane - 64, 0))
    rot = jnp.asarray(is_x1 | is_x2)
    sign = jnp.asarray(np.where(is_x1, -1.0, np.where(is_x2, 1.0, 0.0)), F32)
    cos_l = jnp.where(rot, jnp.cos(ang)[:, idx], 1.0)
    sin_l = jnp.sin(ang)[:, idx] * sign
    return _rope_tables(cos_l, sin_l, (MLA_QK_NOPE + MLA_QK_ROPE) ** -0.5 * LOG2E)


def _rope(blk, cos, sin):
    return blk * cos + pltpu.roll(blk, 64, 1) * sin


VT_ROWS = 80
DEN_ROW = 64


def _ones_rows(n_cols):
    r = lax.broadcasted_iota(jnp.int32, (VT_ROWS - DEN_ROW, n_cols), 0)
    return jnp.where(r == 0, 1.0, 0.0)


def _store_values_t(vt_ref, v, n_heads):
    v_t = v.T
    tail = _ones_rows(v.shape[0]).astype(vt_ref.dtype)
    for h in range(n_heads):
        vt_ref[0, h * VT_ROWS:h * VT_ROWS + DEN_ROW, :] = v_t[h * DEN_ROW:(h + 1) * DEN_ROW, :].astype(vt_ref.dtype)
        vt_ref[0, h * VT_ROWS + DEN_ROW:(h + 1) * VT_ROWS, :] = tail


def _gqa_proj_body(*refs, nqb, nkb, qk_norm, values_t):
    if qk_norm:
        x_ref, mod_ref, g_ref, w_ref, tab_ref, gq_ref, gk_ref, ind_ref, q_ref, k_ref, v_ref = refs
    else:
        x_ref, mod_ref, g_ref, w_ref, tab_ref, q_ref, k_ref, v_ref = refs
    h = _norm_mod(x_ref[0], g_ref[...], mod_ref[0, 0:1, :], mod_ref[0, 1:2, :]).astype(BF16)
    qkv = _dot(h, w_ref[...])
    cq, sq, ck, sk = tab_ref[0, 0], tab_ref[0, 1], tab_ref[0, 2], tab_ref[0, 3]

    def head_norm(blk, gain):
        ssq = _dot((blk * blk).astype(BF16), ind_ref[...])
        return blk * lax.rsqrt(ssq * (1.0 / HEAD_DIM) + NORM_EPS) * gain

    for c in range(nqb):
        blk = qkv[:, c * LANES:(c + 1) * LANES]
        if qk_norm:
            blk = head_norm(blk, gq_ref[...])
        q_ref[0, :, c * LANES:(c + 1) * LANES] = _rope(blk, cq, sq).astype(BF16)
    for c in range(nkb):
        blk = qkv[:, (nqb + c) * LANES:(nqb + c + 1) * LANES]
        if qk_norm:
            blk = head_norm(blk, gk_ref[...])
        k_ref[0, :, c * LANES:(c + 1) * LANES] = _rope(blk, ck, sk).astype(BF16)
    v = qkv[:, (nqb + nkb) * LANES:]
    if values_t:
        _store_values_t(v_ref, v, 2 * nkb)
    else:
        v_ref[0] = v.astype(BF16)


def _gqa_proj(xs, mods_i, norm_g, w_perm, tables, n_kv, qk_gains=None, values_t=True):
    nb, s, d = xs.shape
    tm = min(ROW_TILE, s)
    nqb = N_HEADS * HEAD_DIM // LANES
    nkb = n_kv * HEAD_DIM // LANES
    wn = w_perm.shape[1]
    in_specs = [
        pl.BlockSpec((1, tm, d), lambda b, i: (b, i, 0)),
        pl.BlockSpec((1, 6, d), lambda b, i: (b, 0, 0)),
        pl.BlockSpec((1, d), lambda b, i: (0, 0)),
        pl.BlockSpec((d, wn), lambda b, i: (0, 0)),
        pl.BlockSpec((1, 4, tm, LANES), lambda b, i: (b // (nb - 1), 0, i, 0)),
    ]
    args = [xs, mods_i, norm_g.reshape(1, d), w_perm, tables]
    if qk_gains is not None:
        is_b, dim = _pair_lanes()
        ind = jnp.asarray(is_b[:, None] == is_b[None, :], BF16)
        in_specs += [pl.BlockSpec((1, LANES), lambda b, i: (0, 0)),
                     pl.BlockSpec((1, LANES), lambda b, i: (0, 0)),
                     pl.BlockSpec((LANES, LANES), lambda b, i: (0, 0))]
        args += [qk_gains[0][dim].reshape(1, LANES), qk_gains[1][dim].reshape(1, LANES), ind]
    out_w = (nqb * LANES, nkb * LANES, nkb * LANES)
    out_specs = [pl.BlockSpec((1, tm, w), lambda b, i: (b, i, 0)) for w in out_w]
    out_shape = [jax.ShapeDtypeStruct((nb, s, w), BF16) for w in out_w]
    if values_t:
        out_specs[2] = pl.BlockSpec((1, n_kv * VT_ROWS, tm), lambda b, i: (b, 0, i))
        out_shape[2] = jax.ShapeDtypeStruct((nb, n_kv * VT_ROWS, s), BF16)
    return pl.pallas_call(
        functools.partial(_gqa_proj_body, nqb=nqb, nkb=nkb, qk_norm=qk_gains is not None, values_t=values_t),
        grid=(nb, s // tm),
        in_specs=in_specs,
        out_specs=out_specs,
        out_shape=out_shape,
        compiler_params=_cparams("arbitrary", "arbitrary"),
        name="gqa_proj",
    )(*args)


def _split_specs(lat, tm):
    b_lat, s, w = lat.shape
    last_i = s // tm - 1
    return (pl.BlockSpec((1, tm, w), lambda b, i: (jnp.minimum(b, b_lat - 1), jnp.where(b < b_lat, i, last_i), 0)),
            pl.BlockSpec((1, tm, w), lambda b, i: (0, jnp.where(b < b_lat, 0, i), 0)))


def _pick_stream(lat_ref, ctx_ref, b_lat):
    return jnp.where(pl.program_id(0) == b_lat, ctx_ref[0], lat_ref[0])


def _mla_proj_body(*refs, split_x):
    refs = list(refs)
    x_ref = refs.pop(0)
    x = _pick_stream(x_ref, refs.pop(0), pl.num_programs(0) - 1) if split_x else x_ref[0]
    mod_ref, g_ref, wa_ref, gq_ref, gkv_ref, wuq_ref, wukv_ref, tab_ref, q_ref, k_ref, v_ref = refs
    h = _norm_mod(x, g_ref[...], mod_ref[0, 0:1, :], mod_ref[0, 1:2, :]).astype(BF16)
    a = _dot(h, wa_ref[...])
    qn = _rms(a[:, :MLA_Q_LORA], gq_ref[...]).astype(BF16)
    cn = _rms(a[:, MLA_Q_LORA:MLA_Q_LORA + MLA_KV_LORA], gkv_ref[...]).astype(BF16)
    kr = a[:, MLA_Q_LORA + MLA_KV_LORA:]
    cq, sq, ck, sk = tab_ref[0, 0], tab_ref[0, 1], tab_ref[0, 2], tab_ref[0, 3]
    q = _dot(qn, wuq_ref[...])
    kv = _dot(cn, wukv_ref[...])
    kr = _rope(kr, ck, sk)
    for hh in range(N_HEADS):
        sl = slice(hh * LANES, (hh + 1) * LANES)
        q_ref[0, :, sl] = _rope(q[:, sl], cq, sq).astype(BF16)
        k_ref[0, :, sl] = (kv[:, sl] + kr).astype(BF16)
    _store_values_t(v_ref, kv[:, N_HEADS * LANES:], N_HEADS)


def _mla_proj(xs, mods_i, norm_g, w_a, gq, gkv, w_uq, w_ukv, tables):
    split_x = isinstance(xs, tuple)
    if split_x:
        (b_lat, s, d), nb = xs[0].shape, xs[0].shape[0] + 1
    else:
        nb, s, d = xs.shape
    tm = min(ROW_TILE, s)
    full = lambda arr: pl.BlockSpec(arr.shape, lambda b, i: (0,) * arr.ndim)
    gq = gq.reshape(1, -1)
    gkv = gkv.reshape(1, -1)
    g = norm_g.reshape(1, d)
    qk_w = N_HEADS * LANES
    vt_rows = N_HEADS * VT_ROWS
    x_specs = list(_split_specs(xs[0], tm)) if split_x else [pl.BlockSpec((1, tm, d), lambda b, i: (b, i, 0))]
    x_args = list(xs) if split_x else [xs]
    return pl.pallas_call(
        functools.partial(_mla_proj_body, split_x=split_x),
        grid=(nb, s // tm),
        in_specs=x_specs + [
            pl.BlockSpec((1, 6, d), lambda b, i: (b, 0, 0)),
            full(g), full(w_a), full(gq), full(gkv), full(w_uq), full(w_ukv),
            pl.BlockSpec((1, 4, tm, LANES), lambda b, i: (b // (nb - 1), 0, i, 0)),
        ],
        out_specs=[pl.BlockSpec((1, tm, qk_w), lambda b, i: (b, i, 0)),
                   pl.BlockSpec((1, tm, qk_w), lambda b, i: (b, i, 0)),
                   pl.BlockSpec((1, vt_rows, tm), lambda b, i: (b, 0, i))],
        out_shape=[jax.ShapeDtypeStruct((nb, s, qk_w), BF16), jax.ShapeDtypeStruct((nb, s, qk_w), BF16),
                   jax.ShapeDtypeStruct((nb, vt_rows, s), BF16)],
        compiler_params=_cparams("arbitrary", "arbitrary"),
        name="mla_proj",
    )(*x_args, mods_i, g, w_a, gq, gkv, w_uq, w_ukv, tables)


def _attn_body(*refs, pps, group, ql, klw, split, mode, tq, band, seq, use_sink, values_t):
    refs = list(refs)
    sink_ref = refs.pop(0) if use_sink else None
    q_ref = refs.pop(0)
    if mode != "ctx":
        kl_ref, vl_ref = refs.pop(0), refs.pop(0)
    kc_ref, vc_ref, o_ref = refs
    j = pl.program_id(1)
    qi = pl.program_id(2)
    lane = lax.broadcasted_iota(jnp.int32, (1, LANES), 1)
    in_a = (lane & 63) < 32

    def transpose_values(v):
        v_t = v.astype(F32).T
        tail = _ones_rows(v.shape[0])
        return [jnp.concatenate([v_t[h * DEN_ROW:(h + 1) * DEN_ROW], tail], axis=0).astype(BF16)
                for h in range(2)]

    ctx_len = kc_ref.shape[1]
    valid = None
    if mode == "dense":
        n_lat = seq
    elif mode == "window":
        start = jnp.clip(qi * tq - SWA_WINDOW, 0, seq - band)
        start = pl.multiple_of(start, LANES)
        kpos = start + lax.broadcasted_iota(jnp.int32, (band, 1), 0)
        qpos = qi * tq + lax.broadcasted_iota(jnp.int32, (1, tq), 1)
        valid = jnp.abs(qpos - kpos) <= SWA_WINDOW
        n_lat = band
    else:
        n_lat = 0

    chunks = []
    for pp in range(pps):
        ks = slice(pp * klw, (pp + 1) * klw)
        mine = []
        if values_t:
            head_rows = lambda half, pp=pp: slice((2 * pp + half) * VT_ROWS, (2 * pp + half + 1) * VT_ROWS)
            mine.append((lambda ks=ks: kc_ref[0, :, ks], lambda half, hr=head_rows: vc_ref[0, hr(half), :], None))
            for c0 in range(0, n_lat, KEY_CHUNK):
                c1 = min(c0 + KEY_CHUNK, n_lat)
                mine.append((lambda ks=ks, c0=c0, c1=c1: kl_ref[0, c0:c1, ks],
                             lambda half, hr=head_rows, c0=c0, c1=c1: vl_ref[0, hr(half), c0:c1], None))
        else:
            ps = slice(pp * LANES, (pp + 1) * LANES)
            vc = transpose_values(vc_ref[0, :, ps])
            mine.append((lambda ks=ks: kc_ref[0, :, ks], lambda half, vc=vc: vc[half], None))
            if mode != "ctx":
                vl = transpose_values(vl_ref[0, pl.ds(start, band), ps])
                for c0 in range(0, n_lat, KEY_CHUNK):
                    c1 = min(c0 + KEY_CHUNK, n_lat)
                    mine.append((lambda ks=ks, c0=c0, c1=c1: kl_ref[0, pl.ds(start + c0, c1 - c0), ks],
                                 lambda half, vl=vl, c0=c0, c1=c1: vl[half][:, c0:c1],
                                 valid[c0:c1]))
        chunks.append(mine)
    n_chunks = len(chunks[0])

    def head_q(pp, g, half):
        q = q_ref[0, :, (pp * group + g) * ql:(pp * group + g + 1) * ql]
        if split:
            return q[:, half * LANES:(half + 1) * LANES]
        return jnp.where(in_a if half == 0 else jnp.logical_not(in_a), q, jnp.zeros_like(q))

    def scores(item):
        pp, g, half, ci = item
        load_keys, _, ok = chunks[pp][ci]
        kch = load_keys()
        if split:
            kch = kch[:, half * LANES:(half + 1) * LANES]
        s = _dot_t(kch, head_q(pp, g, half))
        return s if ok is None else jnp.where(ok, s, NEG_INF)

    items = [(pp, g, half, ci) for pp in range(pps) for g in range(group) for half in range(2)
             for ci in range(n_chunks)]
    last = n_chunks - 1

    def run(lagged):
        outs = []
        acc = None
        bad = jnp.zeros((1, tq), F32)

        def values_step(pend, acc, bad):
            (pp, g, half, ci), p, scale, m_end, sink = pend
            pv = _dot(chunks[pp][ci][1](half), p)
            if ci == 0:
                acc = pv
            elif lagged:
                acc = (acc + pv) * scale
            else:
                acc = acc * scale + pv
            if ci == last:
                den = acc[DEN_ROW:DEN_ROW + 1, :]
                if sink is not None:
                    den = den + jnp.exp2(sink - m_end)
                out = acc[:DEN_ROW] * (1.0 / den)
                if lagged:
                    finite = (out - out) == 0.0
                    bad = bad + jnp.sum(jnp.where(finite, 0.0, 1.0), axis=0, keepdims=True)
                outs.append(out)
                if half == 1:
                    o_t = jnp.concatenate(outs, axis=0)
                    o_ref[0, :, (pp * group + g) * LANES:(pp * group + g + 1) * LANES] = o_t.T.astype(BF16)
                    outs.clear()
            return acc, bad

        s_next = scores(items[0])
        pending = None
        for idx, item in enumerate(items):
            pp, g, half, ci = item
            s = s_next
            if idx + 1 < len(items):
                s_next = scores(items[idx + 1])
            cm = jnp.max(s, axis=0, keepdims=True)
            if ci == 0:
                sink = sink_ref[(2 * (j * pps + pp) + half) * group + g] * LOG2E if use_sink else None
                m = cm if sink is None else jnp.maximum(cm, sink)
                p = jnp.exp2(s - m).astype(BF16)
                scale = None
            else:
                m_new = jnp.maximum(m, cm)
                scale = jnp.exp2(m - m_new)
                p = jnp.exp2(s - (m if lagged else m_new)).astype(BF16)
                m = m_new
            if pending is not None:
                acc, bad = values_step(pending, acc, bad)
            pending = (item, p, scale, m, sink)
        _, bad = values_step(pending, acc, bad)
        return jnp.max(bad)

    if n_chunks == 1:
        run(lagged=False)
        return
    overflowed = run(lagged=True)

    @pl.when(overflowed > 0.0)
    def _():
        run(lagged=False)


def _attention(q, k, v, n_pairs, group, split, window, sinks, need_ctx):
    nb, s, _ = q.shape
    b_lat = nb - 1
    ctx_len = s // b_lat
    ql = q.shape[2] // (n_pairs * group)
    klw = k.shape[2] // n_pairs
    ow = n_pairs * group * LANES
    tq = min(Q_TILE, s)
    band = min(tq + 2 * SWA_WINDOW, s)
    use_sink = sinks is not None
    pps = max(1, HEADS_PER_STEP // (2 * group))
    mode = "window" if window else "dense"
    values_t = not window
    common = dict(pps=pps, group=group, ql=ql, klw=klw, split=split, tq=tq, band=band, seq=s,
                  use_sink=use_sink, values_t=values_t)
    smem = [pl.BlockSpec(memory_space=pltpu.SMEM)] if use_sink else []
    sink_args = [sinks] if use_sink else []
    qw, kw, o_w = pps * group * ql, pps * klw, pps * group * LANES
    if values_t:
        vr = 2 * pps * VT_ROWS
        v_lat = pl.BlockSpec((1, vr, s), lambda b, j, i: (b, j, 0))
        v_ctx = pl.BlockSpec((1, vr, ctx_len), lambda b, j, i: (b_lat, j, b))
    else:
        v_lat = pl.BlockSpec((1, s, pps * LANES), lambda b, j, i: (b, 0, j))
        v_ctx = pl.BlockSpec((1, ctx_len, pps * LANES), lambda b, j, i: (b_lat, b, j))

    lat_specs = smem + [
        pl.BlockSpec((1, tq, qw), lambda b, j, i: (b, i, j)),
        pl.BlockSpec((1, s, kw), lambda b, j, i: (b, 0, j)),
        v_lat,
        pl.BlockSpec((1, ctx_len, kw), lambda b, j, i: (b_lat, b, j)),
        v_ctx,
    ]
    o_lat = pl.pallas_call(
        functools.partial(_attn_body, mode=mode, **common),
        grid=(b_lat, n_pairs // pps, s // tq),
        in_specs=lat_specs,
        out_specs=pl.BlockSpec((1, tq, o_w), lambda b, j, i: (b, i, j)),
        out_shape=jax.ShapeDtypeStruct((b_lat, s, ow), BF16),
        compiler_params=_cparams("arbitrary", "arbitrary", "arbitrary"),
        name="attn_latent",
    )(*sink_args, q, k, v, k, v)
    if not need_ctx:
        return o_lat, None

    common["tq"] = ctx_len
    ctx_specs = smem + [
        pl.BlockSpec((1, ctx_len, qw), lambda b, j, i: (b_lat, b, j)),
        pl.BlockSpec((1, ctx_len, kw), lambda b, j, i: (b_lat, b, j)),
        v_ctx,
    ]
    o_ctx = pl.pallas_call(
        functools.partial(_attn_body, mode="ctx", **common),
        grid=(b_lat, n_pairs // pps, 1),
        in_specs=ctx_specs,
        out_specs=pl.BlockSpec((1, ctx_len, o_w), lambda b, j, i: (0, b, j)),
        out_shape=jax.ShapeDtypeStruct((1, s, ow), BF16),
        compiler_params=_cparams("arbitrary", "arbitrary", "arbitrary"),
        name="attn_context",
    )(*sink_args, q, k, v)
    return o_lat, o_ctx


def _post_body(*refs, moe, b_lat, has_ctx, split_x):
    refs = list(refs)
    o_ref = refs.pop(0)
    o = _pick_stream(o_ref, refs.pop(0), b_lat) if has_ctx else o_ref[0]
    x_ref = refs.pop(0)
    x_in = _pick_stream(x_ref, refs.pop(0), b_lat) if split_x else x_ref[0]
    if moe:
        mod_ref, g_ref, wo_ref, rw_ref, rb_ref, xo_ref, h_ref, route_ref = refs
    else:
        mod_ref, g_ref, wo_ref, wgu_ref, wd_ref, xo_ref = refs
    y = _dot(o, wo_ref[...])
    x = x_in + mod_ref[0, 2:3, :] * y
    h = _norm_mod(x, g_ref[...], mod_ref[0, 3:4, :], mod_ref[0, 4:5, :])
    h_hi = h.astype(BF16)
    if not moe:
        acc = None
        for c in range(D_FF // FF_CHUNK):
            gate = _dot(h_hi, wgu_ref[:, c * FF_CHUNK:(c + 1) * FF_CHUNK])
            up = _dot(h_hi, wgu_ref[:, D_FF + c * FF_CHUNK:D_FF + (c + 1) * FF_CHUNK])
            act = (_silu(gate) * up).astype(BF16)
            part = _dot(act, wd_ref[c * FF_CHUNK:(c + 1) * FF_CHUNK, :])
            acc = part if acc is None else acc + part
        xo_ref[0] = x + mod_ref[0, 5:6, :] * acc
        return
    xo_ref[0] = x
    _store_rows_as_tiles(h_ref, h)
    h_lo = (h - h_hi.astype(F32)).astype(BF16)
    both = _dot(h_hi, rw_ref[...])
    logits = both[:, :LANES] + both[:, LANES:] + _dot(h_lo, rw_ref[:, :LANES]) + rb_ref[...]
    lane = lax.broadcasted_iota(jnp.int32, logits.shape, 1).astype(F32)
    lg = jnp.where(lane < N_EXPERTS, logits, -jnp.inf)
    v1 = jnp.max(lg, axis=-1, keepdims=True)
    i1 = jnp.min(jnp.where(lg == v1, lane, float(LANES)), axis=-1, keepdims=True)
    lg2 = jnp.where(lane == i1, -jnp.inf, lg)
    v2 = jnp.max(lg2, axis=-1, keepdims=True)
    i2 = jnp.min(jnp.where(lg2 == v2, lane, float(LANES)), axis=-1, keepdims=True)
    e = jnp.exp(v2 - v1)
    g1 = 1.0 / (1.0 + e)
    g2 = e / (1.0 + e)
    route_ref[0] = jnp.where(lane == 0, i1, jnp.where(lane == 1, i2, jnp.where(
        lane == 2, g1, jnp.where(lane == 3, g2, 0.0))))


def _post(o_lat, o_ctx, xs, mods_i, norm_g, w_o, nb, router=None, ffn=None):
    split_x = isinstance(xs, tuple)
    b_lat, s, d = o_lat.shape
    tm = min(ROW_TILE, s)
    moe = router is not None
    assert moe != (ffn is not None)
    has_ctx = o_ctx is not None
    assert has_ctx or not split_x
    resident = lambda arr: pl.BlockSpec(arr.shape, lambda b, i: (0, 0), pipeline_mode=pl.Buffered(1))
    tok = pl.BlockSpec((1, tm, d), lambda b, i: (b, i, 0))
    in_specs = list(_split_specs(o_lat, tm)) if has_ctx else [tok]
    args = [o_lat, o_ctx] if has_ctx else [o_lat]
    n_o = len(args)
    in_specs += list(_split_specs(xs[0], tm)) if split_x else [tok]
    args += list(xs) if split_x else [xs]
    in_specs += [
        pl.BlockSpec((1, 6, d), lambda b, i: (b, 0, 0)),
        pl.BlockSpec((1, d), lambda b, i: (0, 0)),
        resident(w_o),
    ]
    args += [mods_i, norm_g.reshape(1, d), w_o]
    out_specs = [tok]
    out_shape = [jax.ShapeDtypeStruct((b_lat + 1, s, d), F32)]
    if moe:
        in_specs += [pl.BlockSpec((d, 2 * LANES), lambda b, i: (0, 0)),
                     pl.BlockSpec((1, LANES), lambda b, i: (0, 0))]
        args += list(router)
        per_b = s // tm
        out_specs += [pl.BlockSpec((tm * ROW_SUB, LANES), lambda b, i: (b * per_b + i, 0)),
                      pl.BlockSpec((1, tm, LANES), lambda b, i: (b, i, 0))]
        out_shape += [jax.ShapeDtypeStruct((nb * s * ROW_SUB, LANES), F32),
                      jax.ShapeDtypeStruct((nb, s, LANES), F32)]
    else:
        layer = ffn[2]
        in_specs += [pl.BlockSpec((None,) + w.shape[1:], lambda b, i: (layer, 0, 0), pipeline_mode=pl.Buffered(1))
                     for w in ffn[:2]]
        args += list(ffn[:2])
    return pl.pallas_call(
        functools.partial(_post_body, moe=moe, b_lat=b_lat, has_ctx=has_ctx, split_x=split_x),
        grid=(nb, s // tm),
        in_specs=in_specs,
        out_specs=out_specs,
        out_shape=out_shape,
        input_output_aliases={} if split_x else {n_o: 0},
        compiler_params=_cparams("arbitrary", "arbitrary"),
        name="attn_out_router" if moe else "attn_out_ffn",
    )(*args)


ROW_SUB = D_MODEL // LANES


def _store_rows_as_tiles(ref, x):
    n = x.shape[0]
    for sub in range(ROW_SUB):
        ref[pl.ds(sub, n, stride=ROW_SUB), :] = x[:, sub * LANES:(sub + 1) * LANES]


def _load_row_pieces(ref, first_tile, n):
    return [ref[pl.ds(first_tile * ROW_SUB + sub, n, stride=ROW_SUB), :] for sub in range(ROW_SUB)]


def _tile(ref, row8):
    return ref.at[pl.ds(pl.multiple_of(row8, ROW_SUB), ROW_SUB)]


def _tile_dma_wait(src_ref, dst_ref, sem, tiles):
    n = tiles * ROW_SUB
    pltpu.make_async_copy(src_ref.at[pl.ds(0, n)], dst_ref.at[pl.ds(0, n)], sem).wait()


def _dispatch_body(pos_ref, h_ref, xs_in_ref, xs_ref, sem):
    del xs_in_ref
    tm = h_ref.shape[0] // ROW_SUB

    def issue(r, carry):
        src = _tile(h_ref, r * ROW_SUB)
        for k in range(2):
            pltpu.make_async_copy(src, _tile(xs_ref, pos_ref[0, 0, 2 * r + k]), sem).start(priority=k)
        return carry

    lax.fori_loop(0, tm, issue, 0, unroll=8)
    _tile_dma_wait(h_ref, xs_ref, sem, tm)
    _tile_dma_wait(h_ref, xs_ref, sem, tm)


def _moe_dispatch(h_tiles, pos8, p, tm, target=None):
    t = h_tiles.shape[0] // ROW_SUB
    if target is None:
        target = jnp.zeros((p * ROW_SUB, LANES), F32)
    return pl.pallas_call(
        _dispatch_body,
        grid=(t // tm,),
        in_specs=[
            pl.BlockSpec((1, 1, 2 * tm), lambda i: (i, 0, 0), memory_space=pltpu.SMEM),
            pl.BlockSpec((tm * ROW_SUB, LANES), lambda i: (i, 0)),
            pl.BlockSpec(memory_space=pl.ANY),
        ],
        out_specs=pl.BlockSpec(memory_space=pl.ANY),
        out_shape=jax.ShapeDtypeStruct((p * ROW_SUB, LANES), F32),
        scratch_shapes=[pltpu.SemaphoreType.DMA(())],
        input_output_aliases={2: 0},
        compiler_params=_cparams("arbitrary"),
        name="moe_dispatch",
    )(pos8.reshape(t // tm, 1, 2 * tm), h_tiles, target)


def _moe_body(te_ref, nu_ref, xs_ref, wg_ref, wu_ref, wd_ref, ys_ref):
    i = pl.program_id(0)
    tm = xs_ref.shape[0] // ROW_SUB

    @pl.when(i < nu_ref[0])
    def _():
        x = jnp.concatenate(_load_row_pieces(xs_ref, 0, tm), axis=1).astype(BF16)
        act = (_silu(_dot(x, wg_ref[0])) * _dot(x, wu_ref[0])).astype(BF16)
        _store_rows_as_tiles(ys_ref, _dot(act, wd_ref[0]))

    @pl.when(i >= nu_ref[0])
    def _():
        ys_ref[...] = jnp.zeros_like(ys_ref)


def _moe_experts(xs_sorted, tile_expert, n_used, w_gu, w_d, layer, tm):
    p = xs_sorted.shape[0] // ROW_SUB
    f, d = w_d.shape[2:]
    rows = pl.BlockSpec((tm * ROW_SUB, LANES), lambda i, te, nu: (i, 0))
    grid_spec = pltpu.PrefetchScalarGridSpec(
        num_scalar_prefetch=2,
        grid=(p // tm,),
        in_specs=[
            rows,
            pl.BlockSpec((None, 1, d, f), lambda i, te, nu: (layer, te[i], 0, 0)),
            pl.BlockSpec((None, 1, d, f), lambda i, te, nu: (layer, te[i], 0, 1)),
            pl.BlockSpec((None, 1, f, d), lambda i, te, nu: (layer, te[i], 0, 0)),
        ],
        out_specs=rows,
    )
    return pl.pallas_call(
        _moe_body,
        grid_spec=grid_spec,
        out_shape=jax.ShapeDtypeStruct((p * ROW_SUB, LANES), F32),
        compiler_params=_cparams("arbitrary"),
        name="moe_experts",
    )(tile_expert, n_used, xs_sorted, w_gu, w_gu, w_d)


def _combine_body(*refs, final):
    if final:
        pos_ref, pos_next_ref, x_ref, route_ref, mod_ref, g_ref, ys_ref, o_ref, ybuf, sems = refs
    else:
        pos_ref, pos_next_ref, x_ref, route_ref, mod_ref, ys_ref, o_ref, ybuf, sems = refs
    tm, d = x_ref.shape[1:]
    step = pl.program_id(0) * pl.num_programs(1) + pl.program_id(1)
    n_steps = pl.num_programs(0) * pl.num_programs(1)
    slot = lax.rem(step, 2)

    def gather(p_ref, into):
        def issue(r, carry):
            for k in range(2):
                pltpu.make_async_copy(_tile(ys_ref, p_ref[0, 0, 2 * r + k]),
                                      _tile(ybuf.at[into], (k * tm + r) * ROW_SUB), sems.at[into]).start(priority=k)
            return carry

        lax.fori_loop(0, tm, issue, 0, unroll=8)

    @pl.when(step == 0)
    def _():
        gather(pos_ref, 0)

    @pl.when(step + 1 < n_steps)
    def _():
        gather(pos_next_ref, 1 - slot)

    mine = ybuf.at[slot]
    _tile_dma_wait(ys_ref, mine, sems.at[slot], 2 * tm)
    route = route_ref[0]
    g1, g2 = route[:, 2:3], route[:, 3:4]
    y1, y2 = _load_row_pieces(mine, 0, tm), _load_row_pieces(mine, tm, tm)
    cols = [slice(sub * LANES, (sub + 1) * LANES) for sub in range(ROW_SUB)]
    xn = [x_ref[0, :, c] + mod_ref[0, 5:6, c] * (g1 * a + g2 * b) for c, a, b in zip(cols, y1, y2)]
    if final:
        ms = sum(jnp.sum(v * v, axis=-1, keepdims=True) for v in xn) * (1.0 / d)
        scale = lax.rsqrt(ms + NORM_EPS)
        xn = [v * scale * g_ref[:, c] for c, v in zip(cols, xn)]
    for c, v in zip(cols, xn):
        o_ref[0, :, c] = v


def _moe_combine(xs, ys, pos, route, mods_i, nb, final_g=None):
    _, s, d = xs.shape
    tm = min(ROW_TILE, s)
    per_b = s // tm
    final = final_g is not None
    tok = pl.BlockSpec((1, tm, d), lambda b, i: (b, i, 0))
    last = nb * per_b - 1
    pos3 = pos.reshape(nb * per_b, 1, 2 * tm)
    in_specs = [pl.BlockSpec((1, 1, 2 * tm), lambda b, i: (b * per_b + i, 0, 0), memory_space=pltpu.SMEM),
                pl.BlockSpec((1, 1, 2 * tm), lambda b, i: (jnp.minimum(b * per_b + i + 1, last), 0, 0),
                             memory_space=pltpu.SMEM),
                tok,
                pl.BlockSpec((1, tm, LANES), lambda b, i: (b, i, 0)),
                pl.BlockSpec((1, 6, d), lambda b, i: (b, 0, 0))]
    args = [pos3, pos3, xs, route, mods_i]
    if final:
        in_specs.append(pl.BlockSpec((1, d), lambda b, i: (0, 0)))
        args.append(final_g.reshape(1, d))
    in_specs.append(pl.BlockSpec(memory_space=pl.ANY))
    args.append(ys)
    return pl.pallas_call(
        functools.partial(_combine_body, final=final),
        grid=(nb, per_b),
        in_specs=in_specs,
        out_specs=tok,
        out_shape=jax.ShapeDtypeStruct((nb, s, d) if final else xs.shape, F32),
        scratch_shapes=[pltpu.VMEM((2, 2 * tm * ROW_SUB, LANES), F32), pltpu.SemaphoreType.DMA((2,))],
        input_output_aliases={} if final else {2: 0},
        compiler_params=_cparams("arbitrary", "arbitrary"),
        name="moe_combine",
    )(*args)


def _moe_layer(hp, route, xs, mods_i, w_gu, w_d, layer, nb, final_g, sorted_buf):
    n_all, s, d = xs.shape
    t = nb * s
    tm = min(MOE_TILE, s)
    p = 2 * n_all * s + N_EXPERTS * tm
    expert = route.reshape(t, LANES)[:, :2].astype(jnp.int32).reshape(2 * t)
    onehot = (expert[:, None] == jnp.arange(N_EXPERTS, dtype=jnp.int32)[None, :]).astype(jnp.int32)
    csum = jnp.cumsum(onehot, axis=0)
    padded = ((csum[-1] + tm - 1) // tm) * tm
    off_end = jnp.cumsum(padded)
    pos = jnp.sum(onehot * ((off_end - padded)[None, :] + csum - 1), axis=1)
    tile_start = jnp.arange(p // tm, dtype=jnp.int32) * tm
    tile_expert = jnp.minimum(jnp.sum(tile_start[:, None] >= off_end[None, :], axis=1), N_EXPERTS - 1)
    n_used = (off_end[-1] // tm).reshape(1).astype(jnp.int32)

    pos8 = pos * ROW_SUB
    xs_sorted = _moe_dispatch(hp, pos8, p, tm, sorted_buf)
    ys = _moe_experts(xs_sorted, tile_expert.astype(jnp.int32), n_used, w_gu, w_d, layer, tm)
    return _moe_combine(xs, ys, pos8, route, mods_i, nb, final_g), xs_sorted


def kernel(x, c, ctx, c_ctx, ada_w, ada_b, norm_g, final_norm_g, mla_w_dq, mla_q_norm_g, mla_w_uq, mla_w_dkv, mla_kv_norm_g, mla_w_ukv, mla_w_o, swa_w_qkv, swa_sinks, swa_w_o, ga_w_qkv, ga_q_norm_g, ga_k_norm_g, ga_w_o, ffn_w_gate_up, ffn_w_down, moe_router_w, moe_router_b, moe_w_gate_up, moe_w_down):
    b, s, d = x.shape
    assert d == D_MODEL and b * ctx.shape[1] == s and s % GRID_W == 0
    nb = b + 1
    xs = (x, ctx.reshape(1, s, d))
    c_all = jnp.concatenate([c, c_ctx[None], jnp.zeros((16 - nb, d), F32)], axis=0)
    mods = _mods(c_all, ada_w, ada_b).reshape(DEPTH, 16, 6, d)

    ffn_gu, ffn_dn = ffn_w_gate_up.astype(BF16), ffn_w_down.astype(BF16)
    moe_gu, moe_dn = moe_w_gate_up.astype(BF16), moe_w_down.astype(BF16)
    gqa_tab = _gqa_tables(s)
    mla_tab = _mla_tables(s)
    nope_src, rope_src = _mla_lane_src()
    sorted_buf = None

    for i in range(DEPTH):
        need_ctx = i < DEPTH - 1
        mods_i = mods[i]
        kind, j = i % 3, i // 3
        if kind == 0:
            dkv = jnp.concatenate([mla_w_dkv[j], jnp.zeros((d, 1), F32)], axis=1)
            kr_cols = np.where(rope_src >= 0, MLA_KV_LORA + rope_src, MLA_KV_LORA + MLA_QK_ROPE)
            w_a = jnp.concatenate([mla_w_dq[j], dkv[:, :MLA_KV_LORA], dkv[:, kr_cols]], axis=1).astype(BF16)
            qd = MLA_QK_NOPE + MLA_QK_ROPE
            q_src = np.where(nope_src >= 0, nope_src, np.where(rope_src >= 0, MLA_QK_NOPE + rope_src, qd))
            uq = jnp.concatenate([mla_w_uq[j].reshape(MLA_Q_LORA, N_HEADS, qd),
                                  jnp.zeros((MLA_Q_LORA, N_HEADS, 1), F32)], axis=2)
            w_uq = uq[:, :, q_src].reshape(MLA_Q_LORA, N_HEADS * LANES).astype(BF16)
            kvd = MLA_QK_NOPE + MLA_V_DIM
            ukv = jnp.concatenate([mla_w_ukv[j].reshape(MLA_KV_LORA, N_HEADS, kvd),
                                   jnp.zeros((MLA_KV_LORA, N_HEADS, 1), F32)], axis=2)
            k_src = np.where(nope_src >= 0, nope_src, kvd)
            w_ukv = jnp.concatenate([ukv[:, :, k_src].reshape(MLA_KV_LORA, N_HEADS * LANES),
                                     ukv[:, :, MLA_QK_NOPE:kvd].reshape(MLA_KV_LORA, N_HEADS * MLA_V_DIM)],
                                    axis=1).astype(BF16)
            q, k, v = _mla_proj(xs, mods_i, norm_g[i, 0], w_a, mla_q_norm_g[j], mla_kv_norm_g[j],
                                w_uq, w_ukv, mla_tab)
            o = _attention(q, k, v, N_HEADS // 2, 1, True, False, None, need_ctx)
            w_o = mla_w_o[j].astype(BF16)
        elif kind == 1:
            cols, o_rows = _gqa_perm(SWA_KV_HEADS, N_HEADS // SWA_KV_HEADS)
            q, k, v = _gqa_proj(xs, mods_i, norm_g[i, 0], swa_w_qkv[j][:, cols].astype(BF16), gqa_tab,
                                SWA_KV_HEADS, values_t=False)
            o = _attention(q, k, v, SWA_KV_HEADS // 2, N_HEADS // SWA_KV_HEADS, False, True,
                           swa_sinks[j], need_ctx)
            w_o = swa_w_o[j][o_rows].astype(BF16)
        else:
            cols, o_rows = _gqa_perm(GA_KV_HEADS, N_HEADS // GA_KV_HEADS)
            q, k, v = _gqa_proj(xs, mods_i, norm_g[i, 0], ga_w_qkv[j][:, cols].astype(BF16), gqa_tab,
                                GA_KV_HEADS, (ga_q_norm_g[j], ga_k_norm_g[j]))
            o = _attention(q, k, v, GA_KV_HEADS // 2, N_HEADS // GA_KV_HEADS, False, False, None, need_ctx)
            w_o = ga_w_o[j][o_rows].astype(BF16)

        n_tok_b = nb if need_ctx else b
        f = i // 2
        if i % 2 == 0:
            xs, = _post(*o, xs, mods_i, norm_g[i, 1], w_o, n_tok_b, ffn=(ffn_gu, ffn_dn, f))
        else:
            rw = jnp.concatenate([moe_router_w[f], jnp.zeros((d, LANES - N_EXPERTS), F32)], axis=1)
            rw_hi = rw.astype(BF16)
            rw_lo = (rw - rw_hi.astype(F32)).astype(BF16)
            rb = jnp.concatenate([moe_router_b[f], jnp.zeros((LANES - N_EXPERTS,), F32)]).reshape(1, LANES)
            xs, h2, comb = _post(*o, xs, mods_i, norm_g[i, 1], w_o, n_tok_b,
                                 (jnp.concatenate([rw_hi, rw_lo], axis=1), rb))
            xs, sorted_buf = _moe_layer(h2, comb, xs, mods_i, moe_gu, moe_dn, f, n_tok_b,
                                        None if need_ctx else final_norm_g, sorted_buf)
    return xs
```
